```python
import math
import jax
import jax.numpy as jnp
from jax import lax
import numpy as np

D_MODEL = 1024
BATCH = 4
SEQ = 4096
DEPTH = 2

GRID_W = 64
CTX_LEN = 256
EPS = 1e-6
N_MOD = 6

MLA_HEADS = 4
Q_LORA = 256
KV_LORA = 128
QK_NOPE = 64
QK_ROPE = 32
V_HEAD = 64
MLA_WIDTH = MLA_HEADS * V_HEAD
MLA_SCALE = (QK_NOPE + QK_ROPE) ** -0.5
ROPE_THETA = 10000.0
Q_BLOCK = 128

SC_WIDTH = 256
SC_KERNEL = 3

SSD_HEADS = 8
SSD_HEAD_DIM = 64
SSD_WIDTH = SSD_HEADS * SSD_HEAD_DIM
SSD_GROUPS = 2
SSD_STATE = 64
SSD_CONV = 3
SSD_CHUNK = 128
SSD_GN = SSD_GROUPS * SSD_STATE
SSD_XBC = SSD_WIDTH + 2 * SSD_GN

D_MIX = MLA_WIDTH + SC_WIDTH + SSD_WIDTH
IN_MLA = Q_LORA + KV_LORA + QK_ROPE
IN_SC = 3 * SC_WIDTH
IN_SSD = SSD_WIDTH + SSD_XBC + 2 * SSD_HEADS
D_IN = IN_MLA + IN_SC + IN_SSD
D_FF = 4 * D_MODEL

kernel_name = 'hybrid_mla_shortconv_ssd_dit_block'


def _rms_norm(x, g):
    xf = x.astype(jnp.float32)
    y = xf * lax.rsqrt(jnp.mean(xf * xf, axis=-1, keepdims=True) + EPS)
    return (y * g.astype(jnp.float32)).astype(x.dtype)


def _modulate(h, shift, scale):
    return h * (1 + scale) + shift


def _dwconv(x, w):
    k = w.shape[0]
    return lax.conv_general_dilated(
        x, w[:, None, :].astype(x.dtype), window_strides=(1,),
        padding=[(k // 2, k // 2)], dimension_numbers=('NWC', 'WIO', 'NWC'),
        feature_group_count=x.shape[-1])


def _axial_rope_tables(rows):
    half = QK_ROPE // 2
    inv_freq = ROPE_THETA ** (-jnp.arange(0, half, 2, dtype=jnp.float32) / half)
    row = jnp.repeat(jnp.arange(rows, dtype=jnp.float32), GRID_W)
    col = jnp.tile(jnp.arange(GRID_W, dtype=jnp.float32), rows)
    ang_r = row[:, None] * inv_freq
    ang_c = col[:, None] * inv_freq
    ang = jnp.concatenate([ang_r, ang_r, ang_c, ang_c], axis=-1)
    return jnp.cos(ang), jnp.sin(ang)


def _apply_rope(x, cos, sin):
    xf = x.astype(jnp.float32)
    a, b, c, d = jnp.split(xf, 4, axis=-1)
    rot = jnp.concatenate([-b, a, -d, c], axis=-1)
    return (xf * cos + rot * sin).astype(x.dtype)


def _split_in(u):
    return u[..., :IN_MLA], u[..., IN_MLA:IN_MLA + IN_SC], u[..., IN_MLA + IN_SC:]


def _mla_kv(u, kv_g, w_ukv, rope):
    b, l, _ = u.shape
    ckv = _rms_norm(u[..., Q_LORA:Q_LORA + KV_LORA], kv_g)
    kv = (ckv @ w_ukv).reshape(b, l, MLA_HEADS, QK_NOPE + V_HEAD)
    k_nope, v = kv[..., :QK_NOPE], kv[..., QK_NOPE:]
    k_rope = u[..., Q_LORA + KV_LORA:]
    if rope is not None:
        k_rope = _apply_rope(k_rope, rope[0], rope[1])
    k_rope = jnp.broadcast_to(k_rope[:, :, None, :], (b, l, MLA_HEADS, QK_ROPE))
    return jnp.concatenate([k_nope, k_rope], axis=-1), v


def _mla_q(u, q_g, w_uq, rope):
    b, l, _ = u.shape
    cq = _rms_norm(u[..., :Q_LORA], q_g)
    q = (cq @ w_uq).reshape(b, l, MLA_HEADS, QK_NOPE + QK_ROPE)
    if rope is not None:
        q_rope = _apply_rope(q[..., QK_NOPE:], rope[0][:, None, :], rope[1][:, None, :])
        q = jnp.concatenate([q[..., :QK_NOPE], q_rope], axis=-1)
    return q


def _softmax_attend(q, k, v):
    s = jnp.einsum('bqhd,bkhd->bhqk', q, k).astype(jnp.float32) * MLA_SCALE
    p = jax.nn.softmax(s, axis=-1).astype(v.dtype)
    return jnp.einsum('bhqk,bkhd->bqhd', p, v)


def _blocked_attention(q, k, v):
    b, l, h, d = q.shape
    nb = l // Q_BLOCK
    qb = q.reshape(b, nb, Q_BLOCK, h, d).transpose(1, 0, 2, 3, 4)
    o = lax.map(lambda qi: _softmax_attend(qi, k, v), qb)
    return o.transpose(1, 0, 2, 3, 4).reshape(b, l, h * V_HEAD)


def _short_conv(u, w):
    gate_b, gate_c, val = jnp.split(u, 3, axis=-1)
    return gate_b * _dwconv(gate_c * val, w)


def _segsum_exp(cs):
    t = cs.shape[-1]
    diff = cs[..., :, None] - cs[..., None, :]
    mask = jnp.tril(jnp.ones((t, t), dtype=bool))
    return jnp.exp(jnp.where(mask, diff, -jnp.inf))


def _ssd_chunked(xs, dt, a, bm, cm, h0):
    b, l, h, p = xs.shape
    nc = l // SSD_CHUNK
    rep = h // SSD_GROUPS
    f32 = jnp.float32
    xdt = (xs.astype(f32) * dt[..., None]).reshape(b, nc, SSD_CHUNK, h, p)
    bc = jnp.repeat(bm.astype(f32), rep, axis=2).reshape(b, nc, SSD_CHUNK, h, SSD_STATE)
    cc = jnp.repeat(cm.astype(f32), rep, axis=2).reshape(b, nc, SSD_CHUNK, h, SSD_STATE)
    la = (dt * a).reshape(b, nc, SSD_CHUNK, h).transpose(0, 3, 1, 2)
    cs = jnp.cumsum(la, axis=-1)
    scores = jnp.einsum('bclhn,bcshn->bhcls', cc, bc) * _segsum_exp(cs)
    y_diag = jnp.einsum('bhcls,bcshp->bclhp', scores, xdt)
    decay_states = jnp.exp(cs[..., -1:] - cs).transpose(0, 2, 3, 1)
    states = jnp.einsum('bclhn,bclhp->bchpn', bc, xdt * decay_states[..., None])
    states = jnp.concatenate([h0[:, None], states], axis=1)
    chunk_tot = jnp.pad(cs[..., -1], ((0, 0), (0, 0), (1, 0)))
    decay_chunk = _segsum_exp(jnp.cumsum(chunk_tot, axis=-1))
    new_states = jnp.einsum('bhzc,bchpn->bzhpn', decay_chunk, states)
    state_decay = jnp.exp(cs).transpose(0, 2, 3, 1)[..., None]
    y_off = jnp.einsum('bclhn,bchpn->bclhp', cc, new_states[:, :-1]) * state_decay
    y = (y_diag + y_off).reshape(b, l, h, p)
    return y, new_states[:, -1]


def _ssd_prepare(u, conv_w, conv_b, dt_bias):
    b, l, _ = u.shape
    z = u[..., :SSD_WIDTH]
    xbc = jax.nn.silu(_dwconv(u[..., SSD_WIDTH:SSD_WIDTH + SSD_XBC], conv_w) + conv_b)
    xs = xbc[..., :SSD_WIDTH].reshape(b, l, SSD_HEADS, SSD_HEAD_DIM)
    bm = xbc[..., SSD_WIDTH:SSD_WIDTH + SSD_GN].reshape(b, l, SSD_GROUPS, SSD_STATE)
    cm = xbc[..., SSD_WIDTH + SSD_GN:].reshape(b, l, SSD_GROUPS, SSD_STATE)
    dt_raw = u[..., SSD_WIDTH + SSD_XBC:].reshape(b, l, 2, SSD_HEADS).astype(jnp.float32)
    dt = jax.nn.softplus(dt_raw + dt_bias.astype(jnp.float32))
    return z, xs, bm, cm, dt


def _bidir_ssd(xs, bm, cm, dt, a, h0):
    flip = lambda t: jnp.flip(t, axis=1)
    xs2 = jnp.stack([xs, flip(xs)])
    bm2 = jnp.stack([bm, flip(bm)])
    cm2 = jnp.stack([cm, flip(cm)])
    dt2 = jnp.stack([dt[:, :, 0], flip(dt[:, :, 1])])
    y2, h_final = jax.vmap(_ssd_chunked)(xs2, dt2, a, bm2, cm2, h0)
    return y2[0] + flip(y2[1]), h_final


def _ssd_output(y, xs, z, d_skip, norm_g):
    b, l = z.shape[0], z.shape[1]
    y = (y + d_skip.astype(jnp.float32)[:, None] * xs.astype(jnp.float32)).reshape(b, l, SSD_WIDTH)
    g = (y * jax.nn.silu(z.astype(jnp.float32))).reshape(b, l, SSD_GROUPS, SSD_WIDTH // SSD_GROUPS)
    g = g * lax.rsqrt(jnp.mean(g * g, axis=-1, keepdims=True) + EPS)
    return (g.reshape(b, l, SSD_WIDTH) * norm_g.astype(jnp.float32)).astype(z.dtype)


def _ffn_sublayer(x, shift, scale, gate, p):
    h = _modulate(_rms_norm(x, p['g_pre_ffn']), shift, scale)
    f = jnp.square(jax.nn.relu(h @ p['w_ff1'])) @ p['w_ff2']
    return x + gate * _rms_norm(f, p['g_post_ffn'])


def _layer(x, xc, mod, mod_c, rope, need_ctx, p):
    sh1, sc1, g1, sh2, sc2, g2 = jnp.split(mod[:, None, :], N_MOD, axis=-1)
    csh1, csc1, cg1, csh2, csc2, cg2 = jnp.split(mod_c, N_MOD, axis=-1)
    bsz, lc = xc.shape[0], xc.shape[1]

    h = _modulate(_rms_norm(x, p['g_pre_mix']), sh1, sc1)
    hc = _modulate(_rms_norm(xc, p['g_pre_mix']), csh1, csc1)
    ua, us, um = _split_in(h @ p['w_in'])
    uca, ucs, ucm = _split_in(hc @ p['w_in'])

    k, v = _mla_kv(ua, p['mla_kv_norm'], p['w_ukv'], rope)
    kc, vc = _mla_kv(uca, p['mla_kv_norm'], p['w_ukv'], None)
    q = _mla_q(ua, p['mla_q_norm'], p['w_uq'], rope)
    y_att = _blocked_attention(q, jnp.concatenate([kc, k], axis=1), jnp.concatenate([vc, v], axis=1))

    y_sc = _short_conv(us, p['sc_conv_w'])

    a = -jnp.exp(p['ssd_a_log'].astype(jnp.float32))
    zc, xsc, bmc, cmc, dtc = _ssd_prepare(ucm, p['ssd_conv_w'], p['ssd_conv_b'], p['ssd_dt_bias'])
    z, xs, bm, cm, dt = _ssd_prepare(um, p['ssd_conv_w'], p['ssd_conv_b'], p['ssd_dt_bias'])
    h0 = jnp.zeros((2, bsz, SSD_HEADS, SSD_HEAD_DIM, SSD_STATE), jnp.float32)
    yc_scan, h_ctx = _bidir_ssd(xsc, bmc, cmc, dtc, a, h0)
    y_scan, _ = _bidir_ssd(xs, bm, cm, dt, a, h_ctx)
    y_ssd = _ssd_output(y_scan, xs, z, p['ssd_d'], p['ssd_norm'])

    y = jnp.concatenate([y_att, y_sc, y_ssd], axis=-1) @ p['w_out']
    x = x + g1 * _rms_norm(y, p['g_post_mix'])
    x = _ffn_sublayer(x, sh2, sc2, g2, p)

    if need_ctx:
        qc = _mla_q(uca, p['mla_q_norm'], p['w_uq'], None)
        yc_att = _softmax_attend(qc, kc, vc).reshape(bsz, lc, MLA_WIDTH)
        yc_sc = _short_conv(ucs, p['sc_conv_w'])
        yc_ssd = _ssd_output(yc_scan, xsc, zc, p['ssd_d'], p['ssd_norm'])
        yc = jnp.concatenate([yc_att, yc_sc, yc_ssd], axis=-1) @ p['w_out']
        xc = xc + cg1 * _rms_norm(yc, p['g_post_mix'])
        xc = _ffn_sublayer(xc, csh2, csc2, cg2, p)
    return x, xc


def setup_inputs(seed: int = 0) -> dict:
    key = jax.random.key(seed)
    k = jax.random.split(key, 25)
    f32 = jnp.float32
    L = DEPTH

    def nrm(i, shape, scale):
        return jax.random.normal(k[i], shape, f32) * scale

    def gain(i, shape):
        return 1.0 + 0.1 * jax.random.normal(k[i], shape, f32)

    dt0 = jnp.exp(jax.random.uniform(k[16], (L, 2, SSD_HEADS), f32, math.log(1e-3), math.log(1e-1)))
    return {
        'x': nrm(0, (BATCH, SEQ, D_MODEL), 1.0),
        'c': nrm(1, (BATCH, D_MODEL), 1.0),
        'ctx': nrm(2, (BATCH, CTX_LEN, D_MODEL), 1.0),
        'c_ctx': nrm(3, (D_MODEL,), 1.0),
        'w_mod': nrm(4, (L, D_MODEL, N_MOD * D_MODEL), 0.5 * D_MODEL ** -0.5),
        'b_mod': nrm(5, (L, N_MOD * D_MODEL), 0.02),
        'g_pre_mix': gain(6, (L, D_MODEL)),
        'w_in': nrm(7, (L, D_MODEL, D_IN), D_MODEL ** -0.5),
        'mla_q_norm': gain(8, (L, Q_LORA)),
        'w_uq': nrm(9, (L, Q_LORA, MLA_HEADS * (QK_NOPE + QK_ROPE)), Q_LORA ** -0.5),
        'mla_kv_norm': gain(10, (L, KV_LORA)),
        'w_ukv': nrm(11, (L, KV_LORA, MLA_HEADS * (QK_NOPE + V_HEAD)), KV_LORA ** -0.5),
        'sc_conv_w': nrm(12, (L, SC_KERNEL, SC_WIDTH), SC_KERNEL ** -0.5),
        'ssd_conv_w': nrm(13, (L, SSD_CONV, SSD_XBC), SSD_CONV ** -0.5),
        'ssd_conv_b': nrm(14, (L, SSD_XBC), 0.02),
        'ssd_a_log': jnp.log(jax.random.uniform(k[15], (L, 2, SSD_HEADS), f32, 1.0, 16.0)),
        'ssd_dt_bias': dt0 + jnp.log(-jnp.expm1(-dt0)),
        'ssd_d': gain(17, (L, SSD_HEADS)),
        'ssd_norm': gain(18, (L, SSD_WIDTH)),
        'w_out': nrm(19, (L, D_MIX, D_MODEL), D_MIX ** -0.5),
        'g_post_mix': gain(20, (L, D_MODEL)),
        'g_pre_ffn': gain(21, (L, D_MODEL)),
        'w_ff1': nrm(22, (L, D_MODEL, D_FF), D_MODEL ** -0.5),
        'w_ff2': nrm(23, (L, D_FF, D_MODEL), D_FF ** -0.5),
        'g_post_ffn': gain(24, (L, D_MODEL)),
    }


def reference(x, c, ctx, c_ctx, w_mod, b_mod, g_pre_mix, w_in, mla_q_norm, w_uq, mla_kv_norm, w_ukv,
              sc_conv_w, ssd_conv_w, ssd_conv_b, ssd_a_log, ssd_dt_bias, ssd_d, ssd_norm, w_out,
              g_post_mix, g_pre_ffn, w_ff1, w_ff2, g_post_ffn):
    rows = x.shape[1] // GRID_W
    rope = _axial_rope_tables(rows)
    xc = ctx
    s_lat = jax.nn.silu(c)
    s_ctx = jax.nn.silu(c_ctx)
    for i in range(DEPTH):
        p = {
            'g_pre_mix': g_pre_mix[i], 'w_in': w_in[i],
            'mla_q_norm': mla_q_norm[i], 'w_uq': w_uq[i],
            'mla_kv_norm': mla_kv_norm[i], 'w_ukv': w_ukv[i],
            'sc_conv_w': sc_conv_w[i],
            'ssd_conv_w': ssd_conv_w[i], 'ssd_conv_b': ssd_conv_b[i],
            'ssd_a_log': ssd_a_log[i], 'ssd_dt_bias': ssd_dt_bias[i],
            'ssd_d': ssd_d[i], 'ssd_norm': ssd_norm[i],
            'w_out': w_out[i], 'g_post_mix': g_post_mix[i],
            'g_pre_ffn': g_pre_ffn[i], 'w_ff1': w_ff1[i], 'w_ff2': w_ff2[i],
            'g_post_ffn': g_post_ffn[i],
        }
        mod = s_lat @ w_mod[i] + b_mod[i]
        mod_c = s_ctx @ w_mod[i] + b_mod[i]
        x, xc = _layer(x, xc, mod, mod_c, rope, i < DEPTH - 1, p)
    return x
```

```python
import functools
import math

import jax
import jax.numpy as jnp
from jax import lax
from jax.experimental import pallas as pl
from jax.experimental.pallas import tpu as pltpu

F32 = jnp.float32
BF16 = jnp.bfloat16

GRID_W = 64
EPS = 1e-6
N_MOD = 6
MLA_HEADS = 4
Q_LORA = 256
KV_LORA = 128
QK_NOPE = 64
QK_ROPE = 32
V_HEAD = 64
MLA_WIDTH = MLA_HEADS * V_HEAD
MLA_SCALE = (QK_NOPE + QK_ROPE) ** -0.5
ROPE_THETA = 10000.0
SC_WIDTH = 256
SSD_HEADS = 8
SSD_HEAD_DIM = 64
SSD_WIDTH = SSD_HEADS * SSD_HEAD_DIM
SSD_GROUPS = 2
SSD_STATE = 64
SSD_CHUNK = 128
SSD_GN = SSD_GROUPS * SSD_STATE
SSD_XBC = SSD_WIDTH + 2 * SSD_GN
HEADS_PER_GROUP = SSD_HEADS // SSD_GROUPS
IN_MLA = Q_LORA + KV_LORA + QK_ROPE
IN_SC = 3 * SC_WIDTH

LANES = 128
SUBLANES = 8
VMEM_LIMIT_BYTES = 56 * 1024 * 1024

HEAD_PAD = LANES
ROPE_LANE0 = QK_NOPE
TOKEN_TILE = 256
FF_CHUNK = 1024

C_Q0 = 0
C_KV0 = C_Q0 + Q_LORA
C_KR0 = C_KV0 + KV_LORA
C_SC0 = C_KR0 + LANES
C_Z0 = C_SC0 + IN_SC
C_XBC0 = C_Z0 + SSD_WIDTH
C_DT0 = C_XBC0 + SSD_XBC
D_IN_PAD = C_DT0 + LANES

Q_PRESCALE = MLA_SCALE * math.log2(math.e)


def _rms(x, g):
    return x * lax.rsqrt(jnp.mean(x * x, axis=-1, keepdims=True) + EPS) * g


def _silu(x):
    return x / (1.0 + jnp.exp(-x))


def _dot(a, b):
    return jnp.dot(a, b, preferred_element_type=F32)


def _mod_kernel(c_ref, w_ref, b_ref, o_ref):
    s = _silu(c_ref[...]).astype(BF16)
    o_ref[0] = _dot(s, w_ref[0].astype(BF16)) + b_ref[0]


def _modulation(cvec, w_mod, b_mod):
    depth, d, nd = w_mod.shape
    rows = cvec.shape[0]
    return pl.pallas_call(
        _mod_kernel,
        out_shape=jax.ShapeDtypeStruct((depth, rows, nd), F32),
        grid=(depth, nd // d),
        in_specs=[
            pl.BlockSpec((rows, d), lambda l, j: (0, 0)),
            pl.BlockSpec((1, d, d), lambda l, j: (l, 0, j)),
            pl.BlockSpec((1, 1, d), lambda l, j: (l, 0, j)),
        ],
        out_specs=pl.BlockSpec((1, rows, d), lambda l, j: (l, 0, j)),
        compiler_params=pltpu.CompilerParams(dimension_semantics=("parallel", "parallel")),
        name="modulation",
    )(cvec, w_mod, b_mod.reshape(depth, 1, nd))


def _inproj_kernel(x_ref, mod_ref, gpre_ref, win_ref, qg_ref, wuq_ref, kvg_ref, wukv_ref,
                   cos_ref, sin_ref,
                   q_ref, k_ref, v_ref, sc_ref, z_ref, xbc_ref, dt_ref):
    x = x_ref[0]
    shift = mod_ref[0, 0:1, :]
    scale = mod_ref[0, 1:2, :]
    h = _rms(x, gpre_ref[...]) * (1.0 + scale) + shift
    u = _dot(h.astype(BF16), win_ref[...])

    sc_ref[0] = u[:, C_SC0:C_Z0]
    z_ref[0] = u[:, C_Z0:C_XBC0]
    xbc_ref[0] = u[:, C_XBC0:C_DT0]
    dt_ref[0] = u[:, C_DT0:D_IN_PAD]

    cos = cos_ref[...]
    sin = sin_ref[...]
    nh = MLA_HEADS
    cos_h = jnp.concatenate([cos] * nh, axis=1)
    sin_h = jnp.concatenate([sin] * nh, axis=1)

    cq = _rms(u[:, C_Q0:C_KV0], qg_ref[...]).astype(BF16)
    q2 = _dot(cq, wuq_ref[...])
    qw = nh * HEAD_PAD
    q = (q2[:, :qw] * cos_h + q2[:, qw:] * sin_h) * Q_PRESCALE
    q_ref[0] = q.astype(BF16)

    ckv = _rms(u[:, C_KV0:C_KR0], kvg_ref[...]).astype(BF16)
    kv = _dot(ckv, wukv_ref[...])
    lane = lax.broadcasted_iota(jnp.int32, (1, LANES), 1)
    rope_lane = (lane >= ROPE_LANE0) & (lane < ROPE_LANE0 + QK_ROPE)
    cos_k = jnp.where(rope_lane, cos, 0.0)
    krb = u[:, C_KR0:C_SC0]
    kr = krb * cos_k + pltpu.roll(krb, LANES // 2, 1) * sin
    k = kv[:, :qw] + jnp.concatenate([kr] * nh, axis=1)
    k_ref[0] = k.astype(BF16)
    v_ref[0] = kv[:, qw:].astype(BF16)


def _inproj(xs, mod, gpre, win, qg, wuq, kvg, wukv, cos_t, sin_t, ctx_row):
    bsz, t, d = xs.shape
    tm = TOKEN_TILE
    nt = t // tm
    qw = MLA_HEADS * HEAD_PAD

    def tok(width):
        return pl.BlockSpec((1, tm, width), lambda b, i: (b, i, 0))

    def const(shape):
        return pl.BlockSpec(shape, lambda b, i: (0,) * len(shape))

    out_shape = (
        jax.ShapeDtypeStruct((bsz, t, qw), BF16),
        jax.ShapeDtypeStruct((bsz, t, qw), BF16),
        jax.ShapeDtypeStruct((bsz, t, MLA_WIDTH), BF16),
        jax.ShapeDtypeStruct((bsz, t, IN_SC), F32),
        jax.ShapeDtypeStruct((bsz, t, SSD_WIDTH), F32),
        jax.ShapeDtypeStruct((bsz, t, SSD_XBC), F32),
        jax.ShapeDtypeStruct((bsz, t, LANES), F32),
    )
    return pl.pallas_call(
        _inproj_kernel,
        out_shape=out_shape,
        grid=(bsz, nt),
        in_specs=[
            tok(d),
            pl.BlockSpec((1, N_MOD, d), lambda b, i: (jnp.where(i >= 1, b, ctx_row), 0, 0)),
            const((1, d)),
            const(win.shape),
            const((1, Q_LORA)),
            const(wuq.shape),
            const((1, KV_LORA)),
            const(wukv.shape),
            pl.BlockSpec((tm, LANES), lambda b, i: (i, 0)),
            pl.BlockSpec((tm, LANES), lambda b, i: (i, 0)),
        ],
        out_specs=(tok(qw), tok(qw), tok(MLA_WIDTH), tok(IN_SC), tok(SSD_WIDTH), tok(SSD_XBC), tok(LANES)),
        compiler_params=pltpu.CompilerParams(
            dimension_semantics=("parallel", "parallel"), vmem_limit_bytes=VMEM_LIMIT_BYTES),
        name="inproj",
    )(xs, mod, gpre, win, qg, wuq, kvg, wukv, cos_t, sin_t)


def _attn_kernel(q_ref, k_ref, v_ref, o_ref, *, n_keys, n_ctx, ctx_queries):
    def attend(nk):
        lane = lax.broadcasted_iota(jnp.int32, (1, LANES), 1)
        for pair in range(MLA_HEADS // 2):
            halves = []
            for hh in range(2):
                h = 2 * pair + hh
                q = q_ref[0, :, h * HEAD_PAD:(h + 1) * HEAD_PAD]
                k = k_ref[0, 0:nk, h * HEAD_PAD:(h + 1) * HEAD_PAD]
                s = lax.dot_general(q, k, (((1,), (1,)), ((), ())), preferred_element_type=F32)
                m = jnp.max(s, axis=1, keepdims=True)
                e = jnp.exp2(s - m)
                l = jnp.sum(e, axis=1, keepdims=True)
                pv = _dot(e.astype(BF16), v_ref[0, 0:nk, pair * LANES:(pair + 1) * LANES])
                halves.append(pv / l)
            o = jnp.where(lane < V_HEAD, halves[0], halves[1])
            o_ref[0, :, pair * LANES:(pair + 1) * LANES] = o.astype(o_ref.dtype)

    if ctx_queries:
        i = pl.program_id(1)

        @pl.when(i == 0)
        def _():
            attend(n_ctx)

        @pl.when(i > 0)
        def _():
            attend(n_keys)
    else:
        attend(n_keys)


def _attention(q, k, v, n_ctx, ctx_queries):
    bsz, t, qw = q.shape
    tq = TOKEN_TILE
    t0 = 0 if ctx_queries else n_ctx // tq
    nq = t // tq - t0
    return pl.pallas_call(
        functools.partial(_attn_kernel, n_keys=t, n_ctx=n_ctx, ctx_queries=ctx_queries),
        out_shape=jax.ShapeDtypeStruct((bsz, t, MLA_WIDTH), BF16),
        grid=(bsz, nq),
        in_specs=[
            pl.BlockSpec((1, tq, qw), lambda b, i: (b, i + t0, 0)),
            pl.BlockSpec((1, t, qw), lambda b, i: (b, 0, 0)),
            pl.BlockSpec((1, t, MLA_WIDTH), lambda b, i: (b, 0, 0)),
        ],
        out_specs=pl.BlockSpec((1, tq, MLA_WIDTH), lambda b, i: (b, i + t0, 0)),
        compiler_params=pltpu.CompilerParams(
            dimension_semantics=("parallel", "parallel"), vmem_limit_bytes=VMEM_LIMIT_BYTES),
        name="attention",
    )(q, k, v)


def _split3(x):
    hi = x.astype(BF16)
    r = x - hi.astype(F32)
    mid = r.astype(BF16)
    lo = (r - mid.astype(F32)).astype(BF16)
    return hi, mid, lo


def _ssd_role(raw, prev_row, next_row, chunk, n_chunks, ctx_chunks, dt_raw, direction,
              cw_ref, cb_ref, alog_ref, dtbias_ref, dskip_ref, y_ref, h_ref):
    tc = SSD_CHUNK
    first_of_seq = (chunk == 0) | (chunk == ctx_chunks)
    last_of_seq = (chunk == ctx_chunks - 1) | (chunk == n_chunks - 1)
    prev_row = jnp.where(first_of_seq, 0.0, prev_row)
    next_row = jnp.where(last_of_seq, 0.0, next_row)
    row = lax.broadcasted_iota(jnp.int32, (tc, 1), 0)
    xm1 = jnp.where(row == 0, prev_row, pltpu.roll(raw, 1, 0))
    xp1 = jnp.where(row == tc - 1, next_row, pltpu.roll(raw, tc - 1, 0))
    conv = cw_ref[0:1, :] * xm1 + cw_ref[1:2, :] * raw + cw_ref[2:3, :] * xp1 + cb_ref[...]
    act = _silu(conv)
    xs = act[:, :SSD_WIDTH]
    bm = act[:, SSD_WIDTH:SSD_WIDTH + SSD_GN]
    cm = act[:, SSD_WIDTH + SSD_GN:]
    bm_t = bm.T

    a = -jnp.exp(alog_ref[...])
    xb = dt_raw + dtbias_ref[...]
    dt = jnp.maximum(xb, 0.0) + jnp.log(1.0 + jnp.exp(-jnp.abs(xb)))
    la = dt * a

    r_i = lax.broadcasted_iota(jnp.int32, (tc, tc), 0)
    c_i = lax.broadcasted_iota(jnp.int32, (tc, tc), 1)
    tri = (c_i <= r_i) if direction == 0 else (c_i >= r_i)
    tri_b = jnp.where(tri, 1.0, 0.0).astype(BF16)
    hi, mid, lo = _split3(la)
    cs = _dot(tri_b, hi) + _dot(tri_b, mid) + _dot(tri_b, lo)
    cs_t = cs.T
    tot = cs[tc - 1:tc, :] if direction == 0 else cs[0:1, :]
    dec_in = jnp.exp(cs)
    dec_st = jnp.exp(tot - cs)
    dec_tot = jnp.exp(tot)

    p = SSD_HEAD_DIM
    n = SSD_STATE
    for g in range(SSD_GROUPS):
        c_g = cm[:, g * n:(g + 1) * n].astype(BF16)
        cb = _dot(c_g, bm_t[g * n:(g + 1) * n, :].astype(BF16))
        h_g = h_ref[direction, g]
        y_off = _dot(c_g, h_g.astype(BF16))
        xw_parts = []
        dh_parts = []
        for hh in range(HEADS_PER_GROUP):
            hd = g * HEADS_PER_GROUP + hh
            col = direction * SSD_HEADS + hd
            diff = cs[:, col:col + 1] - cs_t[col:col + 1, :]
            lm = jnp.exp(jnp.where(tri, diff, -jnp.inf))
            xs_h = xs[:, hd * p:(hd + 1) * p]
            xdt = xs_h * dt[:, col:col + 1]
            y_h = _dot((cb * lm).astype(BF16), xdt.astype(BF16))
            y_h = y_h + y_off[:, hh * p:(hh + 1) * p] * dec_in[:, col:col + 1]
            if direction == 0:
                y_h = y_h + dskip_ref[:, hd * p:(hd + 1) * p] * xs_h
            y_ref[0, :, hd * p:(hd + 1) * p] = y_h
            xw_parts.append(xdt * dec_st[:, col:col + 1])
            dh_parts.append(jnp.broadcast_to(dec_tot[:, col:col + 1], (1, p)))
        xw = jnp.concatenate(xw_parts, axis=1).astype(BF16)
        h_new = _dot(bm_t[g * n:(g + 1) * n, :].astype(BF16), xw)
        h_ref[direction, g] = h_g * jnp.concatenate(dh_parts, axis=1) + h_new


def _bwd_chunk(i, n_chunks, ctx_chunks):
    return jnp.where(i < ctx_chunks, ctx_chunks - 1 - i, n_chunks - 1 + ctx_chunks - i)


def _ssd_kernel(xf_ref, xfp_ref, xfn_ref, xb_ref, xbp_ref, xbn_ref, dtf_ref, dtb_ref,
                cw_ref, cb_ref, alog_ref, dtbias_ref, dskip_ref,
                yf_ref, yb_ref, h_ref, *, n_chunks, ctx_chunks):
    i = pl.program_id(1)

    @pl.when(i == 0)
    def _():
        h_ref[...] = jnp.zeros_like(h_ref)

    params = (cw_ref, cb_ref, alog_ref, dtbias_ref, dskip_ref)
    _ssd_role(xf_ref[0], xfp_ref[0, SUBLANES - 1:SUBLANES, :], xfn_ref[0, 0:1, :], i, n_chunks, ctx_chunks,
              dtf_ref[0], 0, *params, yf_ref, h_ref)
    _ssd_role(xb_ref[0], xbp_ref[0, SUBLANES - 1:SUBLANES, :], xbn_ref[0, 0:1, :],
              _bwd_chunk(i, n_chunks, ctx_chunks), n_chunks, ctx_chunks,
              dtb_ref[0], 1, *params, yb_ref, h_ref)


def _ssd_scan(xbc, dt, conv_w, conv_b, alog, dtbias, dskip, n_ctx):
    bsz, t, _ = xbc.shape
    tc = SSD_CHUNK
    n_chunks = t // tc
    ctx_chunks = n_ctx // tc
    rb = tc // SUBLANES
    last_rb = t // SUBLANES - 1

    def fwd(i):
        return i

    def bwd(i):
        return _bwd_chunk(i, n_chunks, ctx_chunks)

    def main(order, width):
        return pl.BlockSpec((1, tc, width), lambda b, i: (b, order(i), 0))

    def prev8(order):
        return pl.BlockSpec((1, SUBLANES, SSD_XBC), lambda b, i: (b, jnp.maximum(order(i) * rb - 1, 0), 0))

    def next8(order):
        return pl.BlockSpec((1, SUBLANES, SSD_XBC), lambda b, i: (b, jnp.minimum((order(i) + 1) * rb, last_rb), 0))

    def const(shape):
        return pl.BlockSpec(shape, lambda b, i: (0,) * len(shape))

    y_shape = jax.ShapeDtypeStruct((bsz, t, SSD_WIDTH), F32)
    return pl.pallas_call(
        functools.partial(_ssd_kernel, n_chunks=n_chunks, ctx_chunks=ctx_chunks),
        out_shape=(y_shape, y_shape),
        grid=(bsz, n_chunks),
        in_specs=[
            main(fwd, SSD_XBC), prev8(fwd), next8(fwd),
            main(bwd, SSD_XBC), prev8(bwd), next8(bwd),
            main(fwd, LANES), main(bwd, LANES),
            const(conv_w.shape), const(conv_b.shape), const(alog.shape), const(dtbias.shape), const(dskip.shape),
        ],
        out_specs=(main(fwd, SSD_WIDTH), main(bwd, SSD_WIDTH)),
        scratch_shapes=[pltpu.VMEM((2, SSD_GROUPS, SSD_STATE, HEADS_PER_GROUP * SSD_HEAD_DIM), F32)],
        compiler_params=pltpu.CompilerParams(
            dimension_semantics=("arbitrary", "arbitrary"), vmem_limit_bytes=VMEM_LIMIT_BYTES),
        name="ssd_scan",
    )(xbc, xbc, xbc, xbc, xbc, xbc, dt, dt, conv_w, conv_b, alog, dtbias, dskip)


def _mix_ffn_kernel(x_ref, mod_ref, att_ref, sc_ref, scp_ref, scn_ref, z_ref, yf_ref, yb_ref,
                    scw_ref, normg_ref, wout_ref, gpost_ref, gpre2_ref, w1_ref, w2_ref, gpost2_ref,
                    o_ref, ycat_ref, acc_ref, *, tile0, n_tiles, ctx_tiles):
    tm = x_ref.shape[1]
    tile = pl.program_id(1) + tile0
    first_of_seq = (tile == 0) | (tile == ctx_tiles)
    last_of_seq = (tile == ctx_tiles - 1) | (tile == n_tiles - 1)

    w = SC_WIDTH
    u = sc_ref[0]
    gate_b = u[:, :w]
    pr = u[:, w:2 * w] * u[:, 2 * w:]
    up = scp_ref[0, SUBLANES - 1:SUBLANES, :]
    un = scn_ref[0, 0:1, :]
    pr_prev = jnp.where(first_of_seq, 0.0, up[:, w:2 * w] * up[:, 2 * w:])
    pr_next = jnp.where(last_of_seq, 0.0, un[:, w:2 * w] * un[:, 2 * w:])
    row = lax.broadcasted_iota(jnp.int32, (tm, 1), 0)
    pm1 = jnp.where(row == 0, pr_prev, pltpu.roll(pr, 1, 0))
    pp1 = jnp.where(row == tm - 1, pr_next, pltpu.roll(pr, tm - 1, 0))
    y_sc = gate_b * (scw_ref[0:1, :] * pm1 + scw_ref[1:2, :] * pr + scw_ref[2:3, :] * pp1)

    gated = (yf_ref[0] + yb_ref[0]) * _silu(z_ref[0])
    gw = SSD_WIDTH // SSD_GROUPS
    ycat_ref[:, 0:MLA_WIDTH] = att_ref[0]
    ycat_ref[:, MLA_WIDTH:MLA_WIDTH + w] = y_sc.astype(BF16)
    c0 = MLA_WIDTH + w
    for g in range(SSD_GROUPS):
        gg = gated[:, g * gw:(g + 1) * gw]
        gg = gg * lax.rsqrt(jnp.mean(gg * gg, axis=-1, keepdims=True) + EPS) * normg_ref[:, g * gw:(g + 1) * gw]
        ycat_ref[:, c0 + g * gw:c0 + (g + 1) * gw] = gg.astype(BF16)

    x = x_ref[0]
    gate1 = mod_ref[0, 2:3, :]
    shift2 = mod_ref[0, 3:4, :]
    scale2 = mod_ref[0, 4:5, :]
    gate2 = mod_ref[0, 5:6, :]
    y = _dot(ycat_ref[...], wout_ref[...])
    x1 = x + gate1 * _rms(y, gpost_ref[...])

    h2 = (_rms(x1, gpre2_ref[...]) * (1.0 + scale2) + shift2).astype(BF16)
    d_ff = w1_ref.shape[1]
    for c in range(d_ff // FF_CHUNK):
        a = _dot(h2, w1_ref[:, c * FF_CHUNK:(c + 1) * FF_CHUNK])
        r = jnp.square(jnp.maximum(a, 0.0)).astype(BF16)
        part = _dot(r, w2_ref[c * FF_CHUNK:(c + 1) * FF_CHUNK, :])
        if c == 0:
            acc_ref[...] = part
        else:
            acc_ref[...] += part
    o_ref[0] = x1 + gate2 * _rms(acc_ref[...], gpost2_ref[...])


def _mix_ffn(xs, mod, att, sc, z, yf, yb, scw, normg, wout, gpost, gpre2, w1, w2, gpost2, n_ctx, ctx_row,
             latent_only):
    bsz, t, d = xs.shape
    tm = TOKEN_TILE
    n_tiles = t // tm
    ctx_tiles = n_ctx // tm
    tile0 = ctx_tiles if latent_only else 0
    rb = tm // SUBLANES
    last_rb = t // SUBLANES - 1

    def tok(width):
        return pl.BlockSpec((1, tm, width), lambda b, i: (b, i + tile0, 0))

    def const(shape):
        return pl.BlockSpec(shape, lambda b, i: (0,) * len(shape), pipeline_mode=pl.Buffered(1))

    out_rows = t - tile0 * tm
    return pl.pallas_call(
        functools.partial(_mix_ffn_kernel, tile0=tile0, n_tiles=n_tiles, ctx_tiles=ctx_tiles),
        out_shape=jax.ShapeDtypeStruct((bsz, out_rows, d), F32),
        grid=(bsz, n_tiles - tile0),
        in_specs=[
            tok(d),
            pl.BlockSpec((1, N_MOD, d), lambda b, i: (jnp.where(i + tile0 >= ctx_tiles, b, ctx_row), 0, 0)),
            tok(MLA_WIDTH),
            tok(IN_SC),
            pl.BlockSpec((1, SUBLANES, IN_SC), lambda b, i: (b, jnp.maximum((i + tile0) * rb - 1, 0), 0)),
            pl.BlockSpec((1, SUBLANES, IN_SC), lambda b, i: (b, jnp.minimum((i + tile0 + 1) * rb, last_rb), 0)),
            tok(SSD_WIDTH), tok(SSD_WIDTH), tok(SSD_WIDTH),
            const(scw.shape), const(normg.shape), const(wout.shape), const(gpost.shape), const(gpre2.shape),
            const(w1.shape), const(w2.shape), const(gpost2.shape),
        ],
        out_specs=pl.BlockSpec((1, tm, d), lambda b, i: (b, i, 0)),
        scratch_shapes=[pltpu.VMEM((tm, d), BF16), pltpu.VMEM((tm, d), F32)],
        compiler_params=pltpu.CompilerParams(
            dimension_semantics=("parallel", "parallel"), vmem_limit_bytes=VMEM_LIMIT_BYTES),
        name="mix_ffn",
    )(xs, mod, att, sc, sc, sc, z, yf, yb, scw, normg, wout, gpost, gpre2, w1, w2, gpost2)


def _rotate_cols(w):
    a, b, c, d = jnp.split(w, 4, axis=-1)
    return jnp.concatenate([-b, a, -d, c], axis=-1)


def _pad_cols(w, width):
    return jnp.pad(w, ((0, 0), (0, width - w.shape[-1])))


def _relayout_w_in(w_in):
    d = w_in.shape[0]
    w_kr = w_in[:, Q_LORA + KV_LORA:IN_MLA]
    gap = jnp.zeros((d, LANES // 2 - QK_ROPE), w_in.dtype)
    kr_block = jnp.concatenate([_rotate_cols(w_kr), gap, w_kr, gap], axis=1)
    w_dt = w_in[:, IN_MLA + IN_SC + SSD_WIDTH + SSD_XBC:]
    return jnp.concatenate([
        w_in[:, :Q_LORA + KV_LORA], kr_block,
        w_in[:, IN_MLA:IN_MLA + IN_SC + SSD_WIDTH + SSD_XBC],
        _pad_cols(w_dt, LANES)], axis=1)


def _relayout_w_uq(w_uq):
    dqk = QK_NOPE + QK_ROPE
    plain, rot = [], []
    zeros_nope = jnp.zeros((w_uq.shape[0], QK_NOPE), w_uq.dtype)
    for h in range(MLA_HEADS):
        wh = w_uq[:, h * dqk:(h + 1) * dqk]
        plain.append(_pad_cols(wh, HEAD_PAD))
        rot.append(_pad_cols(jnp.concatenate([zeros_nope, _rotate_cols(wh[:, QK_NOPE:])], axis=1), HEAD_PAD))
    return jnp.concatenate(plain + rot, axis=1)


def _relayout_w_ukv(w_ukv):
    dkv = QK_NOPE + V_HEAD
    ks, vs = [], []
    for h in range(MLA_HEADS):
        wh = w_ukv[:, h * dkv:(h + 1) * dkv]
        ks.append(_pad_cols(wh[:, :QK_NOPE], HEAD_PAD))
        vs.append(wh[:, QK_NOPE:])
    return jnp.concatenate(ks + vs, axis=1)


def _rope_tables(n_ctx, seq):
    half = QK_ROPE // 2
    inv_freq = ROPE_THETA ** (-jnp.arange(0, half, 2, dtype=F32) / half)
    rows = seq // GRID_W
    row = jnp.repeat(jnp.arange(rows, dtype=F32), GRID_W)
    col = jnp.tile(jnp.arange(GRID_W, dtype=F32), rows)
    ang_r = row[:, None] * inv_freq
    ang_c = col[:, None] * inv_freq
    ang = jnp.concatenate([ang_r, ang_r, ang_c, ang_c], axis=-1)
    cos = jnp.concatenate([jnp.ones((n_ctx, QK_ROPE), F32), jnp.cos(ang)], axis=0)
    sin = jnp.concatenate([jnp.zeros((n_ctx, QK_ROPE), F32), jnp.sin(ang)], axis=0)
    t = n_ctx + seq
    cos_t = jnp.ones((t, LANES), F32).at[:, ROPE_LANE0:ROPE_LANE0 + QK_ROPE].set(cos)
    sin_t = jnp.zeros((t, LANES), F32).at[:, ROPE_LANE0:ROPE_LANE0 + QK_ROPE].set(sin)
    return cos_t, sin_t


def kernel(x, c, ctx, c_ctx, w_mod, b_mod, g_pre_mix, w_in, mla_q_norm, w_uq, mla_kv_norm, w_ukv, sc_conv_w, ssd_conv_w, ssd_conv_b, ssd_a_log, ssd_dt_bias, ssd_d, ssd_norm, w_out, g_post_mix, g_pre_ffn, w_ff1, w_ff2, g_post_ffn):
    bsz, seq, d = x.shape
    n_ctx = ctx.shape[1]
    depth = w_mod.shape[0]
    assert n_ctx == TOKEN_TILE and seq % TOKEN_TILE == 0 and seq % GRID_W == 0
    ctx_row = bsz

    cvec = jnp.zeros((SUBLANES, d), F32).at[:bsz].set(c).at[ctx_row].set(c_ctx)
    mod_all = _modulation(cvec, w_mod, b_mod).reshape(depth, SUBLANES, N_MOD, d)
    cos_t, sin_t = _rope_tables(n_ctx, seq)

    xs = jnp.concatenate([ctx, x], axis=1)
    for i in range(depth):
        last = i == depth - 1
        win = _relayout_w_in(w_in[i]).astype(BF16)
        wuq = _relayout_w_uq(w_uq[i]).astype(BF16)
        wukv = _relayout_w_ukv(w_ukv[i]).astype(BF16)
        q, k, v, u_sc, z, xbc, dt = _inproj(
            xs, mod_all[i], g_pre_mix[i][None], win, mla_q_norm[i][None], wuq, mla_kv_norm[i][None], wukv,
            cos_t, sin_t, ctx_row)
        att = _attention(q, k, v, n_ctx, ctx_queries=not last)
        alog = _pad_cols(ssd_a_log[i].reshape(1, -1), LANES)
        dtbias = _pad_cols(ssd_dt_bias[i].reshape(1, -1), LANES)
        dskip = jnp.repeat(ssd_d[i], SSD_HEAD_DIM)[None]
        yf, yb = _ssd_scan(xbc, dt, ssd_conv_w[i], ssd_conv_b[i][None], alog, dtbias, dskip, n_ctx)
        xs = _mix_ffn(xs, mod_all[i], att, u_sc, z, yf, yb, sc_conv_w[i], ssd_norm[i][None],
                      w_out[i].astype(BF16), g_post_mix[i][None], g_pre_ffn[i][None],
                      w_ff1[i].astype(BF16), w_ff2[i].astype(BF16), g_post_ffn[i][None],
                      n_ctx, ctx_row, latent_only=last)
    return xs
```

```python
import functools
import math

import jax
import jax.numpy as jnp
from jax import lax
from jax.experimental import pallas as pl
from jax.experimental.pallas import tpu as pltpu

F32 = jnp.float32
BF16 = jnp.bfloat16

GRID_W = 64
EPS = 1e-6
N_MOD = 6
MLA_HEADS = 4
Q_LORA = 256
KV_LORA = 128
QK_NOPE = 64
QK_ROPE = 32
V_HEAD = 64
MLA_WIDTH = MLA_HEADS * V_HEAD
MLA_SCALE = (QK_NOPE + QK_ROPE) ** -0.5
ROPE_THETA = 10000.0
SC_WIDTH = 256
SSD_HEADS = 8
SSD_HEAD_DIM = 64
SSD_WIDTH = SSD_HEADS * SSD_HEAD_DIM
SSD_GROUPS = 2
SSD_STATE = 64
SSD_CHUNK = 128
SSD_GN = SSD_GROUPS * SSD_STATE
SSD_XBC = SSD_WIDTH + 2 * SSD_GN
HEADS_PER_GROUP = SSD_HEADS // SSD_GROUPS
IN_MLA = Q_LORA + KV_LORA + QK_ROPE
IN_SC = 3 * SC_WIDTH

LANES = 128
SUBLANES = 8
VMEM_LIMIT_BYTES = 56 * 1024 * 1024

HEAD_PAD = LANES
ROPE_LANE0 = QK_NOPE
TOKEN_TILE = 256
FF_CHUNK = 1024

C_Q0 = 0
C_KV0 = C_Q0 + Q_LORA
C_KR0 = C_KV0 + KV_LORA
C_SC0 = C_KR0 + LANES
C_Z0 = C_SC0 + IN_SC
C_XBC0 = C_Z0 + SSD_WIDTH
C_DT0 = C_XBC0 + SSD_XBC
D_IN_PAD = C_DT0 + LANES

Q_PRESCALE = MLA_SCALE * math.log2(math.e)


def _rms(x, g):
    return x * lax.rsqrt(jnp.mean(x * x, axis=-1, keepdims=True) + EPS) * g


def _silu(x):
    return x / (1.0 + jnp.exp(-x))


def _dot(a, b):
    return jnp.dot(a, b, preferred_element_type=F32)


def _mod_kernel(c_ref, w_ref, b_ref, o_ref):
    s = _silu(c_ref[...]).astype(BF16)
    o_ref[0] = _dot(s, w_ref[0].astype(BF16)) + b_ref[0]


def _modulation(cvec, w_mod, b_mod):
    depth, d, nd = w_mod.shape
    rows = cvec.shape[0]
    return pl.pallas_call(
        _mod_kernel,
        out_shape=jax.ShapeDtypeStruct((depth, rows, nd), F32),
        grid=(depth, nd // d),
        in_specs=[
            pl.BlockSpec((rows, d), lambda l, j: (0, 0)),
            pl.BlockSpec((1, d, d), lambda l, j: (l, 0, j)),
            pl.BlockSpec((1, 1, d), lambda l, j: (l, 0, j)),
        ],
        out_specs=pl.BlockSpec((1, rows, d), lambda l, j: (l, 0, j)),
        compiler_params=pltpu.CompilerParams(dimension_semantics=("parallel", "parallel")),
        name="modulation",
    )(cvec, w_mod, b_mod.reshape(depth, 1, nd))


def _conv3_rows(a_ext, w_ref, tm):
    te = a_ext.shape[0]
    lo, hi = SUBLANES, SUBLANES + tm
    prev = pltpu.roll(a_ext, 1, 0)[lo:hi]
    nxt = pltpu.roll(a_ext, te - 1, 0)[lo:hi]
    return w_ref[0:1, :] * prev + w_ref[1:2, :] * a_ext[lo:hi] + w_ref[2:3, :] * nxt


def _inproj_kernel(x_ref, xp_ref, xn_ref, mod_ref, gpre_ref, win_ref, qg_ref, wuq_ref, kvg_ref, wukv_ref,
                   cos_ref, sin_ref, scw_ref, cw_ref, cb_ref,
                   q_ref, k_ref, v_ref, ysc_ref, z_ref, act_ref, dt_ref, *, n_tiles, ctx_tiles):
    tm = x_ref.shape[1]
    tile = pl.program_id(1)
    first_of_seq = (tile == 0) | (tile == ctx_tiles)
    last_of_seq = (tile == ctx_tiles - 1) | (tile == n_tiles - 1)
    shift = mod_ref[0, 0:1, :]
    scale = mod_ref[0, 1:2, :]

    def norm_mod(xx):
        return _rms(xx, gpre_ref[...]) * (1.0 + scale) + shift

    h_prev = jnp.where(first_of_seq, 0.0, norm_mod(xp_ref[0]))
    h_next = jnp.where(last_of_seq, 0.0, norm_mod(xn_ref[0]))
    h_ext = jnp.concatenate([h_prev, norm_mod(x_ref[0]), h_next], axis=0).astype(BF16)
    u_ext = _dot(h_ext, win_ref[...])
    u = u_ext[SUBLANES:SUBLANES + tm]

    z_ref[0] = u[:, C_Z0:C_XBC0].astype(z_ref.dtype)
    dt_ref[0] = u[:, C_DT0:D_IN_PAD]

    conv = _conv3_rows(u_ext[:, C_XBC0:C_DT0], cw_ref, tm) + cb_ref[...]
    act_ref[0] = _silu(conv).astype(act_ref.dtype)

    w = SC_WIDTH
    prod = u_ext[:, C_SC0 + w:C_SC0 + 2 * w] * u_ext[:, C_SC0 + 2 * w:C_Z0]
    ysc_ref[0] = (u[:, C_SC0:C_SC0 + w] * _conv3_rows(prod, scw_ref, tm)).astype(ysc_ref.dtype)

    cos = cos_ref[...]
    sin = sin_ref[...]
    nh = MLA_HEADS
    cos_h = jnp.concatenate([cos] * nh, axis=1)
    sin_h = jnp.concatenate([sin] * nh, axis=1)

    cq = _rms(u[:, C_Q0:C_KV0], qg_ref[...]).astype(BF16)
    q2 = _dot(cq, wuq_ref[...])
    qw = nh * HEAD_PAD
    q = (q2[:, :qw] * cos_h + q2[:, qw:] * sin_h) * Q_PRESCALE
    q_ref[0] = q.astype(BF16)

    ckv = _rms(u[:, C_KV0:C_KR0], kvg_ref[...]).astype(BF16)
    kv = _dot(ckv, wukv_ref[...])
    lane = lax.broadcasted_iota(jnp.int32, (1, LANES), 1)
    rope_lane = (lane >= ROPE_LANE0) & (lane < ROPE_LANE0 + QK_ROPE)
    cos_k = jnp.where(rope_lane, cos, 0.0)
    krb = u[:, C_KR0:C_SC0]
    kr = krb * cos_k + pltpu.roll(krb, LANES // 2, 1) * sin
    k = kv[:, :qw] + jnp.concatenate([kr] * nh, axis=1)
    k_ref[0] = k.astype(BF16)
    v_ref[0] = kv[:, qw:].astype(BF16)


def _inproj(xs, mod, gpre, win, qg, wuq, kvg, wukv, cos_t, sin_t, scw, cw, cb, n_ctx, ctx_row):
    bsz, t, d = xs.shape
    tm = TOKEN_TILE
    nt = t // tm
    ctx_tiles = n_ctx // tm
    qw = MLA_HEADS * HEAD_PAD
    rb = tm // SUBLANES
    last_rb = t // SUBLANES - 1

    def tok(width):
        return pl.BlockSpec((1, tm, width), lambda b, i: (b, i, 0))

    def const(shape):
        return pl.BlockSpec(shape, lambda b, i: (0,) * len(shape))

    out_shape = (
        jax.ShapeDtypeStruct((bsz, t, qw), BF16),
        jax.ShapeDtypeStruct((bsz, t, qw), BF16),
        jax.ShapeDtypeStruct((bsz, t, MLA_WIDTH), BF16),
        jax.ShapeDtypeStruct((bsz, t, SC_WIDTH), BF16),
        jax.ShapeDtypeStruct((bsz, t, SSD_WIDTH), BF16),
        jax.ShapeDtypeStruct((bsz, t, SSD_XBC), BF16),
        jax.ShapeDtypeStruct((bsz, t, LANES), F32),
    )
    return pl.pallas_call(
        functools.partial(_inproj_kernel, n_tiles=nt, ctx_tiles=ctx_tiles),
        out_shape=out_shape,
        grid=(bsz, nt),
        in_specs=[
            tok(d),
            pl.BlockSpec((1, SUBLANES, d), lambda b, i: (b, jnp.maximum(i * rb - 1, 0), 0)),
            pl.BlockSpec((1, SUBLANES, d), lambda b, i: (b, jnp.minimum((i + 1) * rb, last_rb), 0)),
            pl.BlockSpec((1, N_MOD, d), lambda b, i: (jnp.where(i >= ctx_tiles, b, ctx_row), 0, 0)),
            const((1, d)),
            const(win.shape),
            const((1, Q_LORA)),
            const(wuq.shape),
            const((1, KV_LORA)),
            const(wukv.shape),
            pl.BlockSpec((tm, LANES), lambda b, i: (i, 0)),
            pl.BlockSpec((tm, LANES), lambda b, i: (i, 0)),
            const(scw.shape), const(cw.shape), const(cb.shape),
        ],
        out_specs=(tok(qw), tok(qw), tok(MLA_WIDTH), tok(SC_WIDTH), tok(SSD_WIDTH), tok(SSD_XBC), tok(LANES)),
        compiler_params=pltpu.CompilerParams(
            dimension_semantics=("parallel", "parallel"), vmem_limit_bytes=VMEM_LIMIT_BYTES),
        name="inproj",
    )(xs, xs, xs, mod, gpre, win, qg, wuq, kvg, wukv, cos_t, sin_t, scw, cw, cb)


def _attn_kernel(q_ref, k_ref, v_ref, o_ref, *, n_keys, n_ctx, ctx_queries):
    def attend(nk):
        lane = lax.broadcasted_iota(jnp.int32, (1, LANES), 1)
        for pair in range(MLA_HEADS // 2):
            halves = []
            for hh in range(2):
                h = 2 * pair + hh
                q = q_ref[0, :, h * HEAD_PAD:(h + 1) * HEAD_PAD]
                k = k_ref[0, 0:nk, h * HEAD_PAD:(h + 1) * HEAD_PAD]
                s = lax.dot_general(q, k, (((1,), (1,)), ((), ())), preferred_element_type=F32)
                m = jnp.max(s, axis=1, keepdims=True)
                e = jnp.exp2(s - m)
                l = jnp.sum(e, axis=1, keepdims=True)
                pv = _dot(e.astype(BF16), v_ref[0, 0:nk, pair * LANES:(pair + 1) * LANES])
                halves.append(pv / l)
            o = jnp.where(lane < V_HEAD, halves[0], halves[1])
            o_ref[0, :, pair * LANES:(pair + 1) * LANES] = o.astype(o_ref.dtype)

    if ctx_queries:
        i = pl.program_id(1)

        @pl.when(i == 0)
        def _():
            attend(n_ctx)

        @pl.when(i > 0)
        def _():
            attend(n_keys)
    else:
        attend(n_keys)


def _attention(q, k, v, n_ctx, ctx_queries):
    bsz, t, qw = q.shape
    tq = TOKEN_TILE
    t0 = 0 if ctx_queries else n_ctx // tq
    nq = t // tq - t0
    return pl.pallas_call(
        functools.partial(_attn_kernel, n_keys=t, n_ctx=n_ctx, ctx_queries=ctx_queries),
        out_shape=jax.ShapeDtypeStruct((bsz, t, MLA_WIDTH), BF16),
        grid=(bsz, nq),
        in_specs=[
            pl.BlockSpec((1, tq, qw), lambda b, i: (b, i + t0, 0)),
            pl.BlockSpec((1, t, qw), lambda b, i: (b, 0, 0)),
            pl.BlockSpec((1, t, MLA_WIDTH), lambda b, i: (b, 0, 0)),
        ],
        out_specs=pl.BlockSpec((1, tq, MLA_WIDTH), lambda b, i: (b, i + t0, 0)),
        compiler_params=pltpu.CompilerParams(
            dimension_semantics=("parallel", "parallel"), vmem_limit_bytes=VMEM_LIMIT_BYTES),
        name="attention",
    )(q, k, v)


def _split3(x):
    hi = x.astype(BF16)
    r = x - hi.astype(F32)
    mid = r.astype(BF16)
    lo = (r - mid.astype(F32)).astype(BF16)
    return hi, mid, lo


def _ssd_role(act, dt_raw, direction, alog_ref, dtbias_ref, dskip_ref, y_ref, h_ref):
    tc = SSD_CHUNK
    n = SSD_STATE
    xs = act[:, :SSD_WIDTH]
    bm = act[:, SSD_WIDTH:SSD_WIDTH + SSD_GN].astype(F32)
    cm = act[:, SSD_WIDTH + SSD_GN:]
    bm_t = bm.T

    a = -jnp.exp(alog_ref[...])
    xb = dt_raw + dtbias_ref[...]
    dt = jnp.maximum(xb, 0.0) + jnp.log(1.0 + jnp.exp(-jnp.abs(xb)))
    la = dt * a

    r_i = lax.broadcasted_iota(jnp.int32, (tc, tc), 0)
    c_i = lax.broadcasted_iota(jnp.int32, (tc, tc), 1)
    tri = (c_i <= r_i) if direction == 0 else (c_i >= r_i)
    tri_b = jnp.where(tri, 1.0, 0.0).astype(BF16)
    hi, mid, lo = _split3(la)
    cs = _dot(tri_b, hi) + _dot(tri_b, mid) + _dot(tri_b, lo)
    end = tc - 1 if direction == 0 else 0
    nd = 2 * SSD_HEADS
    cs_t = cs.T[0:nd]
    dt_t = dt.T[0:nd]
    w_t = jnp.exp(cs_t[:, end:end + 1] - cs_t) * dt_t
    dec_tot = jnp.exp(cs[end:end + 1, :])
    cm_f = cm.astype(F32)
    lane = lax.broadcasted_iota(jnp.int32, (1, LANES), 1)
    low_half = lane < SSD_HEAD_DIM
    zeros_h = jnp.zeros((n, LANES), BF16)
    zeros_lhs = jnp.zeros((n, tc), F32)

    for g in range(SSD_GROUPS):
        cb = _dot(cm[:, g * n:(g + 1) * n], bm_t[g * n:(g + 1) * n, :].astype(BF16))
        bt_g = bm_t[g * n:(g + 1) * n, :]
        for pr in range(HEADS_PER_GROUP // 2):
            slab = g * (HEADS_PER_GROUP // 2) + pr
            h_pair = h_ref[direction, g, :, pr * LANES:(pr + 1) * LANES]
            h_b = h_pair.astype(BF16)
            xs_pair = xs[:, slab * LANES:(slab + 1) * LANES]
            rhs = jnp.concatenate([xs_pair] + [h_b if gg == g else zeros_h for gg in range(SSD_GROUPS)], axis=0)
            lhs_rows = []
            cols = []
            for hh in range(2):
                col = direction * SSD_HEADS + 2 * slab + hh
                cols.append(col)
                cs_col = jnp.broadcast_to(cs[:, col:col + 1], (tc, tc))
                lm = jnp.exp(jnp.where(tri, cs_col - cs_t[col:col + 1, :], -jnp.inf))
                top = jnp.concatenate([cb * dt_t[col:col + 1, :] * lm, cm_f * jnp.exp(cs_col)], axis=1)
                bot = jnp.concatenate([bt_g * w_t[col:col + 1, :], zeros_lhs], axis=1)
                lhs_rows += [top, bot]
            lhs = jnp.concatenate(lhs_rows, axis=0).astype(BF16)
            out = _dot(lhs, rhs)
            m = tc + n
            y_pair = jnp.where(low_half, out[0:tc], out[m:m + tc])
            if direction == 0:
                y_pair = y_pair + dskip_ref[:, slab * LANES:(slab + 1) * LANES] * xs_pair.astype(F32)
            y_ref[0, :, slab * LANES:(slab + 1) * LANES] = y_pair
            h_new = jnp.where(low_half, out[tc:m], out[m + tc:2 * m])
            keep = jnp.where(low_half, dec_tot[:, cols[0]:cols[0] + 1], dec_tot[:, cols[1]:cols[1] + 1])
            h_ref[direction, g, :, pr * LANES:(pr + 1) * LANES] = h_pair * keep + h_new


def _bwd_chunk(i, n_chunks, ctx_chunks):
    return jnp.where(i < ctx_chunks, ctx_chunks - 1 - i, n_chunks - 1 + ctx_chunks - i)


def _ssd_kernel(af_ref, ab_ref, dtf_ref, dtb_ref, alog_ref, dtbias_ref, dskip_ref,
                yf_ref, yb_ref, h_ref):
    @pl.when(pl.program_id(1) == 0)
    def _():
        h_ref[...] = jnp.zeros_like(h_ref)

    params = (alog_ref, dtbias_ref, dskip_ref)
    _ssd_role(af_ref[0], dtf_ref[0], 0, *params, yf_ref, h_ref)
    _ssd_role(ab_ref[0], dtb_ref[0], 1, *params, yb_ref, h_ref)


def _ssd_scan(act, dt, alog, dtbias, dskip, n_ctx):
    bsz, t, _ = act.shape
    tc = SSD_CHUNK
    n_chunks = t // tc
    ctx_chunks = n_ctx // tc

    def fwd(i):
        return i

    def bwd(i):
        return _bwd_chunk(i, n_chunks, ctx_chunks)

    def main(order, width):
        return pl.BlockSpec((1, tc, width), lambda b, i: (b, order(i), 0))

    def const(shape):
        return pl.BlockSpec(shape, lambda b, i: (0,) * len(shape))

    y_shape = jax.ShapeDtypeStruct((bsz, t, SSD_WIDTH), F32)
    return pl.pallas_call(
        _ssd_kernel,
        out_shape=(y_shape, y_shape),
        grid=(bsz, n_chunks),
        in_specs=[
            main(fwd, SSD_XBC), main(bwd, SSD_XBC), main(fwd, LANES), main(bwd, LANES),
            const(alog.shape), const(dtbias.shape), const(dskip.shape),
        ],
        out_specs=(main(fwd, SSD_WIDTH), main(bwd, SSD_WIDTH)),
        scratch_shapes=[pltpu.VMEM((2, SSD_GROUPS, SSD_STATE, HEADS_PER_GROUP * SSD_HEAD_DIM), F32)],
        compiler_params=pltpu.CompilerParams(
            dimension_semantics=("arbitrary", "arbitrary"), vmem_limit_bytes=VMEM_LIMIT_BYTES),
        name="ssd_scan",
    )(act, act, dt, dt, alog, dtbias, dskip)


def _mix_ffn_kernel(x_ref, mod_ref, att_ref, ysc_ref, z_ref, yf_ref, yb_ref,
                    normg_ref, wout_ref, gpost_ref, gpre2_ref, w1_ref, w2_ref, gpost2_ref,
                    o_ref, ycat_ref, acc_ref):
    gated = (yf_ref[0] + yb_ref[0]) * _silu(z_ref[0].astype(F32))
    gw = SSD_WIDTH // SSD_GROUPS
    w = SC_WIDTH
    ycat_ref[:, 0:MLA_WIDTH] = att_ref[0]
    ycat_ref[:, MLA_WIDTH:MLA_WIDTH + w] = ysc_ref[0]
    c0 = MLA_WIDTH + w
    for g in range(SSD_GROUPS):
        gg = gated[:, g * gw:(g + 1) * gw]
        gg = gg * lax.rsqrt(jnp.mean(gg * gg, axis=-1, keepdims=True) + EPS) * normg_ref[:, g * gw:(g + 1) * gw]
        ycat_ref[:, c0 + g * gw:c0 + (g + 1) * gw] = gg.astype(BF16)

    x = x_ref[0]
    gate1 = mod_ref[0, 2:3, :]
    shift2 = mod_ref[0, 3:4, :]
    scale2 = mod_ref[0, 4:5, :]
    gate2 = mod_ref[0, 5:6, :]
    y = _dot(ycat_ref[...], wout_ref[...])
    x1 = x + gate1 * _rms(y, gpost_ref[...])

    h2 = (_rms(x1, gpre2_ref[...]) * (1.0 + scale2) + shift2).astype(BF16)
    d_ff = w1_ref.shape[1]
    for c in range(d_ff // FF_CHUNK):
        a = _dot(h2, w1_ref[:, c * FF_CHUNK:(c + 1) * FF_CHUNK])
        r = jnp.square(jnp.maximum(a, 0.0)).astype(BF16)
        part = _dot(r, w2_ref[c * FF_CHUNK:(c + 1) * FF_CHUNK, :])
        if c == 0:
            acc_ref[...] = part
        else:
            acc_ref[...] += part
    o_ref[0] = x1 + gate2 * _rms(acc_ref[...], gpost2_ref[...])


def _mix_ffn(xs, mod, att, ysc, z, yf, yb, normg, wout, gpost, gpre2, w1, w2, gpost2, n_ctx, ctx_row,
             latent_only):
    bsz, t, d = xs.shape
    tm = TOKEN_TILE
    n_tiles = t // tm
    ctx_tiles = n_ctx // tm
    tile0 = ctx_tiles if latent_only else 0

    def tok(width):
        return pl.BlockSpec((1, tm, width), lambda b, i: (b, i + tile0, 0))

    def const(shape):
        return pl.BlockSpec(shape, lambda b, i: (0,) * len(shape), pipeline_mode=pl.Buffered(1))

    out_rows = t - tile0 * tm
    return pl.pallas_call(
        _mix_ffn_kernel,
        out_shape=jax.ShapeDtypeStruct((bsz, out_rows, d), F32),
        grid=(bsz, n_tiles - tile0),
        in_specs=[
            tok(d),
            pl.BlockSpec((1, N_MOD, d), lambda b, i: (jnp.where(i + tile0 >= ctx_tiles, b, ctx_row), 0, 0)),
            tok(MLA_WIDTH), tok(SC_WIDTH),
            tok(SSD_WIDTH), tok(SSD_WIDTH), tok(SSD_WIDTH),
            const(normg.shape), const(wout.shape), const(gpost.shape), const(gpre2.shape),
            const(w1.shape), const(w2.shape), const(gpost2.shape),
        ],
        out_specs=pl.BlockSpec((1, tm, d), lambda b, i: (b, i, 0)),
        scratch_shapes=[pltpu.VMEM((tm, d), BF16), pltpu.VMEM((tm, d), F32)],
        compiler_params=pltpu.CompilerParams(
            dimension_semantics=("parallel", "parallel"), vmem_limit_bytes=VMEM_LIMIT_BYTES),
        name="mix_ffn",
    )(xs, mod, att, ysc, z, yf, yb, normg, wout, gpost, gpre2, w1, w2, gpost2)


def _rotate_cols(w):
    a, b, c, d = jnp.split(w, 4, axis=-1)
    return jnp.concatenate([-b, a, -d, c], axis=-1)


def _pad_cols(w, width):
    return jnp.pad(w, ((0, 0), (0, width - w.shape[-1])))


def _relayout_w_in(w_in):
    d = w_in.shape[0]
    w_kr = w_in[:, Q_LORA + KV_LORA:IN_MLA]
    gap = jnp.zeros((d, LANES // 2 - QK_ROPE), w_in.dtype)
    kr_block = jnp.concatenate([_rotate_cols(w_kr), gap, w_kr, gap], axis=1)
    w_dt = w_in[:, IN_MLA + IN_SC + SSD_WIDTH + SSD_XBC:]
    return jnp.concatenate([
        w_in[:, :Q_LORA + KV_LORA], kr_block,
        w_in[:, IN_MLA:IN_MLA + IN_SC + SSD_WIDTH + SSD_XBC],
        _pad_cols(w_dt, LANES)], axis=1)


def _relayout_w_uq(w_uq):
    dqk = QK_NOPE + QK_ROPE
    plain, rot = [], []
    zeros_nope = jnp.zeros((w_uq.shape[0], QK_NOPE), w_uq.dtype)
    for h in range(MLA_HEADS):
        wh = w_uq[:, h * dqk:(h + 1) * dqk]
        plain.append(_pad_cols(wh, HEAD_PAD))
        rot.append(_pad_cols(jnp.concatenate([zeros_nope, _rotate_cols(wh[:, QK_NOPE:])], axis=1), HEAD_PAD))
    return jnp.concatenate(plain + rot, axis=1)


def _relayout_w_ukv(w_ukv):
    dkv = QK_NOPE + V_HEAD
    ks, vs = [], []
    for h in range(MLA_HEADS):
        wh = w_ukv[:, h * dkv:(h + 1) * dkv]
        ks.append(_pad_cols(wh[:, :QK_NOPE], HEAD_PAD))
        vs.append(wh[:, QK_NOPE:])
    return jnp.concatenate(ks + vs, axis=1)


def _rope_tables(n_ctx, seq):
    half = QK_ROPE // 2
    inv_freq = ROPE_THETA ** (-jnp.arange(0, half, 2, dtype=F32) / half)
    rows = seq // GRID_W
    row = jnp.repeat(jnp.arange(rows, dtype=F32), GRID_W)
    col = jnp.tile(jnp.arange(GRID_W, dtype=F32), rows)
    ang_r = row[:, None] * inv_freq
    ang_c = col[:, None] * inv_freq
    ang = jnp.concatenate([ang_r, ang_r, ang_c, ang_c], axis=-1)
    cos = jnp.concatenate([jnp.ones((n_ctx, QK_ROPE), F32), jnp.cos(ang)], axis=0)
    sin = jnp.concatenate([jnp.zeros((n_ctx, QK_ROPE), F32), jnp.sin(ang)], axis=0)
    t = n_ctx + seq
    cos_t = jnp.ones((t, LANES), F32).at[:, ROPE_LANE0:ROPE_LANE0 + QK_ROPE].set(cos)
    sin_t = jnp.zeros((t, LANES), F32).at[:, ROPE_LANE0:ROPE_LANE0 + QK_ROPE].set(sin)
    return cos_t, sin_t


def kernel(x, c, ctx, c_ctx, w_mod, b_mod, g_pre_mix, w_in, mla_q_norm, w_uq, mla_kv_norm, w_ukv, sc_conv_w, ssd_conv_w, ssd_conv_b, ssd_a_log, ssd_dt_bias, ssd_d, ssd_norm, w_out, g_post_mix, g_pre_ffn, w_ff1, w_ff2, g_post_ffn):
    bsz, seq, d = x.shape
    n_ctx = ctx.shape[1]
    depth = w_mod.shape[0]
    assert n_ctx == TOKEN_TILE and seq % TOKEN_TILE == 0 and seq % GRID_W == 0
    ctx_row = bsz

    cvec = jnp.zeros((SUBLANES, d), F32).at[:bsz].set(c).at[ctx_row].set(c_ctx)
    mod_all = _modulation(cvec, w_mod, b_mod).reshape(depth, SUBLANES, N_MOD, d)
    cos_t, sin_t = _rope_tables(n_ctx, seq)

    xs = jnp.concatenate([ctx, x], axis=1)
    for i in range(depth):
        last = i == depth - 1
        win = _relayout_w_in(w_in[i]).astype(BF16)
        wuq = _relayout_w_uq(w_uq[i]).astype(BF16)
        wukv = _relayout_w_ukv(w_ukv[i]).astype(BF16)
        q, k, v, ysc, z, act, dt = _inproj(
            xs, mod_all[i], g_pre_mix[i][None], win, mla_q_norm[i][None], wuq, mla_kv_norm[i][None], wukv,
            cos_t, sin_t, sc_conv_w[i], ssd_conv_w[i], ssd_conv_b[i][None], n_ctx, ctx_row)
        att = _attention(q, k, v, n_ctx, ctx_queries=not last)
        alog = _pad_cols(ssd_a_log[i].reshape(1, -1), LANES)
        dtbias = _pad_cols(ssd_dt_bias[i].reshape(1, -1), LANES)
        dskip = jnp.repeat(ssd_d[i], SSD_HEAD_DIM)[None]
        yf, yb = _ssd_scan(act, dt, alog, dtbias, dskip, n_ctx)
        xs = _mix_ffn(xs, mod_all[i], att, ysc, z, yf, yb, ssd_norm[i][None],
                      w_out[i].astype(BF16), g_post_mix[i][None], g_pre_ffn[i][None],
                      w_ff1[i].astype(BF16), w_ff2[i].astype(BF16), g_post_ffn[i][None],
                      n_ctx, ctx_row, latent_only=last)
    return xs
```

```python
import functools
import math

import jax
import jax.numpy as jnp
import numpy as np
from jax import lax
from jax.experimental import pallas as pl
from jax.experimental.pallas import tpu as pltpu

F32 = jnp.float32
BF16 = jnp.bfloat16

GRID_W = 64
EPS = 1e-6
N_MOD = 6
MLA_HEADS = 4
Q_LORA = 256
KV_LORA = 128
QK_NOPE = 64
QK_ROPE = 32
V_HEAD = 64
MLA_WIDTH = MLA_HEADS * V_HEAD
MLA_SCALE = (QK_NOPE + QK_ROPE) ** -0.5
ROPE_THETA = 10000.0
SC_WIDTH = 256
SSD_HEADS = 8
SSD_HEAD_DIM = 64
SSD_WIDTH = SSD_HEADS * SSD_HEAD_DIM
SSD_GROUPS = 2
SSD_STATE = 64
SSD_CHUNK = 128
SSD_GN = SSD_GROUPS * SSD_STATE
SSD_XBC = SSD_WIDTH + 2 * SSD_GN
HEADS_PER_GROUP = SSD_HEADS // SSD_GROUPS
IN_MLA = Q_LORA + KV_LORA + QK_ROPE
IN_SC = 3 * SC_WIDTH

LANES = 128
SUBLANES = 8
VMEM_LIMIT_BYTES = 56 * 1024 * 1024

HEAD_PAD = LANES
V_ROWS = V_HEAD + 16
ROPE_LANE0 = QK_NOPE
TOKEN_TILE = 256
TILES_PER_STEP = 2
SSD_BLOCK = 2 * SSD_CHUNK
FF_CHUNK = 1024

C_Q0 = 0
C_KV0 = C_Q0 + Q_LORA
C_KR0 = C_KV0 + KV_LORA
C_SC0 = C_KR0 + LANES
C_Z0 = C_SC0 + IN_SC
C_XBC0 = C_Z0 + SSD_WIDTH
C_DT0 = C_XBC0 + SSD_XBC
D_IN_PAD = C_DT0 + LANES

Q_PRESCALE = MLA_SCALE * math.log2(math.e)

SHIFT_HEADROOM = 60.0
SHIFT_MAX_BOUND = 90.0
BOUND_SLACK = 1.0 + 2.0 ** -8


def _rms(x, g):
    return x * lax.rsqrt(jnp.mean(x * x, axis=-1, keepdims=True) + EPS) * g


def _silu(x):
    return x / (1.0 + jnp.exp(-x))


def _dot(a, b):
    return jnp.dot(a, b, preferred_element_type=F32)


class _TokenLayout:
    def __init__(self, bsz, n_ctx, seq):
        tm = TOKEN_TILE
        assert n_ctx % tm == 0 and seq % tm == 0
        self.bsz = bsz
        self.ctx_tiles = n_ctx // tm
        self.lat_tiles = seq // tm
        self.tiles_per_batch = self.ctx_tiles + self.lat_tiles
        self.n_tiles = bsz * self.tiles_per_batch
        self.ctx_row = bsz

    def unified_tile(self, o, latent_only):
        if not latent_only:
            return o
        return o + (o // self.lat_tiles + 1) * self.ctx_tiles

    def coords(self, u):
        b = u // self.tiles_per_batch
        return b, u - b * self.tiles_per_batch

    def mod_row(self, u):
        b, i = self.coords(u)
        return jnp.where(i < self.ctx_tiles, self.ctx_row, b)

    def seq_edges(self, u):
        _, i = self.coords(u)
        first = (i == 0) | (i == self.ctx_tiles)
        last = (i == self.ctx_tiles - 1) | (i == self.tiles_per_batch - 1)
        return first, last

    def _latent_tile(self, u):
        b, i = self.coords(u)
        return b * self.lat_tiles + jnp.maximum(i - self.ctx_tiles, 0)

    def _context_tile(self, u):
        b, i = self.coords(u)
        return b * self.ctx_tiles + jnp.minimum(i, self.ctx_tiles - 1)

    def source_specs(self, u_of, d, split):
        tm = TOKEN_TILE
        if not split:
            return [pl.BlockSpec((tm, d), lambda j: (u_of(j), 0))]
        return [pl.BlockSpec((tm, d), lambda j: (self._context_tile(u_of(j)), 0)),
                pl.BlockSpec((tm, d), lambda j: (self._latent_tile(u_of(j)), 0))]

    def halo_specs(self, u_of, d, split):
        rb = TOKEN_TILE // SUBLANES
        if split:
            assert self.ctx_tiles == 1
            tile_of, n_rb = (lambda j: self._latent_tile(u_of(j))), self.bsz * self.lat_tiles * rb
        else:
            tile_of, n_rb = u_of, self.n_tiles * rb
        return [pl.BlockSpec((SUBLANES, d), lambda j: (jnp.maximum(tile_of(j) * rb - 1, 0), 0)),
                pl.BlockSpec((SUBLANES, d), lambda j: (jnp.minimum((tile_of(j) + 1) * rb, n_rb - 1), 0))]

    def select_source(self, u, refs):
        if len(refs) == 1:
            return refs[0][...]
        _, i = self.coords(u)
        return jnp.where(i < self.ctx_tiles, refs[0][...], refs[1][...])


def _mod_kernel(c_ref, w_ref, b_ref, o_ref):
    s = _silu(c_ref[...]).astype(BF16)
    o_ref[0] = _dot(s, w_ref[0].astype(BF16)) + b_ref[0]


def _modulation(cvec, w_mod, b_mod):
    depth, d, nd = w_mod.shape
    rows = cvec.shape[0]
    return pl.pallas_call(
        _mod_kernel,
        out_shape=jax.ShapeDtypeStruct((depth, rows, nd), F32),
        grid=(depth, nd // d),
        in_specs=[
            pl.BlockSpec((rows, d), lambda l, j: (0, 0)),
            pl.BlockSpec((1, d, d), lambda l, j: (l, 0, j)),
            pl.BlockSpec((1, 1, d), lambda l, j: (l, 0, j)),
        ],
        out_specs=pl.BlockSpec((1, rows, d), lambda l, j: (l, 0, j)),
        compiler_params=pltpu.CompilerParams(dimension_semantics=("parallel", "parallel")),
        name="modulation",
    )(cvec, w_mod, b_mod.reshape(depth, 1, nd))


def _conv3_rows(a_ext, w_ref, tm):
    te = a_ext.shape[0]
    lo, hi = SUBLANES, SUBLANES + tm
    prev = pltpu.roll(a_ext, 1, 0)[lo:hi]
    nxt = pltpu.roll(a_ext, te - 1, 0)[lo:hi]
    return w_ref[0:1, :] * prev + w_ref[1:2, :] * a_ext[lo:hi] + w_ref[2:3, :] * nxt


N_INPROJ_PARAMS = 9
N_INPROJ_TOKEN_INPUTS = 5


def _inproj_kernel(*refs, layout, split):
    n_x = 2 if split else 1
    per_half = n_x + N_INPROJ_TOKEN_INPUTS
    n_in = TILES_PER_STEP * per_half
    params = refs[n_in:n_in + N_INPROJ_PARAMS]
    outs = refs[n_in + N_INPROJ_PARAMS:]
    for h in range(TILES_PER_STEP):
        half = refs[h * per_half:(h + 1) * per_half]
        u = pl.program_id(0) * TILES_PER_STEP + h
        x = layout.select_source(u, half[:n_x])
        _inproj_tile(x, *half[n_x:], params, outs, h, layout.seq_edges(u))


def _inproj_tile(x, xp_ref, xn_ref, mod_ref, cos_ref, sin_ref, params, outs, slot, seq_edges):
    gpre_ref, win_ref, qg_ref, wuq_ref, kvg_ref, wukv_ref, scw_ref, cw_ref, cb_ref = params
    q_ref, k_ref, v_ref, kn_ref, ysc_ref, z_ref, act_ref, dt_ref = outs
    tm = x.shape[0]
    rows = slice(slot * tm, (slot + 1) * tm)
    first_of_seq, last_of_seq = seq_edges
    shift = mod_ref[0, 0:1, :]
    scale = mod_ref[0, 1:2, :]

    def norm_mod(xx):
        return _rms(xx, gpre_ref[...]) * (1.0 + scale) + shift

    h_prev = jnp.where(first_of_seq, 0.0, norm_mod(xp_ref[...]))
    h_next = jnp.where(last_of_seq, 0.0, norm_mod(xn_ref[...]))
    h_ext = jnp.concatenate([h_prev, norm_mod(x), h_next], axis=0).astype(BF16)
    u_ext = _dot(h_ext, win_ref[...])
    u = u_ext[SUBLANES:SUBLANES + tm]

    z_ref[rows, :] = u[:, C_Z0:C_XBC0].astype(z_ref.dtype)
    dt_ref[rows, :] = u[:, C_DT0:D_IN_PAD]

    conv = _conv3_rows(u_ext[:, C_XBC0:C_DT0], cw_ref, tm) + cb_ref[...]
    act_ref[rows, :] = _silu(conv).astype(act_ref.dtype)

    w = SC_WIDTH
    prod = u_ext[:, C_SC0 + w:C_SC0 + 2 * w] * u_ext[:, C_SC0 + 2 * w:C_Z0]
    ysc_ref[rows, :] = (u[:, C_SC0:C_SC0 + w] * _conv3_rows(prod, scw_ref, tm)).astype(ysc_ref.dtype)

    cos = cos_ref[...]
    sin = sin_ref[...]
    nh = MLA_HEADS
    cos_h = jnp.concatenate([cos] * nh, axis=1)
    sin_h = jnp.concatenate([sin] * nh, axis=1)

    cq = _rms(u[:, C_Q0:C_KV0], qg_ref[...]).astype(BF16)
    q2 = _dot(cq, wuq_ref[...])
    qw = nh * HEAD_PAD
    q = (q2[:, :qw] * cos_h + q2[:, qw:] * sin_h) * Q_PRESCALE
    q_ref[rows, :] = q.astype(BF16)

    ckv = _rms(u[:, C_KV0:C_KR0], kvg_ref[...]).astype(BF16)
    kv = _dot(ckv, wukv_ref[...])
    lane = lax.broadcasted_iota(jnp.int32, (1, LANES), 1)
    rope_lane = (lane >= ROPE_LANE0) & (lane < ROPE_LANE0 + QK_ROPE)
    cos_k = jnp.where(rope_lane, cos, 0.0)
    krb = u[:, C_KR0:C_SC0]
    kr = krb * cos_k + pltpu.roll(krb, LANES // 2, 1) * sin
    k = kv[:, :qw] + jnp.concatenate([kr] * nh, axis=1)
    k_b = k.astype(BF16)
    k_ref[rows, :] = k_b
    k_f = k_b.astype(F32)
    norms = []
    for h in range(nh):
        k_h = k_f[:, h * HEAD_PAD:(h + 1) * HEAD_PAD]
        n2 = jnp.max(jnp.sum(k_h * k_h, axis=1, keepdims=True), axis=0, keepdims=True)
        norms.append(jnp.broadcast_to(n2, (1, LANES)))
    kn_ref[slot] = jnp.concatenate(norms + [jnp.zeros((SUBLANES - nh, LANES), F32)], axis=0)
    v_ref[rows, :] = kv[:, qw:].astype(BF16)


def _inproj(src, mod, gpre, win, qg, wuq, kvg, wukv, cos_t, sin_t, scw, cw, cb, layout):
    tm = TOKEN_TILE
    d = src[0].shape[-1]
    split = len(src) == 2
    qw = MLA_HEADS * HEAD_PAD
    nt = layout.n_tiles
    assert nt % TILES_PER_STEP == 0
    rows = nt * tm

    def const(shape):
        return pl.BlockSpec(shape, lambda j: (0,) * len(shape))

    in_specs, operands = [], []
    for h in range(TILES_PER_STEP):
        def u_of(j, h=h):
            return j * TILES_PER_STEP + h

        def rope(j, u_of=u_of):
            return (layout.coords(u_of(j))[1], 0)

        in_specs += layout.source_specs(u_of, d, split) + layout.halo_specs(u_of, d, split)
        in_specs += [
            pl.BlockSpec((1, N_MOD, d), lambda j, u_of=u_of: (layout.mod_row(u_of(j)), 0, 0)),
            pl.BlockSpec((tm, LANES), rope), pl.BlockSpec((tm, LANES), rope),
        ]
        operands += list(src) + [src[-1], src[-1], mod, cos_t, sin_t]
    consts = [gpre, win, qg, wuq, kvg, wukv, scw, cw, cb]
    in_specs += [const(c.shape) for c in consts]

    def tok(width):
        return pl.BlockSpec((TILES_PER_STEP * tm, width), lambda j: (j, 0))

    out_shape = (
        jax.ShapeDtypeStruct((rows, qw), BF16),
        jax.ShapeDtypeStruct((rows, qw), BF16),
        jax.ShapeDtypeStruct((rows, MLA_WIDTH), BF16),
        jax.ShapeDtypeStruct((nt, SUBLANES, LANES), F32),
        jax.ShapeDtypeStruct((rows, SC_WIDTH), BF16),
        jax.ShapeDtypeStruct((rows, SSD_WIDTH), BF16),
        jax.ShapeDtypeStruct((rows, SSD_XBC), BF16),
        jax.ShapeDtypeStruct((rows, LANES), F32),
    )
    return pl.pallas_call(
        functools.partial(_inproj_kernel, layout=layout, split=split),
        out_shape=out_shape,
        grid=(nt // TILES_PER_STEP,),
        in_specs=in_specs,
        out_specs=(tok(qw), tok(qw), tok(MLA_WIDTH),
                   pl.BlockSpec((TILES_PER_STEP, SUBLANES, LANES), lambda j: (j, 0, 0)),
                   tok(SC_WIDTH), tok(SSD_WIDTH), tok(SSD_XBC), tok(LANES)),
        compiler_params=pltpu.CompilerParams(
            dimension_semantics=("parallel",), vmem_limit_bytes=VMEM_LIMIT_BYTES),
        name="inproj",
    )(*operands, *consts)


def _attn_kernel(q_ref, k_ref, v_ref, kn_ref, o_ref, vt_ref, s_ref, p_ref, *, n_keys, n_ctx, ctx_queries):
    i = pl.program_id(1)
    tq = q_ref.shape[1]

    @pl.when(i == 0)
    def _():
        for c in range(n_keys // tq):
            v_t = v_ref[0, c * tq:(c + 1) * tq, :].astype(F32).T.astype(BF16)
            for h in range(MLA_HEADS):
                vt_ref[h, 0:V_HEAD, c * tq:(c + 1) * tq] = v_t[h * V_HEAD:(h + 1) * V_HEAD]
        for h in range(MLA_HEADS):
            vt_ref[h, V_HEAD:V_ROWS, :] = jnp.ones((V_ROWS - V_HEAD, n_keys), BF16)

    q_t = q_ref[0].astype(F32).T.astype(BF16)
    kn = jnp.max(kn_ref[0], axis=0)

    def finish(outs):
        o_ref[0] = jnp.concatenate(outs, axis=0).T.astype(o_ref.dtype)

    def head_out(ov):
        return ov[0:V_HEAD] / ov[V_HEAD:V_HEAD + 1]

    def attend_two_pass(nk):
        def scores(h):
            s_ref[h, 0:nk, :] = _dot(k_ref[0, 0:nk, h * HEAD_PAD:(h + 1) * HEAD_PAD],
                                     q_t[h * HEAD_PAD:(h + 1) * HEAD_PAD, :])

        def probs(h):
            s = s_ref[h, 0:nk, :]
            p_ref[h, 0:nk, :] = jnp.exp2(s - jnp.max(s, axis=0, keepdims=True)).astype(BF16)

        def values(h):
            return head_out(_dot(vt_ref[h, :, 0:nk], p_ref[h, 0:nk, :]))

        scores(0), scores(1)
        scores(2), scores(3), probs(0), probs(1)
        outs = [values(0), values(1)]
        probs(2), probs(3)
        outs += [values(2), values(3)]
        finish(outs)

    def attend_one_pass(nk, shift):
        outs = []
        for pair in range(MLA_HEADS // 2):
            heads = (2 * pair, 2 * pair + 1)
            s = [_dot(k_ref[0, 0:nk, h * HEAD_PAD:(h + 1) * HEAD_PAD],
                      q_t[h * HEAD_PAD:(h + 1) * HEAD_PAD, :]) for h in heads]
            p = [jnp.exp2(sh - shift[h]).astype(BF16) for h, sh in zip(heads, s)]
            outs += [head_out(_dot(vt_ref[h, :, 0:nk], ph)) for h, ph in zip(heads, p)]
        finish(outs)

    def attend(nk):
        bounds = []
        for h in range(MLA_HEADS):
            qf = q_t[h * HEAD_PAD:(h + 1) * HEAD_PAD, :].astype(F32)
            q2 = jnp.sum(qf * qf, axis=0, keepdims=True)
            bounds.append(jnp.sqrt(q2 * kn[h:h + 1, 0:1]) * BOUND_SLACK)
        worst = jnp.max(jnp.concatenate(bounds, axis=0))
        one_pass = worst <= SHIFT_MAX_BOUND

        @pl.when(one_pass)
        def _():
            attend_one_pass(nk, [b - SHIFT_HEADROOM for b in bounds])

        @pl.when(jnp.logical_not(one_pass))
        def _():
            attend_two_pass(nk)

    if ctx_queries:
        @pl.when(i == 0)
        def _():
            attend_two_pass(n_ctx)

        @pl.when(i > 0)
        def _():
            attend(n_keys)
    else:
        attend(n_keys)


def _attention(q, k, v, kn, n_ctx, ctx_queries):
    bsz, t, qw = k.shape
    tq = TOKEN_TILE
    t0 = 0 if ctx_queries else n_ctx // tq
    nq = t // tq - t0
    return pl.pallas_call(
        functools.partial(_attn_kernel, n_keys=t, n_ctx=n_ctx, ctx_queries=ctx_queries),
        out_shape=jax.ShapeDtypeStruct((bsz, nq * tq, MLA_WIDTH), BF16),
        grid=(bsz, nq),
        in_specs=[
            pl.BlockSpec((1, tq, qw), lambda b, i: (b, i + t0, 0)),
            pl.BlockSpec((1, t, qw), lambda b, i: (b, 0, 0)),
            pl.BlockSpec((1, t, MLA_WIDTH), lambda b, i: (b, 0, 0)),
            pl.BlockSpec((1,) + kn.shape[1:], lambda b, i: (b, 0, 0, 0)),
        ],
        out_specs=pl.BlockSpec((1, tq, MLA_WIDTH), lambda b, i: (b, i, 0)),
        scratch_shapes=[pltpu.VMEM((MLA_HEADS, V_ROWS, t), BF16),
                        pltpu.VMEM((MLA_HEADS, t, tq), F32), pltpu.VMEM((MLA_HEADS, t, tq), BF16)],
        compiler_params=pltpu.CompilerParams(
            dimension_semantics=("arbitrary", "arbitrary"), vmem_limit_bytes=VMEM_LIMIT_BYTES),
        name="attention",
    )(q, k, v, kn)


def _split3(x):
    hi = x.astype(BF16)
    r = x - hi.astype(F32)
    mid = r.astype(BF16)
    lo = (r - mid.astype(F32)).astype(BF16)
    return hi, mid, lo


def _ssd_role(act, dt_raw, direction, alog_ref, dtbias_ref, dskip_ref, y_ref, row0, h_ref):
    tc = SSD_CHUNK
    n = SSD_STATE
    xs = act[:, :SSD_WIDTH]
    bm = act[:, SSD_WIDTH:SSD_WIDTH + SSD_GN].astype(F32)
    cm = act[:, SSD_WIDTH + SSD_GN:]
    bm_t = bm.T

    a = -jnp.exp(alog_ref[...])
    xb = dt_raw + dtbias_ref[...]
    dt = jnp.maximum(xb, 0.0) + jnp.log(1.0 + jnp.exp(-jnp.abs(xb)))
    la = dt * a

    r_i = lax.broadcasted_iota(jnp.int32, (tc, tc), 0)
    c_i = lax.broadcasted_iota(jnp.int32, (tc, tc), 1)
    tri = (c_i <= r_i) if direction == 0 else (c_i >= r_i)
    tri_b = jnp.where(tri, 1.0, 0.0).astype(BF16)
    hi, mid, lo = _split3(la)
    cs = _dot(tri_b, hi) + _dot(tri_b, mid) + _dot(tri_b, lo)
    end = tc - 1 if direction == 0 else 0
    nd = 2 * SSD_HEADS
    cs_t = cs.T[0:nd]
    dt_t = dt.T[0:nd]
    w_t = jnp.exp(cs_t[:, end:end + 1] - cs_t) * dt_t
    dec_tot = jnp.exp(cs[end:end + 1, :])
    cm_f = cm.astype(F32)
    lane = lax.broadcasted_iota(jnp.int32, (1, LANES), 1)
    low_half = lane < SSD_HEAD_DIM
    zeros_h = jnp.zeros((n, LANES), BF16)
    zeros_lhs = jnp.zeros((n, tc), F32)

    for g in range(SSD_GROUPS):
        cb = _dot(cm[:, g * n:(g + 1) * n], bm_t[g * n:(g + 1) * n, :].astype(BF16))
        bt_g = bm_t[g * n:(g + 1) * n, :]
        for pr in range(HEADS_PER_GROUP // 2):
            slab = g * (HEADS_PER_GROUP // 2) + pr
            h_pair = h_ref[direction, g, :, pr * LANES:(pr + 1) * LANES]
            h_b = h_pair.astype(BF16)
            xs_pair = xs[:, slab * LANES:(slab + 1) * LANES]
            rhs = jnp.concatenate([xs_pair] + [h_b if gg == g else zeros_h for gg in range(SSD_GROUPS)], axis=0)
            lhs_rows = []
            cols = []
            for hh in range(2):
                col = direction * SSD_HEADS + 2 * slab + hh
                cols.append(col)
                cs_col = jnp.broadcast_to(cs[:, col:col + 1], (tc, tc))
                lm = jnp.exp(jnp.where(tri, cs_col - cs_t[col:col + 1, :], -jnp.inf))
                top = jnp.concatenate([cb * dt_t[col:col + 1, :] * lm, cm_f * jnp.exp(cs_col)], axis=1)
                bot = jnp.concatenate([bt_g * w_t[col:col + 1, :], zeros_lhs], axis=1)
                lhs_rows += [top, bot]
            lhs = jnp.concatenate(lhs_rows, axis=0).astype(BF16)
            out = _dot(lhs, rhs)
            m = tc + n
            y_pair = jnp.where(low_half, out[0:tc], out[m:m + tc])
            if direction == 0:
                y_pair = y_pair + dskip_ref[:, slab * LANES:(slab + 1) * LANES] * xs_pair.astype(F32)
            y_ref[0, row0:row0 + tc, slab * LANES:(slab + 1) * LANES] = y_pair
            h_new = jnp.where(low_half, out[tc:m], out[m + tc:2 * m])
            keep = jnp.where(low_half, dec_tot[:, cols[0]:cols[0] + 1], dec_tot[:, cols[1]:cols[1] + 1])
            h_ref[direction, g, :, pr * LANES:(pr + 1) * LANES] = h_pair * keep + h_new


def _bwd_block(i, n_blocks, ctx_blocks):
    return jnp.where(i < ctx_blocks, ctx_blocks - 1 - i, n_blocks - 1 + ctx_blocks - i)


def _ssd_kernel(af_ref, ab_ref, dtf_ref, dtb_ref, alog_ref, dtbias_ref, dskip_ref,
                yf_ref, yb_ref, h_ref):
    @pl.when(pl.program_id(1) == 0)
    def _():
        h_ref[...] = jnp.zeros_like(h_ref)

    params = (alog_ref, dtbias_ref, dskip_ref)
    tc = SSD_CHUNK
    n_sub = af_ref.shape[1] // tc
    for j in range(n_sub):
        r0 = j * tc
        _ssd_role(af_ref[0, r0:r0 + tc, :], dtf_ref[0, r0:r0 + tc, :], 0, *params, yf_ref, r0, h_ref)
        r1 = (n_sub - 1 - j) * tc
        _ssd_role(ab_ref[0, r1:r1 + tc, :], dtb_ref[0, r1:r1 + tc, :], 1, *params, yb_ref, r1, h_ref)


def _ssd_scan(act, dt, alog, dtbias, dskip, n_ctx):
    bsz, t, _ = act.shape
    tc = SSD_BLOCK
    n_blocks = t // tc
    ctx_blocks = n_ctx // tc
    assert n_ctx % tc == 0 and t % tc == 0

    def fwd(i):
        return i

    def bwd(i):
        return _bwd_block(i, n_blocks, ctx_blocks)

    def main(order, width):
        return pl.BlockSpec((1, tc, width), lambda b, i: (b, order(i), 0))

    def const(shape):
        return pl.BlockSpec(shape, lambda b, i: (0,) * len(shape))

    y_shape = jax.ShapeDtypeStruct((bsz, t, SSD_WIDTH), F32)
    return pl.pallas_call(
        _ssd_kernel,
        out_shape=(y_shape, y_shape),
        grid=(bsz, n_blocks),
        in_specs=[
            main(fwd, SSD_XBC), main(bwd, SSD_XBC), main(fwd, LANES), main(bwd, LANES),
            const(alog.shape), const(dtbias.shape), const(dskip.shape),
        ],
        out_specs=(main(fwd, SSD_WIDTH), main(bwd, SSD_WIDTH)),
        scratch_shapes=[pltpu.VMEM((2, SSD_GROUPS, SSD_STATE, HEADS_PER_GROUP * SSD_HEAD_DIM), F32)],
        compiler_params=pltpu.CompilerParams(
            dimension_semantics=("arbitrary", "arbitrary"), vmem_limit_bytes=VMEM_LIMIT_BYTES),
        name="ssd_scan",
    )(act, act, dt, dt, alog, dtbias, dskip)


def _mix_ffn_tile(x, mod_ref, att_ref, ysc_ref, z_ref, yf_ref, yb_ref, params, o_ref, row0, ycat_ref, acc_ref):
    normg_ref, wout_ref, gpost_ref, gpre2_ref, w1_ref, w2_ref, gpost2_ref = params
    tm = x.shape[0]
    gated = (yf_ref[...] + yb_ref[...]) * _silu(z_ref[...].astype(F32))
    gw = SSD_WIDTH // SSD_GROUPS
    w = SC_WIDTH
    ycat_ref[:, 0:MLA_WIDTH] = att_ref[...]
    ycat_ref[:, MLA_WIDTH:MLA_WIDTH + w] = ysc_ref[...]
    c0 = MLA_WIDTH + w
    for g in range(SSD_GROUPS):
        gg = gated[:, g * gw:(g + 1) * gw]
        gg = gg * lax.rsqrt(jnp.mean(gg * gg, axis=-1, keepdims=True) + EPS) * normg_ref[:, g * gw:(g + 1) * gw]
        ycat_ref[:, c0 + g * gw:c0 + (g + 1) * gw] = gg.astype(BF16)

    gate1 = mod_ref[0, 2:3, :]
    shift2 = mod_ref[0, 3:4, :]
    scale2 = mod_ref[0, 4:5, :]
    gate2 = mod_ref[0, 5:6, :]
    y = _dot(ycat_ref[...], wout_ref[...])
    x1 = x + gate1 * _rms(y, gpost_ref[...])

    h2 = (_rms(x1, gpre2_ref[...]) * (1.0 + scale2) + shift2).astype(BF16)
    d_ff = w1_ref.shape[1]
    for c in range(d_ff // FF_CHUNK):
        a = _dot(h2, w1_ref[:, c * FF_CHUNK:(c + 1) * FF_CHUNK])
        r = jnp.square(jnp.maximum(a, 0.0)).astype(BF16)
        part = _dot(r, w2_ref[c * FF_CHUNK:(c + 1) * FF_CHUNK, :])
        if c == 0:
            acc_ref[...] = part
        else:
            acc_ref[...] += part
    o_ref[row0:row0 + tm, :] = x1 + gate2 * _rms(acc_ref[...], gpost2_ref[...])


N_MIX_PARAMS = 7
N_MIX_TOKEN_INPUTS = 6


def _mix_ffn_kernel(*refs, layout, latent_only, split):
    n_x = 2 if split else 1
    per_half = n_x + N_MIX_TOKEN_INPUTS
    n_in = TILES_PER_STEP * per_half
    params = refs[n_in:n_in + N_MIX_PARAMS]
    o_ref = refs[n_in + N_MIX_PARAMS]
    ycat_ref, acc_ref = refs[n_in + N_MIX_PARAMS + 1:]
    tm = TOKEN_TILE
    for h in range(TILES_PER_STEP):
        half = refs[h * per_half:(h + 1) * per_half]
        u = layout.unified_tile(pl.program_id(0) * TILES_PER_STEP + h, latent_only)
        x = layout.select_source(u, half[:n_x])
        _mix_ffn_tile(x, *half[n_x:], params, o_ref, h * tm, ycat_ref.at[h], acc_ref.at[h])


def _mix_ffn(src, mod, att, ysc, z, yf, yb, normg, wout, gpost, gpre2, w1, w2, gpost2, layout, latent_only):
    tm = TOKEN_TILE
    d = src[0].shape[-1]
    n_out_tiles = layout.bsz * (layout.lat_tiles if latent_only else layout.tiles_per_batch)
    assert n_out_tiles % TILES_PER_STEP == 0
    att_unified = att.shape[0] == layout.bsz * layout.tiles_per_batch * tm

    def const(shape):
        return pl.BlockSpec(shape, lambda j: (0,) * len(shape), pipeline_mode=pl.Buffered(1))

    in_specs, operands = [], []
    for h in range(TILES_PER_STEP):
        def u_of(j, h=h):
            return layout.unified_tile(j * TILES_PER_STEP + h, latent_only)

        def tok(width, tile_of=u_of):
            return pl.BlockSpec((tm, width), lambda j: (tile_of(j), 0))

        in_specs += layout.source_specs(u_of, d, len(src) == 2)
        in_specs += [
            pl.BlockSpec((1, N_MOD, d), lambda j, u_of=u_of: (layout.mod_row(u_of(j)), 0, 0)),
            tok(MLA_WIDTH) if att_unified else tok(MLA_WIDTH, lambda j, h=h: j * TILES_PER_STEP + h),
            tok(SC_WIDTH), tok(SSD_WIDTH), tok(SSD_WIDTH), tok(SSD_WIDTH),
        ]
        operands += list(src) + [mod, att, ysc, z, yf, yb]
    consts = [normg, wout, gpost, gpre2, w1, w2, gpost2]
    in_specs += [const(c.shape) for c in consts]
    return pl.pallas_call(
        functools.partial(_mix_ffn_kernel, layout=layout, latent_only=latent_only, split=len(src) == 2),
        out_shape=jax.ShapeDtypeStruct((n_out_tiles * tm, d), F32),
        grid=(n_out_tiles // TILES_PER_STEP,),
        in_specs=in_specs,
        out_specs=pl.BlockSpec((TILES_PER_STEP * tm, d), lambda j: (j, 0)),
        scratch_shapes=[pltpu.VMEM((TILES_PER_STEP, tm, d), BF16), pltpu.VMEM((TILES_PER_STEP, tm, d), F32)],
        compiler_params=pltpu.CompilerParams(
            dimension_semantics=("parallel",), vmem_limit_bytes=VMEM_LIMIT_BYTES),
        name="mix_ffn",
    )(*operands, *consts)


def _rotate_cols(w):
    a, b, c, d = jnp.split(w, 4, axis=-1)
    return jnp.concatenate([-b, a, -d, c], axis=-1)


def _pad_cols(w, width):
    return jnp.pad(w, ((0, 0), (0, width - w.shape[-1])))


def _relayout_w_in(w_in):
    d = w_in.shape[0]
    w_kr = w_in[:, Q_LORA + KV_LORA:IN_MLA]
    gap = jnp.zeros((d, LANES // 2 - QK_ROPE), w_in.dtype)
    kr_block = jnp.concatenate([_rotate_cols(w_kr), gap, w_kr, gap], axis=1)
    w_dt = w_in[:, IN_MLA + IN_SC + SSD_WIDTH + SSD_XBC:]
    return jnp.concatenate([
        w_in[:, :Q_LORA + KV_LORA], kr_block,
        w_in[:, IN_MLA:IN_MLA + IN_SC + SSD_WIDTH + SSD_XBC],
        _pad_cols(w_dt, LANES)], axis=1)


def _relayout_w_uq(w_uq):
    dqk = QK_NOPE + QK_ROPE
    plain, rot = [], []
    zeros_nope = jnp.zeros((w_uq.shape[0], QK_NOPE), w_uq.dtype)
    for h in range(MLA_HEADS):
        wh = w_uq[:, h * dqk:(h + 1) * dqk]
        plain.append(_pad_cols(wh, HEAD_PAD))
        rot.append(_pad_cols(jnp.concatenate([zeros_nope, _rotate_cols(wh[:, QK_NOPE:])], axis=1), HEAD_PAD))
    return jnp.concatenate(plain + rot, axis=1)


def _relayout_w_ukv(w_ukv):
    dkv = QK_NOPE + V_HEAD
    ks, vs = [], []
    for h in range(MLA_HEADS):
        wh = w_ukv[:, h * dkv:(h + 1) * dkv]
        ks.append(_pad_cols(wh[:, :QK_NOPE], HEAD_PAD))
        vs.append(wh[:, QK_NOPE:])
    return jnp.concatenate(ks + vs, axis=1)


def _rope_tables(n_ctx, seq):
    f32 = np.float32
    half = QK_ROPE // 2
    inv_freq = (f32(ROPE_THETA) ** (-np.arange(0, half, 2, dtype=f32) / f32(half))).astype(f32)
    rows = seq // GRID_W
    row = np.repeat(np.arange(rows, dtype=f32), GRID_W)
    col = np.tile(np.arange(GRID_W, dtype=f32), rows)
    ang_r = row[:, None] * inv_freq
    ang_c = col[:, None] * inv_freq
    ang = np.concatenate([ang_r, ang_r, ang_c, ang_c], axis=-1).astype(f32)
    t = n_ctx + seq
    cos_t = np.ones((t, LANES), f32)
    sin_t = np.zeros((t, LANES), f32)
    cos_t[n_ctx:, ROPE_LANE0:ROPE_LANE0 + QK_ROPE] = np.cos(ang)
    sin_t[n_ctx:, ROPE_LANE0:ROPE_LANE0 + QK_ROPE] = np.sin(ang)
    return jnp.asarray(cos_t), jnp.asarray(sin_t)


def kernel(x, c, ctx, c_ctx, w_mod, b_mod, g_pre_mix, w_in, mla_q_norm, w_uq, mla_kv_norm, w_ukv, sc_conv_w, ssd_conv_w, ssd_conv_b, ssd_a_log, ssd_dt_bias, ssd_d, ssd_norm, w_out, g_post_mix, g_pre_ffn, w_ff1, w_ff2, g_post_ffn):
    bsz, seq, d = x.shape
    n_ctx = ctx.shape[1]
    depth = w_mod.shape[0]
    assert n_ctx == TOKEN_TILE and seq % TOKEN_TILE == 0 and seq % GRID_W == 0
    layout = _TokenLayout(bsz, n_ctx, seq)
    t = n_ctx + seq

    cvec = jnp.zeros((SUBLANES, d), F32).at[:bsz].set(c).at[layout.ctx_row].set(c_ctx)
    mod_all = _modulation(cvec, w_mod, b_mod).reshape(depth, SUBLANES, N_MOD, d)
    cos_t, sin_t = _rope_tables(n_ctx, seq)

    src = (ctx.reshape(bsz * n_ctx, d), x.reshape(bsz * seq, d))
    for i in range(depth):
        last = i == depth - 1
        win = _relayout_w_in(w_in[i]).astype(BF16)
        wuq = _relayout_w_uq(w_uq[i]).astype(BF16)
        wukv = _relayout_w_ukv(w_ukv[i]).astype(BF16)
        q, k, v, kn, ysc, z, act, dt = _inproj(
            src, mod_all[i], g_pre_mix[i][None], win, mla_q_norm[i][None], wuq, mla_kv_norm[i][None], wukv,
            cos_t, sin_t, sc_conv_w[i], ssd_conv_w[i], ssd_conv_b[i][None], layout)

        def per_batch(a):
            return a.reshape((bsz, t) + a.shape[1:])

        att = _attention(per_batch(q), per_batch(k), per_batch(v),
                         kn.reshape(bsz, layout.tiles_per_batch, SUBLANES, LANES), n_ctx, ctx_queries=not last)
        alog = _pad_cols(ssd_a_log[i].reshape(1, -1), LANES)
        dtbias = _pad_cols(ssd_dt_bias[i].reshape(1, -1), LANES)
        dskip = jnp.repeat(ssd_d[i], SSD_HEAD_DIM)[None]
        yf, yb = _ssd_scan(per_batch(act), per_batch(dt), alog, dtbias, dskip, n_ctx)
        out = _mix_ffn(src, mod_all[i], att.reshape(-1, MLA_WIDTH), ysc, z,
                       yf.reshape(-1, SSD_WIDTH), yb.reshape(-1, SSD_WIDTH), ssd_norm[i][None],
                       w_out[i].astype(BF16), g_post_mix[i][None], g_pre_ffn[i][None],
                       w_ff1[i].astype(BF16), w_ff2[i].astype(BF16), g_post_ffn[i][None],
                       layout, latent_only=last)
        src = (out,)
    return out.reshape(bsz, seq, d)
```

```python
import functools
import math

import jax
import jax.numpy as jnp
import numpy as np
from jax import lax
from jax.experimental import pallas as pl
from jax.experimental.pallas import tpu as pltpu

F32 = jnp.float32
BF16 = jnp.bfloat16

GRID_W = 64
EPS = 1e-6
N_MOD = 6
MLA_HEADS = 4
Q_LORA = 256
KV_LORA = 128
QK_NOPE = 64
QK_ROPE = 32
V_HEAD = 64
MLA_WIDTH = MLA_HEADS * V_HEAD
MLA_SCALE = (QK_NOPE + QK_ROPE) ** -0.5
ROPE_THETA = 10000.0
SC_WIDTH = 256
SSD_HEADS = 8
SSD_HEAD_DIM = 64
SSD_WIDTH = SSD_HEADS * SSD_HEAD_DIM
SSD_GROUPS = 2
SSD_STATE = 64
SSD_CHUNK = 128
SSD_GN = SSD_GROUPS * SSD_STATE
SSD_XBC = SSD_WIDTH + 2 * SSD_GN
HEADS_PER_GROUP = SSD_HEADS // SSD_GROUPS
IN_MLA = Q_LORA + KV_LORA + QK_ROPE
IN_SC = 3 * SC_WIDTH

LANES = 128
SUBLANES = 8
VMEM_LIMIT_BYTES = 56 * 1024 * 1024

HEAD_PAD = LANES
V_ROWS = V_HEAD + 16
ROPE_LANE0 = QK_NOPE
TOKEN_TILE = 256
TILES_PER_STEP = 2
SSD_BLOCK = 2 * SSD_CHUNK
FF_CHUNK = 1024

C_Q0 = 0
C_KV0 = C_Q0 + Q_LORA
C_KR0 = C_KV0 + KV_LORA
C_SC0 = C_KR0 + LANES
C_Z0 = C_SC0 + IN_SC
C_XBC0 = C_Z0 + SSD_WIDTH
D_IN_PAD = C_XBC0 + SSD_XBC
DT_LANE0 = QK_ROPE

Q_PRESCALE = MLA_SCALE * math.log2(math.e)

SHIFT_HEADROOM = 60.0
SHIFT_MAX_BOUND = 90.0
BOUND_SLACK = 1.0 + 2.0 ** -8


def _rms(x, g):
    return x * lax.rsqrt(jnp.mean(x * x, axis=-1, keepdims=True) + EPS) * g


def _silu(x):
    return x / (1.0 + jnp.exp(-x))


def _dot(a, b):
    return jnp.dot(a, b, preferred_element_type=F32)


class _TokenLayout:
    def __init__(self, bsz, n_ctx, seq):
        tm = TOKEN_TILE
        assert n_ctx % tm == 0 and seq % tm == 0
        self.bsz = bsz
        self.ctx_tiles = n_ctx // tm
        self.lat_tiles = seq // tm
        self.tiles_per_batch = self.ctx_tiles + self.lat_tiles
        self.n_tiles = bsz * self.tiles_per_batch
        self.ctx_row = bsz

    def unified_tile(self, o, latent_only):
        if not latent_only:
            return o
        return o + (o // self.lat_tiles + 1) * self.ctx_tiles

    def coords(self, u):
        b = u // self.tiles_per_batch
        return b, u - b * self.tiles_per_batch

    def mod_row(self, u):
        b, i = self.coords(u)
        return jnp.where(i < self.ctx_tiles, self.ctx_row, b)

    def seq_edges(self, u):
        _, i = self.coords(u)
        first = (i == 0) | (i == self.ctx_tiles)
        last = (i == self.ctx_tiles - 1) | (i == self.tiles_per_batch - 1)
        return first, last

    def _latent_tile(self, u):
        b, i = self.coords(u)
        return b * self.lat_tiles + jnp.maximum(i - self.ctx_tiles, 0)

    def _context_tile(self, u):
        b, i = self.coords(u)
        return b * self.ctx_tiles + jnp.minimum(i, self.ctx_tiles - 1)

    def source_specs(self, u_of, d, split):
        tm = TOKEN_TILE
        if not split:
            return [pl.BlockSpec((tm, d), lambda j: (u_of(j), 0))]
        return [pl.BlockSpec((tm, d), lambda j: (self._context_tile(u_of(j)), 0)),
                pl.BlockSpec((tm, d), lambda j: (self._latent_tile(u_of(j)), 0))]

    def halo_specs(self, u_of, d, split):
        rb = TOKEN_TILE // SUBLANES
        if split:
            assert self.ctx_tiles == 1
            tile_of, n_rb = (lambda j: self._latent_tile(u_of(j))), self.bsz * self.lat_tiles * rb
        else:
            tile_of, n_rb = u_of, self.n_tiles * rb
        return [pl.BlockSpec((SUBLANES, d), lambda j: (jnp.maximum(tile_of(j) * rb - 1, 0), 0)),
                pl.BlockSpec((SUBLANES, d), lambda j: (jnp.minimum((tile_of(j) + 1) * rb, n_rb - 1), 0))]

    def select_source(self, u, refs):
        if len(refs) == 1:
            return refs[0][...]
        _, i = self.coords(u)
        return jnp.where(i < self.ctx_tiles, refs[0][...], refs[1][...])


def _mod_kernel(c_ref, w_ref, b_ref, o_ref):
    s = _silu(c_ref[...]).astype(BF16)
    o_ref[0] = _dot(s, w_ref[0].astype(BF16)) + b_ref[0]


def _modulation(cvec, w_mod, b_mod):
    depth, d, nd = w_mod.shape
    rows = cvec.shape[0]
    return pl.pallas_call(
        _mod_kernel,
        out_shape=jax.ShapeDtypeStruct((depth, rows, nd), F32),
        grid=(depth, nd // d),
        in_specs=[
            pl.BlockSpec((rows, d), lambda l, j: (0, 0)),
            pl.BlockSpec((1, d, d), lambda l, j: (l, 0, j)),
            pl.BlockSpec((1, 1, d), lambda l, j: (l, 0, j)),
        ],
        out_specs=pl.BlockSpec((1, rows, d), lambda l, j: (l, 0, j)),
        compiler_params=pltpu.CompilerParams(dimension_semantics=("parallel", "parallel")),
        name="modulation",
    )(cvec, w_mod, b_mod.reshape(depth, 1, nd))


def _conv3_rows(a_ext, w_ref, tm):
    te = a_ext.shape[0]
    lo, hi = SUBLANES, SUBLANES + tm
    prev = pltpu.roll(a_ext, 1, 0)[lo:hi]
    nxt = pltpu.roll(a_ext, te - 1, 0)[lo:hi]
    return w_ref[0:1, :] * prev + w_ref[1:2, :] * a_ext[lo:hi] + w_ref[2:3, :] * nxt


N_INPROJ_PARAMS = 9
N_INPROJ_TOKEN_INPUTS = 5


def _rotate_cols(w):
    a, b, c, d = jnp.split(w, 4, axis=-1)
    return jnp.concatenate([-b, a, -d, c], axis=-1)


def _relayout_weights(win_raw, wuq_raw, wukv_raw, win_ref, wuq_ref, wukv_ref):
    d = win_raw.shape[0]
    rc = 256
    for r0 in range(0, d, rc):
        w = win_raw[r0:r0 + rc, :]
        w_kr = w[:, Q_LORA + KV_LORA:IN_MLA]
        w_dt = w[:, IN_MLA + IN_SC + SSD_WIDTH + SSD_XBC:]
        kr_block = jnp.concatenate([
            _rotate_cols(w_kr), w_dt, jnp.zeros((rc, LANES // 2 - QK_ROPE - w_dt.shape[1]), F32),
            w_kr, jnp.zeros((rc, LANES // 2 - QK_ROPE), F32)], axis=1)
        win_ref[r0:r0 + rc, C_Q0:C_KR0] = w[:, :Q_LORA + KV_LORA].astype(BF16)
        win_ref[r0:r0 + rc, C_KR0:C_SC0] = kr_block.astype(BF16)
        win_ref[r0:r0 + rc, C_SC0:D_IN_PAD] = w[:, IN_MLA:IN_MLA + IN_SC + SSD_WIDTH + SSD_XBC].astype(BF16)

    dqk = QK_NOPE + QK_ROPE
    wq = wuq_raw[...]
    nq = wq.shape[0]
    plain, rot = [], []
    for h in range(MLA_HEADS):
        wh = wq[:, h * dqk:(h + 1) * dqk]
        pad = jnp.zeros((nq, HEAD_PAD - dqk), F32)
        plain += [wh, pad]
        rot += [jnp.zeros((nq, QK_NOPE), F32), _rotate_cols(wh[:, QK_NOPE:]), pad]
    wuq_ref[...] = jnp.concatenate(plain + rot, axis=1).astype(BF16)

    dkv = QK_NOPE + V_HEAD
    wkv = wukv_raw[...]
    nkv = wkv.shape[0]
    ks, vs = [], []
    for h in range(MLA_HEADS):
        wh = wkv[:, h * dkv:(h + 1) * dkv]
        ks += [wh[:, :QK_NOPE], jnp.zeros((nkv, HEAD_PAD - QK_NOPE), F32)]
        vs.append(wh[:, QK_NOPE:])
    wukv_ref[...] = jnp.concatenate(ks + vs, axis=1).astype(BF16)


def _inproj_kernel(*refs, layout, split):
    n_x = 2 if split else 1
    per_half = n_x + N_INPROJ_TOKEN_INPUTS
    n_in = TILES_PER_STEP * per_half
    gpre, win_raw, qg, wuq_raw, kvg, wukv_raw, scw, cw, cb = [r.at[0] for r in refs[n_in:n_in + N_INPROJ_PARAMS]]
    n_out = len(refs) - n_in - N_INPROJ_PARAMS - 3
    outs = refs[n_in + N_INPROJ_PARAMS:n_in + N_INPROJ_PARAMS + n_out]
    win_ref, wuq_ref, wukv_ref = refs[n_in + N_INPROJ_PARAMS + n_out:]

    @pl.when(pl.program_id(0) == 0)
    def _():
        _relayout_weights(win_raw, wuq_raw, wukv_raw, win_ref, wuq_ref, wukv_ref)

    params = (gpre, win_ref, qg, wuq_ref, kvg, wukv_ref, scw, cw, cb)
    for h in range(TILES_PER_STEP):
        half = refs[h * per_half:(h + 1) * per_half]
        u = pl.program_id(0) * TILES_PER_STEP + h
        x = layout.select_source(u, half[:n_x])
        xp_ref, xn_ref, mod_ref, cos_ref, sin_ref = half[n_x:]
        _inproj_tile(x, xp_ref, xn_ref, mod_ref.at[0], cos_ref, sin_ref, params, outs, h, layout.seq_edges(u))


def _inproj_tile(x, xp_ref, xn_ref, mod_ref, cos_ref, sin_ref, params, outs, slot, seq_edges):
    gpre_ref, win_ref, qg_ref, wuq_ref, kvg_ref, wukv_ref, scw_ref, cw_ref, cb_ref = params
    q_ref, k_ref, v_ref, kn_ref, ysc_ref, z_ref, act_ref, dt_ref = outs
    tm = x.shape[0]
    rows = slice(slot * tm, (slot + 1) * tm)
    first_of_seq, last_of_seq = seq_edges
    shift = mod_ref[0, 0:1, :]
    scale = mod_ref[0, 1:2, :]

    def norm_mod(xx):
        return _rms(xx, gpre_ref[...]) * (1.0 + scale) + shift

    h_prev = jnp.where(first_of_seq, 0.0, norm_mod(xp_ref[...]))
    h_next = jnp.where(last_of_seq, 0.0, norm_mod(xn_ref[...]))
    h_ext = jnp.concatenate([h_prev, norm_mod(x), h_next], axis=0).astype(BF16)
    u_ext = _dot(h_ext, win_ref[...])
    u = u_ext[SUBLANES:SUBLANES + tm]

    z_ref[rows, :] = u[:, C_Z0:C_XBC0].astype(z_ref.dtype)
    lane = lax.broadcasted_iota(jnp.int32, (1, LANES), 1)
    krb = u[:, C_KR0:C_SC0]
    dt_ref[rows, :] = jnp.where((lane >= DT_LANE0) & (lane < DT_LANE0 + 2 * SSD_HEADS), krb, 0.0)

    conv = _conv3_rows(u_ext[:, C_XBC0:D_IN_PAD], cw_ref, tm) + cb_ref[...]
    act_ref[rows, :] = _silu(conv).astype(act_ref.dtype)

    w = SC_WIDTH
    prod = u_ext[:, C_SC0 + w:C_SC0 + 2 * w] * u_ext[:, C_SC0 + 2 * w:C_Z0]
    ysc_ref[rows, :] = (u[:, C_SC0:C_SC0 + w] * _conv3_rows(prod, scw_ref, tm)).astype(ysc_ref.dtype)

    cos = cos_ref[...]
    sin = sin_ref[...]
    nh = MLA_HEADS
    cos_h = jnp.concatenate([cos] * nh, axis=1)
    sin_h = jnp.concatenate([sin] * nh, axis=1)

    cq = _rms(u[:, C_Q0:C_KV0], qg_ref[...]).astype(BF16)
    q2 = _dot(cq, wuq_ref[...])
    qw = nh * HEAD_PAD
    q = (q2[:, :qw] * cos_h + q2[:, qw:] * sin_h) * Q_PRESCALE
    q_ref[rows, :] = q.astype(BF16)

    ckv = _rms(u[:, C_KV0:C_KR0], kvg_ref[...]).astype(BF16)
    kv = _dot(ckv, wukv_ref[...])
    rope_lane = (lane >= ROPE_LANE0) & (lane < ROPE_LANE0 + QK_ROPE)
    cos_k = jnp.where(rope_lane, cos, 0.0)
    kr = krb * cos_k + pltpu.roll(krb, LANES // 2, 1) * sin
    k = kv[:, :qw] + jnp.concatenate([kr] * nh, axis=1)
    k_b = k.astype(BF16)
    k_ref[rows, :] = k_b
    k_f = k_b.astype(F32)
    norms = []
    for h in range(nh):
        k_h = k_f[:, h * HEAD_PAD:(h + 1) * HEAD_PAD]
        n2 = jnp.max(jnp.sum(k_h * k_h, axis=1, keepdims=True), axis=0, keepdims=True)
        norms.append(jnp.broadcast_to(n2, (1, LANES)))
    kn_ref[slot] = jnp.concatenate(norms + [jnp.zeros((SUBLANES - nh, LANES), F32)], axis=0)
    v_ref[rows, :] = kv[:, qw:].astype(BF16)


def _layer_spec(a, layer, single_buffer=False):
    block = (1,) + a.shape[1:]
    index_map = lambda j: (layer,) + (0,) * (a.ndim - 1)
    if single_buffer:
        return pl.BlockSpec(block, index_map, pipeline_mode=pl.Buffered(1))
    return pl.BlockSpec(block, index_map)


def _inproj(src, mod, layer, gpre, win, qg, wuq, kvg, wukv, cos_t, sin_t, scw, cw, cb, layout):
    tm = TOKEN_TILE
    d = src[0].shape[-1]
    split = len(src) == 2
    qw = MLA_HEADS * HEAD_PAD
    nt = layout.n_tiles
    assert nt % TILES_PER_STEP == 0
    rows = nt * tm

    in_specs, operands = [], []
    for h in range(TILES_PER_STEP):
        def u_of(j, h=h):
            return j * TILES_PER_STEP + h

        def rope(j, u_of=u_of):
            return (layout.coords(u_of(j))[1], 0)

        in_specs += layout.source_specs(u_of, d, split) + layout.halo_specs(u_of, d, split)
        in_specs += [
            pl.BlockSpec((1, 1, N_MOD, d), lambda j, u_of=u_of: (layer, layout.mod_row(u_of(j)), 0, 0)),
            pl.BlockSpec((tm, LANES), rope), pl.BlockSpec((tm, LANES), rope),
        ]
        operands += list(src) + [src[-1], src[-1], mod, cos_t, sin_t]
    consts = [gpre, win, qg, wuq, kvg, wukv, scw, cw, cb]
    in_specs += [_layer_spec(c, layer, single_buffer=c is win) for c in consts]

    def tok(width):
        return pl.BlockSpec((TILES_PER_STEP * tm, width), lambda j: (j, 0))

    out_shape = (
        jax.ShapeDtypeStruct((rows, qw), BF16),
        jax.ShapeDtypeStruct((rows, qw), BF16),
        jax.ShapeDtypeStruct((rows, MLA_WIDTH), BF16),
        jax.ShapeDtypeStruct((nt, SUBLANES, LANES), F32),
        jax.ShapeDtypeStruct((rows, SC_WIDTH), BF16),
        jax.ShapeDtypeStruct((rows, SSD_WIDTH), BF16),
        jax.ShapeDtypeStruct((rows, SSD_XBC), BF16),
        jax.ShapeDtypeStruct((rows, LANES), F32),
    )
    return pl.pallas_call(
        functools.partial(_inproj_kernel, layout=layout, split=split),
        out_shape=out_shape,
        grid=(nt // TILES_PER_STEP,),
        in_specs=in_specs,
        out_specs=(tok(qw), tok(qw), tok(MLA_WIDTH),
                   pl.BlockSpec((TILES_PER_STEP, SUBLANES, LANES), lambda j: (j, 0, 0)),
                   tok(SC_WIDTH), tok(SSD_WIDTH), tok(SSD_XBC), tok(LANES)),
        scratch_shapes=[pltpu.VMEM((d, D_IN_PAD), BF16),
                        pltpu.VMEM((Q_LORA, 2 * qw), BF16),
                        pltpu.VMEM((KV_LORA, qw + MLA_WIDTH), BF16)],
        compiler_params=pltpu.CompilerParams(
            dimension_semantics=("arbitrary",), vmem_limit_bytes=VMEM_LIMIT_BYTES),
        name="inproj",
    )(*operands, *consts)


def _attn_kernel(q_ref, k_ref, v_ref, kn_ref, o_ref, vt_ref, s_ref, p_ref, *, n_keys, n_ctx, ctx_queries):
    i = pl.program_id(1)
    tq = q_ref.shape[1]

    @pl.when(i == 0)
    def _():
        for c in range(n_keys // tq):
            v_t = v_ref[0, c * tq:(c + 1) * tq, :].astype(F32).T.astype(BF16)
            for h in range(MLA_HEADS):
                vt_ref[h, 0:V_HEAD, c * tq:(c + 1) * tq] = v_t[h * V_HEAD:(h + 1) * V_HEAD]
        for h in range(MLA_HEADS):
            vt_ref[h, V_HEAD:V_ROWS, :] = jnp.ones((V_ROWS - V_HEAD, n_keys), BF16)

    q_t = q_ref[0].astype(F32).T.astype(BF16)
    kn = jnp.max(kn_ref[0], axis=0)

    def finish(outs):
        o_ref[0] = jnp.concatenate(outs, axis=0).T.astype(o_ref.dtype)

    def head_out(ov):
        return ov[0:V_HEAD] / ov[V_HEAD:V_HEAD + 1]

    def attend_two_pass(nk):
        def scores(h):
            s_ref[h, 0:nk, :] = _dot(k_ref[0, 0:nk, h * HEAD_PAD:(h + 1) * HEAD_PAD],
                                     q_t[h * HEAD_PAD:(h + 1) * HEAD_PAD, :])

        def probs(h):
            s = s_ref[h, 0:nk, :]
            p_ref[h, 0:nk, :] = jnp.exp2(s - jnp.max(s, axis=0, keepdims=True)).astype(BF16)

        def values(h):
            return head_out(_dot(vt_ref[h, :, 0:nk], p_ref[h, 0:nk, :]))

        scores(0), scores(1)
        scores(2), scores(3), probs(0), probs(1)
        outs = [values(0), values(1)]
        probs(2), probs(3)
        outs += [values(2), values(3)]
        finish(outs)

    def attend_one_pass(nk, shift):
        outs = []
        for pair in range(MLA_HEADS // 2):
            heads = (2 * pair, 2 * pair + 1)
            s = [_dot(k_ref[0, 0:nk, h * HEAD_PAD:(h + 1) * HEAD_PAD],
                      q_t[h * HEAD_PAD:(h + 1) * HEAD_PAD, :]) for h in heads]
            p = [jnp.exp2(sh - shift[h]).astype(BF16) for h, sh in zip(heads, s)]
            outs += [head_out(_dot(vt_ref[h, :, 0:nk], ph)) for h, ph in zip(heads, p)]
        finish(outs)

    def attend(nk):
        bounds = []
        for h in range(MLA_HEADS):
            qf = q_t[h * HEAD_PAD:(h + 1) * HEAD_PAD, :].astype(F32)
            q2 = jnp.sum(qf * qf, axis=0, keepdims=True)
            bounds.append(jnp.sqrt(q2 * kn[h:h + 1, 0:1]) * BOUND_SLACK)
        worst = jnp.max(jnp.concatenate(bounds, axis=0))
        one_pass = worst <= SHIFT_MAX_BOUND

        @pl.when(one_pass)
        def _():
            attend_one_pass(nk, [b - SHIFT_HEADROOM for b in bounds])

        @pl.when(jnp.logical_not(one_pass))
        def _():
            attend_two_pass(nk)

    if ctx_queries:
        @pl.when(i == 0)
        def _():
            attend_two_pass(n_ctx)

        @pl.when(i > 0)
        def _():
            attend(n_keys)
    else:
        attend(n_keys)


def _attention(q, k, v, kn, n_ctx, ctx_queries):
    bsz, t, qw = k.shape
    tq = TOKEN_TILE
    t0 = 0 if ctx_queries else n_ctx // tq
    nq = t // tq - t0
    return pl.pallas_call(
        functools.partial(_attn_kernel, n_keys=t, n_ctx=n_ctx, ctx_queries=ctx_queries),
        out_shape=jax.ShapeDtypeStruct((bsz, nq * tq, MLA_WIDTH), BF16),
        grid=(bsz, nq),
        in_specs=[
            pl.BlockSpec((1, tq, qw), lambda b, i: (b, i + t0, 0)),
            pl.BlockSpec((1, t, qw), lambda b, i: (b, 0, 0)),
            pl.BlockSpec((1, t, MLA_WIDTH), lambda b, i: (b, 0, 0)),
            pl.BlockSpec((1,) + kn.shape[1:], lambda b, i: (b, 0, 0, 0)),
        ],
        out_specs=pl.BlockSpec((1, tq, MLA_WIDTH), lambda b, i: (b, i, 0)),
        scratch_shapes=[pltpu.VMEM((MLA_HEADS, V_ROWS, t), BF16),
                        pltpu.VMEM((MLA_HEADS, t, tq), F32), pltpu.VMEM((MLA_HEADS, t, tq), BF16)],
        compiler_params=pltpu.CompilerParams(
            dimension_semantics=("arbitrary", "arbitrary"), vmem_limit_bytes=VMEM_LIMIT_BYTES),
        name="attention",
    )(q, k, v, kn)


def _split3(x):
    hi = x.astype(BF16)
    r = x - hi.astype(F32)
    mid = r.astype(BF16)
    lo = (r - mid.astype(F32)).astype(BF16)
    return hi, mid, lo


def _ssd_role(act, dt_raw, direction, alog_ref, dtbias_ref, dskip_ref, y_ref, row0, h_ref):
    tc = SSD_CHUNK
    n = SSD_STATE
    xs = act[:, :SSD_WIDTH]
    bm = act[:, SSD_WIDTH:SSD_WIDTH + SSD_GN].astype(F32)
    cm = act[:, SSD_WIDTH + SSD_GN:]
    bm_t = bm.T

    a = -jnp.exp(alog_ref[...])
    xb = dt_raw + dtbias_ref[...]
    dt = jnp.maximum(xb, 0.0) + jnp.log(1.0 + jnp.exp(-jnp.abs(xb)))
    la = dt * a

    r_i = lax.broadcasted_iota(jnp.int32, (tc, tc), 0)
    c_i = lax.broadcasted_iota(jnp.int32, (tc, tc), 1)
    tri = (c_i <= r_i) if direction == 0 else (c_i >= r_i)
    tri_b = jnp.where(tri, 1.0, 0.0).astype(BF16)
    hi, mid, lo = _split3(la)
    cs = _dot(tri_b, hi) + _dot(tri_b, mid) + _dot(tri_b, lo)
    end = tc - 1 if direction == 0 else 0
    nd = 2 * SSD_HEADS
    cs_t = cs.T[DT_LANE0:DT_LANE0 + nd]
    dt_t = dt.T[DT_LANE0:DT_LANE0 + nd]
    w_t = jnp.exp(cs_t[:, end:end + 1] - cs_t) * dt_t
    dec_tot = jnp.exp(cs[end:end + 1, :])
    cm_f = cm.astype(F32)
    lane = lax.broadcasted_iota(jnp.int32, (1, LANES), 1)
    low_half = lane < SSD_HEAD_DIM
    zeros_h = jnp.zeros((n, LANES), BF16)
    zeros_lhs = jnp.zeros((n, tc), F32)

    for g in range(SSD_GROUPS):
        cb = _dot(cm[:, g * n:(g + 1) * n], bm_t[g * n:(g + 1) * n, :].astype(BF16))
        bt_g = bm_t[g * n:(g + 1) * n, :]
        for pr in range(HEADS_PER_GROUP // 2):
            slab = g * (HEADS_PER_GROUP // 2) + pr
            h_pair = h_ref[direction, g, :, pr * LANES:(pr + 1) * LANES]
            h_b = h_pair.astype(BF16)
            xs_pair = xs[:, slab * LANES:(slab + 1) * LANES]
            rhs = jnp.concatenate([xs_pair] + [h_b if gg == g else zeros_h for gg in range(SSD_GROUPS)], axis=0)
            lhs_rows = []
            cols = []
            for hh in range(2):
                r = direction * SSD_HEADS + 2 * slab + hh
                col = DT_LANE0 + r
                cols.append(col)
                cs_col = jnp.broadcast_to(cs[:, col:col + 1], (tc, tc))
                lm = jnp.exp(jnp.where(tri, cs_col - cs_t[r:r + 1, :], -jnp.inf))
                top = jnp.concatenate([cb * dt_t[r:r + 1, :] * lm, cm_f * jnp.exp(cs_col)], axis=1)
                bot = jnp.concatenate([bt_g * w_t[r:r + 1, :], zeros_lhs], axis=1)
                lhs_rows += [top, bot]
            lhs = jnp.concatenate(lhs_rows, axis=0).astype(BF16)
            out = _dot(lhs, rhs)
            m = tc + n
            y_pair = jnp.where(low_half, out[0:tc], out[m:m + tc])
            if direction == 0:
                y_pair = y_pair + dskip_ref[:, slab * LANES:(slab + 1) * LANES] * xs_pair.astype(F32)
            y_ref[0, row0:row0 + tc, slab * LANES:(slab + 1) * LANES] = y_pair
            h_new = jnp.where(low_half, out[tc:m], out[m + tc:2 * m])
            keep = jnp.where(low_half, dec_tot[:, cols[0]:cols[0] + 1], dec_tot[:, cols[1]:cols[1] + 1])
            h_ref[direction, g, :, pr * LANES:(pr + 1) * LANES] = h_pair * keep + h_new


def _bwd_block(i, n_blocks, ctx_blocks):
    return jnp.where(i < ctx_blocks, ctx_blocks - 1 - i, n_blocks - 1 + ctx_blocks - i)


def _ssd_kernel(af_ref, ab_ref, dtf_ref, dtb_ref, alog_ref, dtbias_ref, dskip_ref,
                yf_ref, yb_ref, h_ref):
    @pl.when(pl.program_id(1) == 0)
    def _():
        h_ref[...] = jnp.zeros_like(h_ref)

    params = (alog_ref.at[0], dtbias_ref.at[0], dskip_ref.at[0])
    tc = SSD_CHUNK
    n_sub = af_ref.shape[1] // tc
    for j in range(n_sub):
        r0 = j * tc
        _ssd_role(af_ref[0, r0:r0 + tc, :], dtf_ref[0, r0:r0 + tc, :], 0, *params, yf_ref, r0, h_ref)
        r1 = (n_sub - 1 - j) * tc
        _ssd_role(ab_ref[0, r1:r1 + tc, :], dtb_ref[0, r1:r1 + tc, :], 1, *params, yb_ref, r1, h_ref)


def _ssd_scan(act, dt, layer, alog, dtbias, dskip, n_ctx):
    bsz, t, _ = act.shape
    tc = SSD_BLOCK
    n_blocks = t // tc
    ctx_blocks = n_ctx // tc
    assert n_ctx % tc == 0 and t % tc == 0

    def fwd(i):
        return i

    def bwd(i):
        return _bwd_block(i, n_blocks, ctx_blocks)

    def main(order, width):
        return pl.BlockSpec((1, tc, width), lambda b, i: (b, order(i), 0))

    def const(shape):
        return pl.BlockSpec((1,) + shape[1:], lambda b, i: (layer,) + (0,) * (len(shape) - 1))

    y_shape = jax.ShapeDtypeStruct((bsz, t, SSD_WIDTH), F32)
    return pl.pallas_call(
        _ssd_kernel,
        out_shape=(y_shape, y_shape),
        grid=(bsz, n_blocks),
        in_specs=[
            main(fwd, SSD_XBC), main(bwd, SSD_XBC), main(fwd, LANES), main(bwd, LANES),
            const(alog.shape), const(dtbias.shape), const(dskip.shape),
        ],
        out_specs=(main(fwd, SSD_WIDTH), main(bwd, SSD_WIDTH)),
        scratch_shapes=[pltpu.VMEM((2, SSD_GROUPS, SSD_STATE, HEADS_PER_GROUP * SSD_HEAD_DIM), F32)],
        compiler_params=pltpu.CompilerParams(
            dimension_semantics=("arbitrary", "arbitrary"), vmem_limit_bytes=VMEM_LIMIT_BYTES),
        name="ssd_scan",
    )(act, act, dt, dt, alog, dtbias, dskip)


def _mix_ffn_tile(x, mod_ref, att_ref, ysc_ref, z_ref, yf_ref, yb_ref, params, o_ref, row0, ycat_ref, acc_ref):
    normg_ref, wout_ref, gpost_ref, gpre2_ref, w1_ref, w2_ref, gpost2_ref = params
    tm = x.shape[0]
    gated = (yf_ref[...] + yb_ref[...]) * _silu(z_ref[...].astype(F32))
    gw = SSD_WIDTH // SSD_GROUPS
    w = SC_WIDTH
    ycat_ref[:, 0:MLA_WIDTH] = att_ref[...]
    ycat_ref[:, MLA_WIDTH:MLA_WIDTH + w] = ysc_ref[...]
    c0 = MLA_WIDTH + w
    for g in range(SSD_GROUPS):
        gg = gated[:, g * gw:(g + 1) * gw]
        gg = gg * lax.rsqrt(jnp.mean(gg * gg, axis=-1, keepdims=True) + EPS) * normg_ref[:, g * gw:(g + 1) * gw]
        ycat_ref[:, c0 + g * gw:c0 + (g + 1) * gw] = gg.astype(BF16)

    gate1 = mod_ref[0, 2:3, :]
    shift2 = mod_ref[0, 3:4, :]
    scale2 = mod_ref[0, 4:5, :]
    gate2 = mod_ref[0, 5:6, :]
    y = _dot(ycat_ref[...], wout_ref[...])
    x1 = x + gate1 * _rms(y, gpost_ref[...])

    h2 = (_rms(x1, gpre2_ref[...]) * (1.0 + scale2) + shift2).astype(BF16)
    d_ff = w1_ref.shape[1]
    for c in range(d_ff // FF_CHUNK):
        a = _dot(h2, w1_ref[:, c * FF_CHUNK:(c + 1) * FF_CHUNK])
        r = jnp.square(jnp.maximum(a, 0.0)).astype(BF16)
        part = _dot(r, w2_ref[c * FF_CHUNK:(c + 1) * FF_CHUNK, :])
        if c == 0:
            acc_ref[...] = part
        else:
            acc_ref[...] += part
    o_ref[row0:row0 + tm, :] = x1 + gate2 * _rms(acc_ref[...], gpost2_ref[...])


N_MIX_PARAMS = 7
N_MIX_TOKEN_INPUTS = 6


def _mix_ffn_kernel(*refs, layout, latent_only, split):
    n_x = 2 if split else 1
    per_half = n_x + N_MIX_TOKEN_INPUTS
    n_in = TILES_PER_STEP * per_half
    params = [r.at[0] for r in refs[n_in:n_in + N_MIX_PARAMS]]
    o_ref = refs[n_in + N_MIX_PARAMS]
    ycat_ref, acc_ref = refs[n_in + N_MIX_PARAMS + 1:]
    tm = TOKEN_TILE
    for h in range(TILES_PER_STEP):
        half = refs[h * per_half:(h + 1) * per_half]
        u = layout.unified_tile(pl.program_id(0) * TILES_PER_STEP + h, latent_only)
        x = layout.select_source(u, half[:n_x])
        mod_ref, *token_refs = half[n_x:]
        _mix_ffn_tile(x, mod_ref.at[0], *token_refs, params, o_ref, h * tm, ycat_ref.at[h], acc_ref.at[h])


def _mix_ffn(src, mod, layer, att, ysc, z, yf, yb, normg, wout, gpost, gpre2, w1, w2, gpost2, layout,
             latent_only):
    tm = TOKEN_TILE
    d = src[0].shape[-1]
    n_out_tiles = layout.bsz * (layout.lat_tiles if latent_only else layout.tiles_per_batch)
    assert n_out_tiles % TILES_PER_STEP == 0
    att_unified = att.shape[0] == layout.bsz * layout.tiles_per_batch * tm

    in_specs, operands = [], []
    for h in range(TILES_PER_STEP):
        def u_of(j, h=h):
            return layout.unified_tile(j * TILES_PER_STEP + h, latent_only)

        def tok(width, tile_of=u_of):
            return pl.BlockSpec((tm, width), lambda j: (tile_of(j), 0))

        in_specs += layout.source_specs(u_of, d, len(src) == 2)
        in_specs += [
            pl.BlockSpec((1, 1, N_MOD, d), lambda j, u_of=u_of: (layer, layout.mod_row(u_of(j)), 0, 0)),
            tok(MLA_WIDTH) if att_unified else tok(MLA_WIDTH, lambda j, h=h: j * TILES_PER_STEP + h),
            tok(SC_WIDTH), tok(SSD_WIDTH), tok(SSD_WIDTH), tok(SSD_WIDTH),
        ]
        operands += list(src) + [mod, att, ysc, z, yf, yb]
    consts = [normg, wout, gpost, gpre2, w1, w2, gpost2]
    in_specs += [_layer_spec(c, layer, single_buffer=True) for c in consts]
    return pl.pallas_call(
        functools.partial(_mix_ffn_kernel, layout=layout, latent_only=latent_only, split=len(src) == 2),
        out_shape=jax.ShapeDtypeStruct((n_out_tiles * tm, d), F32),
        grid=(n_out_tiles // TILES_PER_STEP,),
        in_specs=in_specs,
        out_specs=pl.BlockSpec((TILES_PER_STEP * tm, d), lambda j: (j, 0)),
        scratch_shapes=[pltpu.VMEM((TILES_PER_STEP, tm, d), BF16), pltpu.VMEM((TILES_PER_STEP, tm, d), F32)],
        compiler_params=pltpu.CompilerParams(
            dimension_semantics=("parallel",), vmem_limit_bytes=VMEM_LIMIT_BYTES),
        name="mix_ffn",
    )(*operands, *consts)


def _rope_tables(n_ctx, seq):
    f32 = np.float32
    half = QK_ROPE // 2
    inv_freq = (f32(ROPE_THETA) ** (-np.arange(0, half, 2, dtype=f32) / f32(half))).astype(f32)
    rows = seq // GRID_W
    row = np.repeat(np.arange(rows, dtype=f32), GRID_W)
    col = np.tile(np.arange(GRID_W, dtype=f32), rows)
    ang_r = row[:, None] * inv_freq
    ang_c = col[:, None] * inv_freq
    ang = np.concatenate([ang_r, ang_r, ang_c, ang_c], axis=-1).astype(f32)
    t = n_ctx + seq
    cos_t = np.ones((t, LANES), f32)
    sin_t = np.zeros((t, LANES), f32)
    cos_t[n_ctx:, ROPE_LANE0:ROPE_LANE0 + QK_ROPE] = np.cos(ang)
    sin_t[n_ctx:, ROPE_LANE0:ROPE_LANE0 + QK_ROPE] = np.sin(ang)
    return jnp.asarray(cos_t), jnp.asarray(sin_t)


def kernel(x, c, ctx, c_ctx, w_mod, b_mod, g_pre_mix, w_in, mla_q_norm, w_uq, mla_kv_norm, w_ukv, sc_conv_w, ssd_conv_w, ssd_conv_b, ssd_a_log, ssd_dt_bias, ssd_d, ssd_norm, w_out, g_post_mix, g_pre_ffn, w_ff1, w_ff2, g_post_ffn):
    bsz, seq, d = x.shape
    n_ctx = ctx.shape[1]
    depth = w_mod.shape[0]
    assert n_ctx == TOKEN_TILE and seq % TOKEN_TILE == 0 and seq % GRID_W == 0
    layout = _TokenLayout(bsz, n_ctx, seq)
    t = n_ctx + seq

    cvec = jnp.zeros((SUBLANES, d), F32).at[:bsz].set(c).at[layout.ctx_row].set(c_ctx)
    mod_all = _modulation(cvec, w_mod, b_mod).reshape(depth, SUBLANES, N_MOD, d)
    cos_t, sin_t = _rope_tables(n_ctx, seq)

    src = (ctx.reshape(bsz * n_ctx, d), x.reshape(bsz * seq, d))

    def rows3(a):
        return a.reshape(depth, 1, -1)

    def dt_lanes(a):
        flat = a.reshape(depth, 1, -1)
        return jnp.pad(flat, ((0, 0), (0, 0), (DT_LANE0, LANES - DT_LANE0 - flat.shape[-1])))

    alog, dtbias = dt_lanes(ssd_a_log), dt_lanes(ssd_dt_bias)
    dskip = rows3(jnp.repeat(ssd_d, SSD_HEAD_DIM, axis=-1))
    w_out_b, w_ff1_b, w_ff2_b = w_out.astype(BF16), w_ff1.astype(BF16), w_ff2.astype(BF16)

    def per_batch(a):
        return a.reshape((bsz, t) + a.shape[1:])

    for i in range(depth):
        last = i == depth - 1
        q, k, v, kn, ysc, z, act, dt = _inproj(
            src, mod_all, i, rows3(g_pre_mix), w_in, rows3(mla_q_norm), w_uq, rows3(mla_kv_norm), w_ukv,
            cos_t, sin_t, sc_conv_w, ssd_conv_w, rows3(ssd_conv_b), layout)
        att = _attention(per_batch(q), per_batch(k), per_batch(v),
                         kn.reshape(bsz, layout.tiles_per_batch, SUBLANES, LANES), n_ctx, ctx_queries=not last)
        yf, yb = _ssd_scan(per_batch(act), per_batch(dt), i, alog, dtbias, dskip, n_ctx)
        out = _mix_ffn(src, mod_all, i, att.reshape(-1, MLA_WIDTH), ysc, z,
                       yf.reshape(-1, SSD_WIDTH), yb.reshape(-1, SSD_WIDTH), rows3(ssd_norm),
                       w_out_b, rows3(g_post_mix), rows3(g_pre_ffn), w_ff1_b, w_ff2_b, rows3(g_post_ffn),
                       layout, latent_only=last)
        src = (out,)
    return out.reshape(bsz, seq, d)
```

```python
import functools
import math

import jax
import jax.numpy as jnp
import numpy as np
from jax import lax
from jax.experimental import pallas as pl
from jax.experimental.pallas import tpu as pltpu

F32 = jnp.float32
BF16 = jnp.bfloat16

GRID_W = 64
EPS = 1e-6
N_MOD = 6
MLA_HEADS = 4
Q_LORA = 256
KV_LORA = 128
QK_NOPE = 64
QK_ROPE = 32
V_HEAD = 64
MLA_WIDTH = MLA_HEADS * V_HEAD
MLA_SCALE = (QK_NOPE + QK_ROPE) ** -0.5
ROPE_THETA = 10000.0
SC_WIDTH = 256
SSD_HEADS = 8
SSD_HEAD_DIM = 64
SSD_WIDTH = SSD_HEADS * SSD_HEAD_DIM
SSD_GROUPS = 2
SSD_STATE = 64
SSD_CHUNK = 128
SSD_GN = SSD_GROUPS * SSD_STATE
SSD_XBC = SSD_WIDTH + 2 * SSD_GN
HEADS_PER_GROUP = SSD_HEADS // SSD_GROUPS
IN_MLA = Q_LORA + KV_LORA + QK_ROPE
IN_SC = 3 * SC_WIDTH

LANES = 128
SUBLANES = 8
VMEM_LIMIT_BYTES = 56 * 1024 * 1024

HEAD_PAD = LANES
V_ROWS = V_HEAD + 16
ROPE_LANE0 = QK_NOPE
TOKEN_TILE = 256
TILES_PER_STEP = 2
SSD_BLOCK = 2 * SSD_CHUNK
FF_CHUNK = 1024

C_Q0 = 0
C_KV0 = C_Q0 + Q_LORA
C_KR0 = C_KV0 + KV_LORA
C_SC0 = C_KR0 + LANES
C_Z0 = C_SC0 + IN_SC
C_XBC0 = C_Z0 + SSD_WIDTH
D_IN_PAD = C_XBC0 + SSD_XBC
DT_LANE0 = QK_ROPE

Q_PRESCALE = MLA_SCALE * math.log2(math.e)

SHIFT_HEADROOM = 60.0
SHIFT_MAX_BOUND = 90.0
BOUND_SLACK = 1.0 + 2.0 ** -8


def _rms(x, g):
    return x * lax.rsqrt(jnp.mean(x * x, axis=-1, keepdims=True) + EPS) * g


def _silu(x):
    return x / (1.0 + jnp.exp(-x))


def _dot(a, b):
    return jnp.dot(a, b, preferred_element_type=F32)


class _TokenLayout:
    def __init__(self, bsz, n_ctx, seq):
        tm = TOKEN_TILE
        assert n_ctx % tm == 0 and seq % tm == 0
        self.bsz = bsz
        self.ctx_tiles = n_ctx // tm
        self.lat_tiles = seq // tm
        self.tiles_per_batch = self.ctx_tiles + self.lat_tiles
        self.n_tiles = bsz * self.tiles_per_batch
        self.ctx_row = bsz

    def unified_tile(self, o, latent_only):
        if not latent_only:
            return o
        return o + (o // self.lat_tiles + 1) * self.ctx_tiles

    def coords(self, u):
        b = u // self.tiles_per_batch
        return b, u - b * self.tiles_per_batch

    def mod_row(self, u):
        b, i = self.coords(u)
        return jnp.where(i < self.ctx_tiles, self.ctx_row, b)

    def seq_edges(self, u):
        _, i = self.coords(u)
        first = (i == 0) | (i == self.ctx_tiles)
        last = (i == self.ctx_tiles - 1) | (i == self.tiles_per_batch - 1)
        return first, last

    def _latent_tile(self, u):
        b, i = self.coords(u)
        return b * self.lat_tiles + jnp.maximum(i - self.ctx_tiles, 0)

    def _context_tile(self, u):
        b, i = self.coords(u)
        return b * self.ctx_tiles + jnp.minimum(i, self.ctx_tiles - 1)

    def source_specs(self, u_of, d, split):
        tm = TOKEN_TILE
        if not split:
            return [pl.BlockSpec((tm, d), lambda j: (u_of(j), 0))]
        return [pl.BlockSpec((tm, d), lambda j: (self._context_tile(u_of(j)), 0)),
                pl.BlockSpec((tm, d), lambda j: (self._latent_tile(u_of(j)), 0))]

    def halo_specs(self, u_of, d, split):
        rb = TOKEN_TILE // SUBLANES
        if split:
            assert self.ctx_tiles == 1
            tile_of, n_rb = (lambda j: self._latent_tile(u_of(j))), self.bsz * self.lat_tiles * rb
        else:
            tile_of, n_rb = u_of, self.n_tiles * rb
        return [pl.BlockSpec((SUBLANES, d), lambda j: (jnp.maximum(tile_of(j) * rb - 1, 0), 0)),
                pl.BlockSpec((SUBLANES, d), lambda j: (jnp.minimum((tile_of(j) + 1) * rb, n_rb - 1), 0))]

    def select_source(self, u, refs):
        if len(refs) == 1:
            return refs[0][...]
        _, i = self.coords(u)
        return jnp.where(i < self.ctx_tiles, refs[0][...], refs[1][...])


def _mod_kernel(c_ref, w_ref, b_ref, o_ref):
    s = _silu(c_ref[...]).astype(BF16)
    o_ref[0] = _dot(s, w_ref[0].astype(BF16)) + b_ref[0]


def _modulation(cvec, w_mod, b_mod):
    depth, d, nd = w_mod.shape
    rows = cvec.shape[0]
    return pl.pallas_call(
        _mod_kernel,
        out_shape=jax.ShapeDtypeStruct((depth, rows, nd), F32),
        grid=(depth, nd // d),
        in_specs=[
            pl.BlockSpec((rows, d), lambda l, j: (0, 0)),
            pl.BlockSpec((1, d, d), lambda l, j: (l, 0, j)),
            pl.BlockSpec((1, 1, d), lambda l, j: (l, 0, j)),
        ],
        out_specs=pl.BlockSpec((1, rows, d), lambda l, j: (l, 0, j)),
        compiler_params=pltpu.CompilerParams(dimension_semantics=("parallel", "parallel")),
        name="modulation",
    )(cvec, w_mod, b_mod.reshape(depth, 1, nd))


def _conv3_rows(a_ext, w_ref, tm):
    te = a_ext.shape[0]
    lo, hi = SUBLANES, SUBLANES + tm
    prev = pltpu.roll(a_ext, 1, 0)[lo:hi]
    nxt = pltpu.roll(a_ext, te - 1, 0)[lo:hi]
    return w_ref[0:1, :] * prev + w_ref[1:2, :] * a_ext[lo:hi] + w_ref[2:3, :] * nxt


N_INPROJ_PARAMS = 11
N_INPROJ_TOKEN_INPUTS = 5


def _split3(x):
    hi = x.astype(BF16)
    r = x - hi.astype(F32)
    mid = r.astype(BF16)
    lo = (r - mid.astype(F32)).astype(BF16)
    return hi, mid, lo


SIDE_ROWS = 3 * 2 * SSD_HEADS


def _ssd_decay_tables(dt_raw, alog_ref, dtbias_ref, cs_ref, side_ref, slot):
    tc = SSD_CHUNK
    tm = dt_raw.shape[0]
    nd = 2 * SSD_HEADS
    a = -jnp.exp(alog_ref[...])
    xb = dt_raw + dtbias_ref[...]
    dt = jnp.maximum(xb, 0.0) + jnp.log(1.0 + jnp.exp(-jnp.abs(xb)))
    la = dt * a
    r_i = lax.broadcasted_iota(jnp.int32, (tc, tc), 0)
    c_i = lax.broadcasted_iota(jnp.int32, (tc, tc), 1)
    tri_f = jnp.where(c_i <= r_i, 1.0, 0.0).astype(BF16)
    tri_b = jnp.where(c_i >= r_i, 1.0, 0.0).astype(BF16)
    lane = lax.broadcasted_iota(jnp.int32, (1, LANES), 1)
    fwd_lane = lane < DT_LANE0 + SSD_HEADS
    row = lax.broadcasted_iota(jnp.int32, (nd, 1), 0)
    for c in range(tm // tc):
        parts = _split3(la[c * tc:(c + 1) * tc])
        cs_f = sum(_dot(tri_f, p) for p in parts)
        cs_b = sum(_dot(tri_b, p) for p in parts)
        cs = jnp.where(fwd_lane, cs_f, cs_b)
        cs_ref[slot * tm + c * tc:slot * tm + (c + 1) * tc, :] = cs
        cs_t = cs.T[DT_LANE0:DT_LANE0 + nd]
        dt_t = dt[c * tc:(c + 1) * tc].T[DT_LANE0:DT_LANE0 + nd]
        cs_end = jnp.where(row < SSD_HEADS, cs_t[:, tc - 1:tc], cs_t[:, 0:1])
        w_t = jnp.exp(cs_end - cs_t) * dt_t
        side_ref[slot * (tm // tc) + c] = jnp.concatenate([cs_t, dt_t, w_t], axis=0)


def _rotate_cols(w):
    a, b, c, d = jnp.split(w, 4, axis=-1)
    return jnp.concatenate([-b, a, -d, c], axis=-1)


def _relayout_weights(win_raw, wuq_raw, wukv_raw, win_ref, wuq_ref, wukv_ref):
    d = win_raw.shape[1]
    w_kr = win_raw[Q_LORA + KV_LORA:IN_MLA, :]
    w_dt = win_raw[IN_MLA + IN_SC + SSD_WIDTH + SSD_XBC:, :]
    a, b, c, e = (w_kr[i * SUBLANES:(i + 1) * SUBLANES] for i in range(4))
    kr_block = jnp.concatenate([
        -b, a, -e, c,
        w_dt, jnp.zeros((LANES // 2 - QK_ROPE - w_dt.shape[0], d), F32),
        w_kr, jnp.zeros((LANES // 2 - QK_ROPE, d), F32)], axis=0)
    win_ref[:, C_KR0:C_SC0] = kr_block.T.astype(BF16)
    step = 2 * LANES
    for src0, dst0, n_rows in ((0, C_Q0, Q_LORA + KV_LORA), (IN_MLA, C_SC0, IN_SC + SSD_WIDTH + SSD_XBC)):
        for off in range(0, n_rows, step):
            rows = min(step, n_rows - off)
            blk = win_raw[src0 + off:src0 + off + rows, :]
            win_ref[:, dst0 + off:dst0 + off + rows] = blk.T.astype(BF16)

    dqk = QK_NOPE + QK_ROPE
    wq = wuq_raw[...]
    nq = wq.shape[0]
    plain, rot = [], []
    for h in range(MLA_HEADS):
        wh = wq[:, h * dqk:(h + 1) * dqk]
        pad = jnp.zeros((nq, HEAD_PAD - dqk), F32)
        plain += [wh, pad]
        rot += [jnp.zeros((nq, QK_NOPE), F32), _rotate_cols(wh[:, QK_NOPE:]), pad]
    wuq_ref[...] = jnp.concatenate(plain + rot, axis=1).astype(BF16)

    dkv = QK_NOPE + V_HEAD
    wkv = wukv_raw[...]
    nkv = wkv.shape[0]
    ks, vs = [], []
    for h in range(MLA_HEADS):
        wh = wkv[:, h * dkv:(h + 1) * dkv]
        ks += [wh[:, :QK_NOPE], jnp.zeros((nkv, HEAD_PAD - QK_NOPE), F32)]
        vs.append(wh[:, QK_NOPE:])
    wukv_ref[...] = jnp.concatenate(ks + vs, axis=1).astype(BF16)


def _inproj_kernel(*refs, layout, split):
    n_x = 2 if split else 1
    per_half = n_x + N_INPROJ_TOKEN_INPUTS
    n_in = TILES_PER_STEP * per_half
    (gpre, win_raw, qg, wuq_raw, kvg, wukv_raw, scw, cw, cb, alog, dtbias) = [
        r.at[0] for r in refs[n_in:n_in + N_INPROJ_PARAMS]]
    n_out = len(refs) - n_in - N_INPROJ_PARAMS - 3
    outs = refs[n_in + N_INPROJ_PARAMS:n_in + N_INPROJ_PARAMS + n_out]
    win_ref, wuq_ref, wukv_ref = refs[n_in + N_INPROJ_PARAMS + n_out:]

    @pl.when(pl.program_id(0) == 0)
    def _():
        _relayout_weights(win_raw, wuq_raw, wukv_raw, win_ref, wuq_ref, wukv_ref)

    params = (gpre, win_ref, qg, wuq_ref, kvg, wukv_ref, scw, cw, cb, alog, dtbias)
    for h in range(TILES_PER_STEP):
        half = refs[h * per_half:(h + 1) * per_half]
        u = pl.program_id(0) * TILES_PER_STEP + h
        x = layout.select_source(u, half[:n_x])
        xp_ref, xn_ref, mod_ref, cos_ref, sin_ref = half[n_x:]
        _inproj_tile(x, xp_ref, xn_ref, mod_ref.at[0], cos_ref, sin_ref, params, outs, h, layout.seq_edges(u))


def _inproj_tile(x, xp_ref, xn_ref, mod_ref, cos_ref, sin_ref, params, outs, slot, seq_edges):
    gpre_ref, win_ref, qg_ref, wuq_ref, kvg_ref, wukv_ref, scw_ref, cw_ref, cb_ref, alog_ref, dtbias_ref = params
    q_ref, k_ref, v_ref, kn_ref, ysc_ref, z_ref, act_ref, cs_ref, side_ref = outs
    tm = x.shape[0]
    rows = slice(slot * tm, (slot + 1) * tm)
    first_of_seq, last_of_seq = seq_edges
    shift = mod_ref[0, 0:1, :]
    scale = mod_ref[0, 1:2, :]

    def norm_mod(xx):
        return _rms(xx, gpre_ref[...]) * (1.0 + scale) + shift

    h_prev = jnp.where(first_of_seq, 0.0, norm_mod(xp_ref[...]))
    h_next = jnp.where(last_of_seq, 0.0, norm_mod(xn_ref[...]))
    h_ext = jnp.concatenate([h_prev, norm_mod(x), h_next], axis=0).astype(BF16)
    u_ext = _dot(h_ext, win_ref[...])
    u = u_ext[SUBLANES:SUBLANES + tm]

    z_ref[rows, :] = u[:, C_Z0:C_XBC0].astype(z_ref.dtype)
    lane = lax.broadcasted_iota(jnp.int32, (1, LANES), 1)
    krb = u[:, C_KR0:C_SC0]
    dt_raw = jnp.where((lane >= DT_LANE0) & (lane < DT_LANE0 + 2 * SSD_HEADS), krb, 0.0)
    _ssd_decay_tables(dt_raw, alog_ref, dtbias_ref, cs_ref, side_ref, slot)

    conv = _conv3_rows(u_ext[:, C_XBC0:D_IN_PAD], cw_ref, tm) + cb_ref[...]
    act_ref[rows, :] = _silu(conv).astype(act_ref.dtype)

    w = SC_WIDTH
    prod = u_ext[:, C_SC0 + w:C_SC0 + 2 * w] * u_ext[:, C_SC0 + 2 * w:C_Z0]
    ysc_ref[rows, :] = (u[:, C_SC0:C_SC0 + w] * _conv3_rows(prod, scw_ref, tm)).astype(ysc_ref.dtype)

    cos = cos_ref[...]
    sin = sin_ref[...]
    nh = MLA_HEADS
    cos_h = jnp.concatenate([cos] * nh, axis=1)
    sin_h = jnp.concatenate([sin] * nh, axis=1)

    cq = _rms(u[:, C_Q0:C_KV0], qg_ref[...]).astype(BF16)
    q2 = _dot(cq, wuq_ref[...])
    qw = nh * HEAD_PAD
    q = (q2[:, :qw] * cos_h + q2[:, qw:] * sin_h) * Q_PRESCALE
    q_ref[rows, :] = q.astype(BF16)

    ckv = _rms(u[:, C_KV0:C_KR0], kvg_ref[...]).astype(BF16)
    kv = _dot(ckv, wukv_ref[...])
    rope_lane = (lane >= ROPE_LANE0) & (lane < ROPE_LANE0 + QK_ROPE)
    cos_k = jnp.where(rope_lane, cos, 0.0)
    kr = krb * cos_k + pltpu.roll(krb, LANES // 2, 1) * sin
    k = kv[:, :qw] + jnp.concatenate([kr] * nh, axis=1)
    k_b = k.astype(BF16)
    k_ref[rows, :] = k_b
    k_f = k_b.astype(F32)
    norms = []
    for h in range(nh):
        k_h = k_f[:, h * HEAD_PAD:(h + 1) * HEAD_PAD]
        n2 = jnp.max(jnp.sum(k_h * k_h, axis=1, keepdims=True), axis=0, keepdims=True)
        norms.append(jnp.broadcast_to(n2, (1, LANES)))
    kn_ref[slot] = jnp.concatenate(norms + [jnp.zeros((SUBLANES - nh, LANES), F32)], axis=0)
    v_ref[rows, :] = kv[:, qw:].astype(BF16)


def _layer_spec(a, layer, single_buffer=False):
    block = (1,) + a.shape[1:]
    index_map = lambda j: (layer,) + (0,) * (a.ndim - 1)
    if single_buffer:
        return pl.BlockSpec(block, index_map, pipeline_mode=pl.Buffered(1))
    return pl.BlockSpec(block, index_map)


def _inproj(src, mod, layer, gpre, win, qg, wuq, kvg, wukv, cos_t, sin_t, scw, cw, cb, alog, dtbias, layout):
    tm = TOKEN_TILE
    d = src[0].shape[-1]
    split = len(src) == 2
    qw = MLA_HEADS * HEAD_PAD
    nt = layout.n_tiles
    assert nt % TILES_PER_STEP == 0
    rows = nt * tm

    in_specs, operands = [], []
    for h in range(TILES_PER_STEP):
        def u_of(j, h=h):
            return j * TILES_PER_STEP + h

        def rope(j, u_of=u_of):
            return (layout.coords(u_of(j))[1], 0)

        in_specs += layout.source_specs(u_of, d, split) + layout.halo_specs(u_of, d, split)
        in_specs += [
            pl.BlockSpec((1, 1, N_MOD, d), lambda j, u_of=u_of: (layer, layout.mod_row(u_of(j)), 0, 0)),
            pl.BlockSpec((tm, LANES), rope), pl.BlockSpec((tm, LANES), rope),
        ]
        operands += list(src) + [src[-1], src[-1], mod, cos_t, sin_t]
    consts = [gpre, win, qg, wuq, kvg, wukv, scw, cw, cb, alog, dtbias]
    in_specs += [_layer_spec(c, layer, single_buffer=c is win) for c in consts]
    chunks_per_step = TILES_PER_STEP * tm // SSD_CHUNK

    def tok(width):
        return pl.BlockSpec((TILES_PER_STEP * tm, width), lambda j: (j, 0))

    out_shape = (
        jax.ShapeDtypeStruct((rows, qw), BF16),
        jax.ShapeDtypeStruct((rows, qw), BF16),
        jax.ShapeDtypeStruct((rows, MLA_WIDTH), BF16),
        jax.ShapeDtypeStruct((nt, SUBLANES, LANES), F32),
        jax.ShapeDtypeStruct((rows, SC_WIDTH), BF16),
        jax.ShapeDtypeStruct((rows, SSD_WIDTH), BF16),
        jax.ShapeDtypeStruct((rows, SSD_XBC), BF16),
        jax.ShapeDtypeStruct((rows, LANES), F32),
        jax.ShapeDtypeStruct((rows // SSD_CHUNK, SIDE_ROWS, SSD_CHUNK), F32),
    )
    return pl.pallas_call(
        functools.partial(_inproj_kernel, layout=layout, split=split),
        out_shape=out_shape,
        grid=(nt // TILES_PER_STEP,),
        in_specs=in_specs,
        out_specs=(tok(qw), tok(qw), tok(MLA_WIDTH),
                   pl.BlockSpec((TILES_PER_STEP, SUBLANES, LANES), lambda j: (j, 0, 0)),
                   tok(SC_WIDTH), tok(SSD_WIDTH), tok(SSD_XBC), tok(LANES),
                   pl.BlockSpec((chunks_per_step, SIDE_ROWS, SSD_CHUNK), lambda j: (j, 0, 0))),
        scratch_shapes=[pltpu.VMEM((d, D_IN_PAD), BF16),
                        pltpu.VMEM((Q_LORA, 2 * qw), BF16),
                        pltpu.VMEM((KV_LORA, qw + MLA_WIDTH), BF16)],
        compiler_params=pltpu.CompilerParams(
            dimension_semantics=("arbitrary",), vmem_limit_bytes=VMEM_LIMIT_BYTES),
        name="inproj",
    )(*operands, *consts)


def _attn_kernel(q_ref, k_ref, v_ref, kn_ref, o_ref, vt_ref, s_ref, p_ref, *, n_keys, n_ctx, ctx_queries):
    i = pl.program_id(1)
    tq = q_ref.shape[1]

    @pl.when(i == 0)
    def _():
        for c in range(n_keys // tq):
            v_t = v_ref[0, c * tq:(c + 1) * tq, :].astype(F32).T.astype(BF16)
            for h in range(MLA_HEADS):
                vt_ref[h, 0:V_HEAD, c * tq:(c + 1) * tq] = v_t[h * V_HEAD:(h + 1) * V_HEAD]
        for h in range(MLA_HEADS):
            vt_ref[h, V_HEAD:V_ROWS, :] = jnp.ones((V_ROWS - V_HEAD, n_keys), BF16)

    q_t = q_ref[0].astype(F32).T.astype(BF16)
    kn = jnp.max(kn_ref[0], axis=0)

    def finish(outs):
        o_ref[0] = jnp.concatenate(outs, axis=0).T.astype(o_ref.dtype)

    def head_out(ov):
        return ov[0:V_HEAD] / ov[V_HEAD:V_HEAD + 1]

    def attend_two_pass(nk):
        def scores(h):
            s_ref[h, 0:nk, :] = _dot(k_ref[0, 0:nk, h * HEAD_PAD:(h + 1) * HEAD_PAD],
                                     q_t[h * HEAD_PAD:(h + 1) * HEAD_PAD, :])

        def probs(h):
            s = s_ref[h, 0:nk, :]
            p_ref[h, 0:nk, :] = jnp.exp2(s - jnp.max(s, axis=0, keepdims=True)).astype(BF16)

        def values(h):
            return head_out(_dot(vt_ref[h, :, 0:nk], p_ref[h, 0:nk, :]))

        scores(0), scores(1)
        scores(2), scores(3), probs(0), probs(1)
        outs = [values(0), values(1)]
        probs(2), probs(3)
        outs += [values(2), values(3)]
        finish(outs)

    def attend_one_pass(nk, shift):
        outs = []
        for pair in range(MLA_HEADS // 2):
            heads = (2 * pair, 2 * pair + 1)
            s = [_dot(k_ref[0, 0:nk, h * HEAD_PAD:(h + 1) * HEAD_PAD],
                      q_t[h * HEAD_PAD:(h + 1) * HEAD_PAD, :]) for h in heads]
            p = [jnp.exp2(sh - shift[h]).astype(BF16) for h, sh in zip(heads, s)]
            outs += [head_out(_dot(vt_ref[h, :, 0:nk], ph)) for h, ph in zip(heads, p)]
        finish(outs)

    def attend(nk):
        bounds = []
        for h in range(MLA_HEADS):
            qf = q_t[h * HEAD_PAD:(h + 1) * HEAD_PAD, :].astype(F32)
            q2 = jnp.sum(qf * qf, axis=0, keepdims=True)
            bounds.append(jnp.sqrt(q2 * kn[h:h + 1, 0:1]) * BOUND_SLACK)
        worst = jnp.max(jnp.concatenate(bounds, axis=0))
        one_pass = worst <= SHIFT_MAX_BOUND

        @pl.when(one_pass)
        def _():
            attend_one_pass(nk, [b - SHIFT_HEADROOM for b in bounds])

        @pl.when(jnp.logical_not(one_pass))
        def _():
            attend_two_pass(nk)

    if ctx_queries:
        @pl.when(i == 0)
        def _():
            attend_two_pass(n_ctx)

        @pl.when(i > 0)
        def _():
            attend(n_keys)
    else:
        attend(n_keys)


def _attention(q, k, v, kn, n_ctx, ctx_queries):
    bsz, t, qw = k.shape
    tq = TOKEN_TILE
    t0 = 0 if ctx_queries else n_ctx // tq
    nq = t // tq - t0
    return pl.pallas_call(
        functools.partial(_attn_kernel, n_keys=t, n_ctx=n_ctx, ctx_queries=ctx_queries),
        out_shape=jax.ShapeDtypeStruct((bsz, nq * tq, MLA_WIDTH), BF16),
        grid=(bsz, nq),
        in_specs=[
            pl.BlockSpec((1, tq, qw), lambda b, i: (b, i + t0, 0)),
            pl.BlockSpec((1, t, qw), lambda b, i: (b, 0, 0)),
            pl.BlockSpec((1, t, MLA_WIDTH), lambda b, i: (b, 0, 0)),
            pl.BlockSpec((1,) + kn.shape[1:], lambda b, i: (b, 0, 0, 0)),
        ],
        out_specs=pl.BlockSpec((1, tq, MLA_WIDTH), lambda b, i: (b, i, 0)),
        scratch_shapes=[pltpu.VMEM((MLA_HEADS, V_ROWS, t), BF16),
                        pltpu.VMEM((MLA_HEADS, t, tq), F32), pltpu.VMEM((MLA_HEADS, t, tq), BF16)],
        compiler_params=pltpu.CompilerParams(
            dimension_semantics=("arbitrary", "arbitrary"), vmem_limit_bytes=VMEM_LIMIT_BYTES),
        name="attention",
    )(q, k, v, kn)


def _ssd_role(act, cs, side, direction, dskip_ref, y_ref, row0, h_ref):
    tc = SSD_CHUNK
    n = SSD_STATE
    xs = act[:, :SSD_WIDTH]
    bm = act[:, SSD_WIDTH:SSD_WIDTH + SSD_GN].astype(F32)
    cm = act[:, SSD_WIDTH + SSD_GN:]
    bm_t = bm.T

    r_i = lax.broadcasted_iota(jnp.int32, (tc, tc), 0)
    c_i = lax.broadcasted_iota(jnp.int32, (tc, tc), 1)
    tri = (c_i <= r_i) if direction == 0 else (c_i >= r_i)
    end = tc - 1 if direction == 0 else 0
    nd = 2 * SSD_HEADS
    cs_t, dt_t, w_t = side[0:nd], side[nd:2 * nd], side[2 * nd:3 * nd]
    dec_tot = jnp.exp(cs[end:end + 1, :])
    cm_f = cm.astype(F32)
    lane = lax.broadcasted_iota(jnp.int32, (1, LANES), 1)
    low_half = lane < SSD_HEAD_DIM
    zeros_h = jnp.zeros((n, LANES), BF16)
    zeros_lhs = jnp.zeros((n, tc), F32)

    for g in range(SSD_GROUPS):
        cb = _dot(cm[:, g * n:(g + 1) * n], bm_t[g * n:(g + 1) * n, :].astype(BF16))
        bt_g = bm_t[g * n:(g + 1) * n, :]
        for pr in range(HEADS_PER_GROUP // 2):
            slab = g * (HEADS_PER_GROUP // 2) + pr
            h_pair = h_ref[direction, g, :, pr * LANES:(pr + 1) * LANES]
            h_b = h_pair.astype(BF16)
            xs_pair = xs[:, slab * LANES:(slab + 1) * LANES]
            rhs = jnp.concatenate([xs_pair] + [h_b if gg == g else zeros_h for gg in range(SSD_GROUPS)], axis=0)
            lhs_rows = []
            cols = []
            for hh in range(2):
                r = direction * SSD_HEADS + 2 * slab + hh
                col = DT_LANE0 + r
                cols.append(col)
                cs_col = jnp.broadcast_to(cs[:, col:col + 1], (tc, tc))
                lm = jnp.exp(jnp.where(tri, cs_col - cs_t[r:r + 1, :], -jnp.inf))
                top = jnp.concatenate([cb * dt_t[r:r + 1, :] * lm, cm_f * jnp.exp(cs_col)], axis=1)
                bot = jnp.concatenate([bt_g * w_t[r:r + 1, :], zeros_lhs], axis=1)
                lhs_rows += [top, bot]
            lhs = jnp.concatenate(lhs_rows, axis=0).astype(BF16)
            out = _dot(lhs, rhs)
            m = tc + n
            y_pair = jnp.where(low_half, out[0:tc], out[m:m + tc])
            if direction == 0:
                y_pair = y_pair + dskip_ref[:, slab * LANES:(slab + 1) * LANES] * xs_pair.astype(F32)
            y_ref[0, row0:row0 + tc, slab * LANES:(slab + 1) * LANES] = y_pair
            h_new = jnp.where(low_half, out[tc:m], out[m + tc:2 * m])
            keep = jnp.where(low_half, dec_tot[:, cols[0]:cols[0] + 1], dec_tot[:, cols[1]:cols[1] + 1])
            h_ref[direction, g, :, pr * LANES:(pr + 1) * LANES] = h_pair * keep + h_new


def _bwd_block(i, n_blocks, ctx_blocks):
    return jnp.where(i < ctx_blocks, ctx_blocks - 1 - i, n_blocks - 1 + ctx_blocks - i)


def _ssd_kernel(af_ref, ab_ref, csf_ref, csb_ref, sidef_ref, sideb_ref, dskip_ref, yf_ref, yb_ref, h_ref):
    @pl.when(pl.program_id(1) == 0)
    def _():
        h_ref[...] = jnp.zeros_like(h_ref)

    dskip = dskip_ref.at[0]
    tc = SSD_CHUNK
    n_sub = af_ref.shape[1] // tc
    for j in range(n_sub):
        r0 = j * tc
        _ssd_role(af_ref[0, r0:r0 + tc, :], csf_ref[0, r0:r0 + tc, :], sidef_ref[0, j], 0, dskip, yf_ref, r0, h_ref)
        jb = n_sub - 1 - j
        r1 = jb * tc
        _ssd_role(ab_ref[0, r1:r1 + tc, :], csb_ref[0, r1:r1 + tc, :], sideb_ref[0, jb], 1, dskip, yb_ref, r1, h_ref)


def _ssd_scan(act, cs, side, layer, dskip, n_ctx):
    bsz, t, _ = act.shape
    tc = SSD_BLOCK
    n_blocks = t // tc
    ctx_blocks = n_ctx // tc
    assert n_ctx % tc == 0 and t % tc == 0

    def fwd(i):
        return i

    def bwd(i):
        return _bwd_block(i, n_blocks, ctx_blocks)

    def main(order, width):
        return pl.BlockSpec((1, tc, width), lambda b, i: (b, order(i), 0))

    def side_spec(order):
        return pl.BlockSpec((1, tc // SSD_CHUNK, SIDE_ROWS, SSD_CHUNK), lambda b, i: (b, order(i), 0, 0))

    def const(shape):
        return pl.BlockSpec((1,) + shape[1:], lambda b, i: (layer,) + (0,) * (len(shape) - 1))

    y_shape = jax.ShapeDtypeStruct((bsz, t, SSD_WIDTH), F32)
    return pl.pallas_call(
        _ssd_kernel,
        out_shape=(y_shape, y_shape),
        grid=(bsz, n_blocks),
        in_specs=[
            main(fwd, SSD_XBC), main(bwd, SSD_XBC), main(fwd, LANES), main(bwd, LANES),
            side_spec(fwd), side_spec(bwd),
            const(dskip.shape),
        ],
        out_specs=(main(fwd, SSD_WIDTH), main(bwd, SSD_WIDTH)),
        scratch_shapes=[pltpu.VMEM((2, SSD_GROUPS, SSD_STATE, HEADS_PER_GROUP * SSD_HEAD_DIM), F32)],
        compiler_params=pltpu.CompilerParams(
            dimension_semantics=("arbitrary", "arbitrary"), vmem_limit_bytes=VMEM_LIMIT_BYTES),
        name="ssd_scan",
    )(act, act, cs, cs, side, side, dskip)


def _mix_ffn_tile(x, mod_ref, att_ref, ysc_ref, z_ref, yf_ref, yb_ref, params, o_ref, row0, ycat_ref, acc_ref):
    normg_ref, wout_ref, gpost_ref, gpre2_ref, w1_ref, w2_ref, gpost2_ref = params
    tm = x.shape[0]
    gated = (yf_ref[...] + yb_ref[...]) * _silu(z_ref[...].astype(F32))
    gw = SSD_WIDTH // SSD_GROUPS
    w = SC_WIDTH
    ycat_ref[:, 0:MLA_WIDTH] = att_ref[...]
    ycat_ref[:, MLA_WIDTH:MLA_WIDTH + w] = ysc_ref[...]
    c0 = MLA_WIDTH + w
    for g in range(SSD_GROUPS):
        gg = gated[:, g * gw:(g + 1) * gw]
        gg = gg * lax.rsqrt(jnp.mean(gg * gg, axis=-1, keepdims=True) + EPS) * normg_ref[:, g * gw:(g + 1) * gw]
        ycat_ref[:, c0 + g * gw:c0 + (g + 1) * gw] = gg.astype(BF16)

    gate1 = mod_ref[0, 2:3, :]
    shift2 = mod_ref[0, 3:4, :]
    scale2 = mod_ref[0, 4:5, :]
    gate2 = mod_ref[0, 5:6, :]
    y = _dot(ycat_ref[...], wout_ref[...])
    x1 = x + gate1 * _rms(y, gpost_ref[...])

    h2 = (_rms(x1, gpre2_ref[...]) * (1.0 + scale2) + shift2).astype(BF16)
    d_ff = w1_ref.shape[1]
    for c in range(d_ff // FF_CHUNK):
        a = _dot(h2, w1_ref[:, c * FF_CHUNK:(c + 1) * FF_CHUNK])
        r = jnp.square(jnp.maximum(a, 0.0)).astype(BF16)
        part = _dot(r, w2_ref[c * FF_CHUNK:(c + 1) * FF_CHUNK, :])
        if c == 0:
            acc_ref[...] = part
        else:
            acc_ref[...] += part
    o_ref[row0:row0 + tm, :] = x1 + gate2 * _rms(acc_ref[...], gpost2_ref[...])


N_MIX_PARAMS = 7
N_MIX_TOKEN_INPUTS = 6


def _mix_ffn_kernel(*refs, layout, latent_only, split):
    n_x = 2 if split else 1
    per_half = n_x + N_MIX_TOKEN_INPUTS
    n_in = TILES_PER_STEP * per_half
    params = [r.at[0] for r in refs[n_in:n_in + N_MIX_PARAMS]]
    o_ref = refs[n_in + N_MIX_PARAMS]
    ycat_ref, acc_ref = refs[n_in + N_MIX_PARAMS + 1:]
    tm = TOKEN_TILE
    for h in range(TILES_PER_STEP):
        half = refs[h * per_half:(h + 1) * per_half]
        u = layout.unified_tile(pl.program_id(0) * TILES_PER_STEP + h, latent_only)
        x = layout.select_source(u, half[:n_x])
        mod_ref, *token_refs = half[n_x:]
        _mix_ffn_tile(x, mod_ref.at[0], *token_refs, params, o_ref, h * tm, ycat_ref.at[h], acc_ref.at[h])


def _mix_ffn(src, mod, layer, att, ysc, z, yf, yb, normg, wout, gpost, gpre2, w1, w2, gpost2, layout,
             latent_only):
    tm = TOKEN_TILE
    d = src[0].shape[-1]
    n_out_tiles = layout.bsz * (layout.lat_tiles if latent_only else layout.tiles_per_batch)
    assert n_out_tiles % TILES_PER_STEP == 0
    att_unified = att.shape[0] == layout.bsz * layout.tiles_per_batch * tm

    in_specs, operands = [], []
    for h in range(TILES_PER_STEP):
        def u_of(j, h=h):
            return layout.unified_tile(j * TILES_PER_STEP + h, latent_only)

        def tok(width, tile_of=u_of):
            return pl.BlockSpec((tm, width), lambda j: (tile_of(j), 0))

        in_specs += layout.source_specs(u_of, d, len(src) == 2)
        in_specs += [
            pl.BlockSpec((1, 1, N_MOD, d), lambda j, u_of=u_of: (layer, layout.mod_row(u_of(j)), 0, 0)),
            tok(MLA_WIDTH) if att_unified else tok(MLA_WIDTH, lambda j, h=h: j * TILES_PER_STEP + h),
            tok(SC_WIDTH), tok(SSD_WIDTH), tok(SSD_WIDTH), tok(SSD_WIDTH),
        ]
        operands += list(src) + [mod, att, ysc, z, yf, yb]
    consts = [normg, wout, gpost, gpre2, w1, w2, gpost2]
    in_specs += [_layer_spec(c, layer, single_buffer=True) for c in consts]
    return pl.pallas_call(
        functools.partial(_mix_ffn_kernel, layout=layout, latent_only=latent_only, split=len(src) == 2),
        out_shape=jax.ShapeDtypeStruct((n_out_tiles * tm, d), F32),
        grid=(n_out_tiles // TILES_PER_STEP,),
        in_specs=in_specs,
        out_specs=pl.BlockSpec((TILES_PER_STEP * tm, d), lambda j: (j, 0)),
        scratch_shapes=[pltpu.VMEM((TILES_PER_STEP, tm, d), BF16), pltpu.VMEM((TILES_PER_STEP, tm, d), F32)],
        compiler_params=pltpu.CompilerParams(
            dimension_semantics=("parallel",), vmem_limit_bytes=VMEM_LIMIT_BYTES),
        name="mix_ffn",
    )(*operands, *consts)


def _rope_tables(n_ctx, seq):
    f32 = np.float32
    half = QK_ROPE // 2
    inv_freq = (f32(ROPE_THETA) ** (-np.arange(0, half, 2, dtype=f32) / f32(half))).astype(f32)
    rows = seq // GRID_W
    row = np.repeat(np.arange(rows, dtype=f32), GRID_W)
    col = np.tile(np.arange(GRID_W, dtype=f32), rows)
    ang_r = row[:, None] * inv_freq
    ang_c = col[:, None] * inv_freq
    ang = np.concatenate([ang_r, ang_r, ang_c, ang_c], axis=-1).astype(f32)
    t = n_ctx + seq
    cos_t = np.ones((t, LANES), f32)
    sin_t = np.zeros((t, LANES), f32)
    cos_t[n_ctx:, ROPE_LANE0:ROPE_LANE0 + QK_ROPE] = np.cos(ang)
    sin_t[n_ctx:, ROPE_LANE0:ROPE_LANE0 + QK_ROPE] = np.sin(ang)
    return jnp.asarray(cos_t), jnp.asarray(sin_t)


def kernel(x, c, ctx, c_ctx, w_mod, b_mod, g_pre_mix, w_in, mla_q_norm, w_uq, mla_kv_norm, w_ukv, sc_conv_w, ssd_conv_w, ssd_conv_b, ssd_a_log, ssd_dt_bias, ssd_d, ssd_norm, w_out, g_post_mix, g_pre_ffn, w_ff1, w_ff2, g_post_ffn):
    bsz, seq, d = x.shape
    n_ctx = ctx.shape[1]
    depth = w_mod.shape[0]
    assert n_ctx == TOKEN_TILE and seq % TOKEN_TILE == 0 and seq % GRID_W == 0
    layout = _TokenLayout(bsz, n_ctx, seq)
    t = n_ctx + seq

    cvec = jnp.zeros((SUBLANES, d), F32).at[:bsz].set(c).at[layout.ctx_row].set(c_ctx)
    mod_all = _modulation(cvec, w_mod, b_mod).reshape(depth, SUBLANES, N_MOD, d)
    cos_t, sin_t = _rope_tables(n_ctx, seq)

    src = (ctx.reshape(bsz * n_ctx, d), x.reshape(bsz * seq, d))

    def rows3(a):
        return a.reshape(depth, 1, -1)

    def dt_lanes(a):
        flat = a.reshape(depth, 1, -1)
        return jnp.pad(flat, ((0, 0), (0, 0), (DT_LANE0, LANES - DT_LANE0 - flat.shape[-1])))

    alog, dtbias = dt_lanes(ssd_a_log), dt_lanes(ssd_dt_bias)
    dskip = rows3(jnp.repeat(ssd_d, SSD_HEAD_DIM, axis=-1))
    w_out_b, w_ff1_b, w_ff2_b = w_out.astype(BF16), w_ff1.astype(BF16), w_ff2.astype(BF16)
    w_in_t = jnp.swapaxes(w_in, 1, 2)

    def per_batch(a):
        return a.reshape((bsz, t) + a.shape[1:])

    for i in range(depth):
        last = i == depth - 1
        q, k, v, kn, ysc, z, act, cs, side = _inproj(
            src, mod_all, i, rows3(g_pre_mix), w_in_t, rows3(mla_q_norm), w_uq, rows3(mla_kv_norm), w_ukv,
            cos_t, sin_t, sc_conv_w, ssd_conv_w, rows3(ssd_conv_b), alog, dtbias, layout)
        att = _attention(per_batch(q), per_batch(k), per_batch(v),
                         kn.reshape(bsz, layout.tiles_per_batch, SUBLANES, LANES), n_ctx, ctx_queries=not last)
        yf, yb = _ssd_scan(per_batch(act), per_batch(cs), side.reshape(bsz, t // SSD_CHUNK, SIDE_ROWS, SSD_CHUNK),
                           i, dskip, n_ctx)
        out = _mix_ffn(src, mod_all, i, att.reshape(-1, MLA_WIDTH), ysc, z,
                       yf.reshape(-1, SSD_WIDTH), yb.reshape(-1, SSD_WIDTH), rows3(ssd_norm),
                       w_out_b, rows3(g_post_mix), rows3(g_pre_ffn), w_ff1_b, w_ff2_b, rows3(g_post_ffn),
                       layout, latent_only=last)
        src = (out,)
    return out.reshape(bsz, seq, d)
```

```python
import functools
import math

import jax
import jax.numpy as jnp
import numpy as np
from jax import lax
from jax.experimental import pallas as pl
from jax.experimental.pallas import tpu as pltpu

F32 = jnp.float32
BF16 = jnp.bfloat16

GRID_W = 64
EPS = 1e-6
N_MOD = 6
MLA_HEADS = 4
Q_LORA = 256
KV_LORA = 128
QK_NOPE = 64
QK_ROPE = 32
V_HEAD = 64
MLA_WIDTH = MLA_HEADS * V_HEAD
MLA_SCALE = (QK_NOPE + QK_ROPE) ** -0.5
ROPE_THETA = 10000.0
SC_WIDTH = 256
SSD_HEADS = 8
SSD_HEAD_DIM = 64
SSD_WIDTH = SSD_HEADS * SSD_HEAD_DIM
SSD_GROUPS = 2
SSD_STATE = 64
SSD_CHUNK = 128
SSD_GN = SSD_GROUPS * SSD_STATE
SSD_XBC = SSD_WIDTH + 2 * SSD_GN
HEADS_PER_GROUP = SSD_HEADS // SSD_GROUPS
IN_MLA = Q_LORA + KV_LORA + QK_ROPE
IN_SC = 3 * SC_WIDTH

LANES = 128
SUBLANES = 8
VMEM_LIMIT_BYTES = 56 * 1024 * 1024

HEAD_PAD = LANES
V_ROWS = V_HEAD + 16
ROPE_LANE0 = QK_NOPE
TOKEN_TILE = 256
TILES_PER_STEP = 2
SSD_BLOCK = 2 * SSD_CHUNK
FF_CHUNK = 1024

C_Q0 = 0
C_KV0 = C_Q0 + Q_LORA
C_KR0 = C_KV0 + KV_LORA
C_SC0 = C_KR0 + LANES
C_Z0 = C_SC0 + IN_SC
C_XBC0 = C_Z0 + SSD_WIDTH
D_IN_PAD = C_XBC0 + SSD_XBC
DT_LANE0 = QK_ROPE

Q_PRESCALE = MLA_SCALE * math.log2(math.e)

SHIFT_HEADROOM = 60.0
SHIFT_MAX_BOUND = 90.0
BOUND_SLACK = 1.0 + 2.0 ** -8


def _rms(x, g):
    return x * lax.rsqrt(jnp.mean(x * x, axis=-1, keepdims=True) + EPS) * g


def _silu(x):
    return x / (1.0 + jnp.exp(-x))


def _dot(a, b):
    return jnp.dot(a, b, preferred_element_type=F32)


class _TokenLayout:
    def __init__(self, bsz, n_ctx, seq):
        tm = TOKEN_TILE
        assert n_ctx % tm == 0 and seq % tm == 0
        self.bsz = bsz
        self.ctx_tiles = n_ctx // tm
        self.lat_tiles = seq // tm
        self.tiles_per_batch = self.ctx_tiles + self.lat_tiles
        self.n_tiles = bsz * self.tiles_per_batch
        self.ctx_row = bsz

    def unified_tile(self, o, latent_only):
        if not latent_only:
            return o
        return o + (o // self.lat_tiles + 1) * self.ctx_tiles

    def coords(self, u):
        b = u // self.tiles_per_batch
        return b, u - b * self.tiles_per_batch

    def mod_row(self, u):
        b, i = self.coords(u)
        return jnp.where(i < self.ctx_tiles, self.ctx_row, b)

    def seq_edges(self, u):
        _, i = self.coords(u)
        first = (i == 0) | (i == self.ctx_tiles)
        last = (i == self.ctx_tiles - 1) | (i == self.tiles_per_batch - 1)
        return first, last

    def _latent_tile(self, u):
        b, i = self.coords(u)
        return b * self.lat_tiles + jnp.maximum(i - self.ctx_tiles, 0)

    def _context_tile(self, u):
        b, i = self.coords(u)
        return b * self.ctx_tiles + jnp.minimum(i, self.ctx_tiles - 1)

    def source_specs(self, u_of, d, split):
        tm = TOKEN_TILE
        if not split:
            return [pl.BlockSpec((tm, d), lambda j: (u_of(j), 0))]
        return [pl.BlockSpec((tm, d), lambda j: (self._context_tile(u_of(j)), 0)),
                pl.BlockSpec((tm, d), lambda j: (self._latent_tile(u_of(j)), 0))]

    def halo_specs(self, u_of, d, split):
        rb = TOKEN_TILE // SUBLANES
        if split:
            assert self.ctx_tiles == 1
            tile_of, n_rb = (lambda j: self._latent_tile(u_of(j))), self.bsz * self.lat_tiles * rb
        else:
            tile_of, n_rb = u_of, self.n_tiles * rb
        return [pl.BlockSpec((SUBLANES, d), lambda j: (jnp.maximum(tile_of(j) * rb - 1, 0), 0)),
                pl.BlockSpec((SUBLANES, d), lambda j: (jnp.minimum((tile_of(j) + 1) * rb, n_rb - 1), 0))]

    def select_source(self, u, refs):
        if len(refs) == 1:
            return refs[0][...]
        _, i = self.coords(u)
        return jnp.where(i < self.ctx_tiles, refs[0][...], refs[1][...])


def _mod_kernel(c_ref, w_ref, b_ref, o_ref):
    s = _silu(c_ref[...]).astype(BF16)
    o_ref[0] = _dot(s, w_ref[0].astype(BF16)) + b_ref[0]


def _modulation(cvec, w_mod, b_mod):
    depth, d, nd = w_mod.shape
    rows = cvec.shape[0]
    return pl.pallas_call(
        _mod_kernel,
        out_shape=jax.ShapeDtypeStruct((depth, rows, nd), F32),
        grid=(depth, nd // d),
        in_specs=[
            pl.BlockSpec((rows, d), lambda l, j: (0, 0)),
            pl.BlockSpec((1, d, d), lambda l, j: (l, 0, j)),
            pl.BlockSpec((1, 1, d), lambda l, j: (l, 0, j)),
        ],
        out_specs=pl.BlockSpec((1, rows, d), lambda l, j: (l, 0, j)),
        compiler_params=pltpu.CompilerParams(dimension_semantics=("parallel", "parallel")),
        name="modulation",
    )(cvec, w_mod, b_mod.reshape(depth, 1, nd))


def _conv3_rows(a_ext, w_ref, tm):
    te = a_ext.shape[0]
    lo, hi = SUBLANES, SUBLANES + tm
    prev = pltpu.roll(a_ext, 1, 0)[lo:hi]
    nxt = pltpu.roll(a_ext, te - 1, 0)[lo:hi]
    return w_ref[0:1, :] * prev + w_ref[1:2, :] * a_ext[lo:hi] + w_ref[2:3, :] * nxt


N_INPROJ_PARAMS = 11
N_INPROJ_TOKEN_INPUTS = 5


def _split3(x):
    hi = x.astype(BF16)
    r = x - hi.astype(F32)
    mid = r.astype(BF16)
    lo = (r - mid.astype(F32)).astype(BF16)
    return hi, mid, lo


SIDE_ROWS = 3 * 2 * SSD_HEADS


def _ssd_decay_tables(dt_raw, alog_ref, dtbias_ref, cs_ref, side_ref, slot):
    tc = SSD_CHUNK
    tm = dt_raw.shape[0]
    nd = 2 * SSD_HEADS
    a = -jnp.exp(alog_ref[...])
    xb = dt_raw + dtbias_ref[...]
    dt = jnp.maximum(xb, 0.0) + jnp.log(1.0 + jnp.exp(-jnp.abs(xb)))
    la = dt * a
    r_i = lax.broadcasted_iota(jnp.int32, (tc, tc), 0)
    c_i = lax.broadcasted_iota(jnp.int32, (tc, tc), 1)
    tri_f = jnp.where(c_i <= r_i, 1.0, 0.0).astype(BF16)
    tri_b = jnp.where(c_i >= r_i, 1.0, 0.0).astype(BF16)
    lane = lax.broadcasted_iota(jnp.int32, (1, LANES), 1)
    fwd_lane = lane < DT_LANE0 + SSD_HEADS
    row = lax.broadcasted_iota(jnp.int32, (nd, 1), 0)
    for c in range(tm // tc):
        parts = _split3(la[c * tc:(c + 1) * tc])
        cs_f = sum(_dot(tri_f, p) for p in parts)
        cs_b = sum(_dot(tri_b, p) for p in parts)
        cs = jnp.where(fwd_lane, cs_f, cs_b)
        cs_ref[slot * tm + c * tc:slot * tm + (c + 1) * tc, :] = cs
        cs_t = cs.T[DT_LANE0:DT_LANE0 + nd]
        dt_t = dt[c * tc:(c + 1) * tc].T[DT_LANE0:DT_LANE0 + nd]
        cs_end = jnp.where(row < SSD_HEADS, cs_t[:, tc - 1:tc], cs_t[:, 0:1])
        w_t = jnp.exp(cs_end - cs_t) * dt_t
        side_ref[slot * (tm // tc) + c] = jnp.concatenate([cs_t, dt_t, w_t], axis=0)


def _rotate_cols(w):
    a, b, c, d = jnp.split(w, 4, axis=-1)
    return jnp.concatenate([-b, a, -d, c], axis=-1)


def _relayout_weights(win_raw, wuq_raw, wukv_raw, win_ref, wuq_ref, wukv_ref):
    d = win_raw.shape[1]
    w_kr = win_raw[Q_LORA + KV_LORA:IN_MLA, :]
    w_dt = win_raw[IN_MLA + IN_SC + SSD_WIDTH + SSD_XBC:, :]
    a, b, c, e = (w_kr[i * SUBLANES:(i + 1) * SUBLANES] for i in range(4))
    kr_block = jnp.concatenate([
        -b, a, -e, c,
        w_dt, jnp.zeros((LANES // 2 - QK_ROPE - w_dt.shape[0], d), F32),
        w_kr, jnp.zeros((LANES // 2 - QK_ROPE, d), F32)], axis=0)
    win_ref[:, C_KR0:C_SC0] = kr_block.T.astype(BF16)
    step = 2 * LANES
    for src0, dst0, n_rows in ((0, C_Q0, Q_LORA + KV_LORA), (IN_MLA, C_SC0, IN_SC + SSD_WIDTH + SSD_XBC)):
        for off in range(0, n_rows, step):
            rows = min(step, n_rows - off)
            blk = win_raw[src0 + off:src0 + off + rows, :]
            win_ref[:, dst0 + off:dst0 + off + rows] = blk.T.astype(BF16)

    dqk = QK_NOPE + QK_ROPE
    wq = wuq_raw[...]
    nq = wq.shape[0]
    plain, rot = [], []
    for h in range(MLA_HEADS):
        wh = wq[:, h * dqk:(h + 1) * dqk]
        pad = jnp.zeros((nq, HEAD_PAD - dqk), F32)
        plain += [wh, pad]
        rot += [jnp.zeros((nq, QK_NOPE), F32), _rotate_cols(wh[:, QK_NOPE:]), pad]
    wuq_ref[...] = jnp.concatenate(plain + rot, axis=1).astype(BF16)

    dkv = QK_NOPE + V_HEAD
    wkv = wukv_raw[...]
    nkv = wkv.shape[0]
    ks, vs = [], []
    for h in range(MLA_HEADS):
        wh = wkv[:, h * dkv:(h + 1) * dkv]
        ks += [wh[:, :QK_NOPE], jnp.zeros((nkv, HEAD_PAD - QK_NOPE), F32)]
        vs.append(wh[:, QK_NOPE:])
    wukv_ref[...] = jnp.concatenate(ks + vs, axis=1).astype(BF16)


def _inproj_kernel(*refs, layout, split):
    n_x = 2 if split else 1
    per_half = n_x + N_INPROJ_TOKEN_INPUTS
    n_in = TILES_PER_STEP * per_half
    (gpre, win_raw, qg, wuq_raw, kvg, wukv_raw, scw, cw, cb, alog, dtbias) = [
        r.at[0] for r in refs[n_in:n_in + N_INPROJ_PARAMS]]
    n_out = len(refs) - n_in - N_INPROJ_PARAMS - 3
    outs = refs[n_in + N_INPROJ_PARAMS:n_in + N_INPROJ_PARAMS + n_out]
    win_ref, wuq_ref, wukv_ref = refs[n_in + N_INPROJ_PARAMS + n_out:]

    @pl.when(pl.program_id(0) == 0)
    def _():
        _relayout_weights(win_raw, wuq_raw, wukv_raw, win_ref, wuq_ref, wukv_ref)

    params = (qg, wuq_ref, kvg, wukv_ref, scw, cw, cb, alog, dtbias)
    for h in range(TILES_PER_STEP):
        half = refs[h * per_half:(h + 1) * per_half]
        u = pl.program_id(0) * TILES_PER_STEP + h
        x = layout.select_source(u, half[:n_x])
        xp_ref, xn_ref, mod_ref, cos_ref, sin_ref = half[n_x:]
        h_ext = _inproj_normalise(x, xp_ref, xn_ref, mod_ref.at[0], gpre, layout.seq_edges(u))
        u_ext = _dot(h_ext, win_ref[...])
        _inproj_finish(u_ext, cos_ref, sin_ref, params, outs, h)


def _inproj_normalise(x, xp_ref, xn_ref, mod_ref, gpre_ref, seq_edges):
    first_of_seq, last_of_seq = seq_edges
    shift = mod_ref[0, 0:1, :]
    scale = mod_ref[0, 1:2, :]

    def norm_mod(xx):
        return _rms(xx, gpre_ref[...]) * (1.0 + scale) + shift

    h_prev = jnp.where(first_of_seq, 0.0, norm_mod(xp_ref[...]))
    h_next = jnp.where(last_of_seq, 0.0, norm_mod(xn_ref[...]))
    return jnp.concatenate([h_prev, norm_mod(x), h_next], axis=0).astype(BF16)


def _inproj_finish(u_ext, cos_ref, sin_ref, params, outs, slot):
    qg_ref, wuq_ref, kvg_ref, wukv_ref, scw_ref, cw_ref, cb_ref, alog_ref, dtbias_ref = params
    qt_ref, k_ref, vt_ref, kn_ref, qn_ref, ysc_ref, z_ref, act_ref, cs_ref, side_ref = outs
    tm = u_ext.shape[0] - 2 * SUBLANES
    rows = slice(slot * tm, (slot + 1) * tm)
    own = slice(SUBLANES, SUBLANES + tm)

    z_ref[rows, :] = u_ext[own, C_Z0:C_XBC0].astype(z_ref.dtype)
    lane = lax.broadcasted_iota(jnp.int32, (1, LANES), 1)
    krb = u_ext[own, C_KR0:C_SC0]
    dt_raw = jnp.where((lane >= DT_LANE0) & (lane < DT_LANE0 + 2 * SSD_HEADS), krb, 0.0)
    _ssd_decay_tables(dt_raw, alog_ref, dtbias_ref, cs_ref, side_ref, slot)

    conv = _conv3_rows(u_ext[:, C_XBC0:D_IN_PAD], cw_ref, tm) + cb_ref[...]
    act_ref[rows, :] = _silu(conv).astype(act_ref.dtype)

    w = SC_WIDTH
    prod = u_ext[:, C_SC0 + w:C_SC0 + 2 * w] * u_ext[:, C_SC0 + 2 * w:C_Z0]
    ysc_ref[rows, :] = (u_ext[own, C_SC0:C_SC0 + w] * _conv3_rows(prod, scw_ref, tm)).astype(ysc_ref.dtype)

    cos = cos_ref[...]
    sin = sin_ref[...]
    nh = MLA_HEADS
    cos_h = jnp.concatenate([cos] * nh, axis=1)
    sin_h = jnp.concatenate([sin] * nh, axis=1)

    def head_norm_maxima(a_f32):
        out = []
        for h in range(nh):
            a_h = a_f32[:, h * HEAD_PAD:(h + 1) * HEAD_PAD]
            n2 = jnp.max(jnp.sum(a_h * a_h, axis=1, keepdims=True), axis=0, keepdims=True)
            out.append(jnp.broadcast_to(n2, (1, LANES)))
        return jnp.concatenate(out + [jnp.zeros((SUBLANES - nh, LANES), F32)], axis=0)

    cq = _rms(u_ext[own, C_Q0:C_KV0], qg_ref[...]).astype(BF16)
    q2 = _dot(cq, wuq_ref[...])
    qw = nh * HEAD_PAD
    q_b = ((q2[:, :qw] * cos_h + q2[:, qw:] * sin_h) * Q_PRESCALE).astype(BF16)
    q_f = q_b.astype(F32)
    qt_ref[:, rows] = q_f.T.astype(BF16)
    qn_ref[slot] = head_norm_maxima(q_f)

    ckv = _rms(u_ext[own, C_KV0:C_KR0], kvg_ref[...]).astype(BF16)
    kv = _dot(ckv, wukv_ref[...])
    rope_lane = (lane >= ROPE_LANE0) & (lane < ROPE_LANE0 + QK_ROPE)
    cos_k = jnp.where(rope_lane, cos, 0.0)
    kr = krb * cos_k + pltpu.roll(krb, LANES // 2, 1) * sin
    k = kv[:, :qw] + jnp.concatenate([kr] * nh, axis=1)
    k_b = k.astype(BF16)
    k_ref[rows, :] = k_b
    kn_ref[slot] = head_norm_maxima(k_b.astype(F32))
    v_t = kv[:, qw:].T.astype(BF16)
    for h in range(nh):
        vt_ref[h, 0:V_HEAD, rows] = v_t[h * V_HEAD:(h + 1) * V_HEAD]
        vt_ref[h, V_HEAD:V_ROWS, rows] = jnp.ones((V_ROWS - V_HEAD, tm), BF16)


def _layer_spec(a, layer, single_buffer=False):
    block = (1,) + a.shape[1:]
    index_map = lambda j: (layer,) + (0,) * (a.ndim - 1)
    if single_buffer:
        return pl.BlockSpec(block, index_map, pipeline_mode=pl.Buffered(1))
    return pl.BlockSpec(block, index_map)


def _inproj(src, mod, layer, gpre, win, qg, wuq, kvg, wukv, cos_t, sin_t, scw, cw, cb, alog, dtbias, layout):
    tm = TOKEN_TILE
    d = src[0].shape[-1]
    split = len(src) == 2
    qw = MLA_HEADS * HEAD_PAD
    nt = layout.n_tiles
    assert nt % TILES_PER_STEP == 0
    rows = nt * tm

    in_specs, operands = [], []
    for h in range(TILES_PER_STEP):
        def u_of(j, h=h):
            return j * TILES_PER_STEP + h

        def rope(j, u_of=u_of):
            return (layout.coords(u_of(j))[1], 0)

        in_specs += layout.source_specs(u_of, d, split) + layout.halo_specs(u_of, d, split)
        in_specs += [
            pl.BlockSpec((1, 1, N_MOD, d), lambda j, u_of=u_of: (layer, layout.mod_row(u_of(j)), 0, 0)),
            pl.BlockSpec((tm, LANES), rope), pl.BlockSpec((tm, LANES), rope),
        ]
        operands += list(src) + [src[-1], src[-1], mod, cos_t, sin_t]
    consts = [gpre, win, qg, wuq, kvg, wukv, scw, cw, cb, alog, dtbias]
    in_specs += [_layer_spec(c, layer, single_buffer=c is win) for c in consts]
    chunks_per_step = TILES_PER_STEP * tm // SSD_CHUNK

    def tok(width):
        return pl.BlockSpec((TILES_PER_STEP * tm, width), lambda j: (j, 0))

    step_rows = TILES_PER_STEP * tm
    norm_shape = jax.ShapeDtypeStruct((nt, SUBLANES, LANES), F32)
    norm_spec = pl.BlockSpec((TILES_PER_STEP, SUBLANES, LANES), lambda j: (j, 0, 0))
    out_shape = (
        jax.ShapeDtypeStruct((qw, rows), BF16),
        jax.ShapeDtypeStruct((rows, qw), BF16),
        jax.ShapeDtypeStruct((MLA_HEADS, V_ROWS, rows), BF16),
        norm_shape,
        norm_shape,
        jax.ShapeDtypeStruct((rows, SC_WIDTH), BF16),
        jax.ShapeDtypeStruct((rows, SSD_WIDTH), BF16),
        jax.ShapeDtypeStruct((rows, SSD_XBC), BF16),
        jax.ShapeDtypeStruct((rows, LANES), F32),
        jax.ShapeDtypeStruct((rows // SSD_CHUNK, SIDE_ROWS, SSD_CHUNK), F32),
    )
    return pl.pallas_call(
        functools.partial(_inproj_kernel, layout=layout, split=split),
        out_shape=out_shape,
        grid=(nt // TILES_PER_STEP,),
        in_specs=in_specs,
        out_specs=(pl.BlockSpec((qw, step_rows), lambda j: (0, j)), tok(qw),
                   pl.BlockSpec((MLA_HEADS, V_ROWS, step_rows), lambda j: (0, 0, j)),
                   norm_spec, norm_spec,
                   tok(SC_WIDTH), tok(SSD_WIDTH), tok(SSD_XBC), tok(LANES),
                   pl.BlockSpec((chunks_per_step, SIDE_ROWS, SSD_CHUNK), lambda j: (j, 0, 0))),
        scratch_shapes=[pltpu.VMEM((d, D_IN_PAD), BF16),
                        pltpu.VMEM((Q_LORA, 2 * qw), BF16),
                        pltpu.VMEM((KV_LORA, qw + MLA_WIDTH), BF16)],
        compiler_params=pltpu.CompilerParams(
            dimension_semantics=("arbitrary",), vmem_limit_bytes=VMEM_LIMIT_BYTES),
        name="inproj",
    )(*operands, *consts)


def _attn_kernel(qt_ref, k_ref, vt_ref, kn_ref, qn_ref, o_ref, s_ref, p_ref, *, n_keys, n_ctx, ctx_queries):
    i = pl.program_id(1)
    q_t = qt_ref
    kn = jnp.max(kn_ref[0], axis=0)

    def finish(outs):
        o_ref[0] = jnp.concatenate(outs, axis=0).T.astype(o_ref.dtype)

    def head_out(ov):
        return ov[0:V_HEAD] / ov[V_HEAD:V_HEAD + 1]

    def attend_two_pass(nk):
        def scores(h):
            s_ref[h, 0:nk, :] = _dot(k_ref[0, 0:nk, h * HEAD_PAD:(h + 1) * HEAD_PAD],
                                     q_t[h * HEAD_PAD:(h + 1) * HEAD_PAD, :])

        def probs(h):
            s = s_ref[h, 0:nk, :]
            p_ref[h, 0:nk, :] = jnp.exp2(s - jnp.max(s, axis=0, keepdims=True)).astype(BF16)

        def values(h):
            return head_out(_dot(vt_ref[h, :, 0:nk], p_ref[h, 0:nk, :]))

        scores(0), scores(1)
        scores(2), scores(3), probs(0), probs(1)
        outs = [values(0), values(1)]
        probs(2), probs(3)
        outs += [values(2), values(3)]
        finish(outs)

    def attend_one_pass(nk, shift):
        outs = []
        for pair in range(MLA_HEADS // 2):
            heads = (2 * pair, 2 * pair + 1)
            s = [_dot(k_ref[0, 0:nk, h * HEAD_PAD:(h + 1) * HEAD_PAD],
                      q_t[h * HEAD_PAD:(h + 1) * HEAD_PAD, :]) for h in heads]
            p = [jnp.exp2(sh - shift[h]).astype(BF16) for h, sh in zip(heads, s)]
            outs += [head_out(_dot(vt_ref[h, :, 0:nk], ph)) for h, ph in zip(heads, p)]
        finish(outs)

    def attend(nk):
        bound = jnp.sqrt(qn_ref[0] * kn) * BOUND_SLACK
        one_pass = jnp.max(bound) <= SHIFT_MAX_BOUND

        @pl.when(one_pass)
        def _():
            attend_one_pass(nk, [bound[h:h + 1, 0:1] - SHIFT_HEADROOM for h in range(MLA_HEADS)])

        @pl.when(jnp.logical_not(one_pass))
        def _():
            attend_two_pass(nk)

    if ctx_queries:
        @pl.when(i == 0)
        def _():
            attend_two_pass(n_ctx)

        @pl.when(i > 0)
        def _():
            attend(n_keys)
    else:
        attend(n_keys)


def _attention(qt, k, vt, kn, qn, n_ctx, ctx_queries):
    bsz, t, qw = k.shape
    tq = TOKEN_TILE
    tiles = t // tq
    t0 = 0 if ctx_queries else n_ctx // tq
    nq = tiles - t0
    return pl.pallas_call(
        functools.partial(_attn_kernel, n_keys=t, n_ctx=n_ctx, ctx_queries=ctx_queries),
        out_shape=jax.ShapeDtypeStruct((bsz, nq * tq, MLA_WIDTH), BF16),
        grid=(bsz, nq),
        in_specs=[
            pl.BlockSpec((qw, tq), lambda b, i: (0, b * tiles + i + t0)),
            pl.BlockSpec((1, t, qw), lambda b, i: (b, 0, 0)),
            pl.BlockSpec((MLA_HEADS, V_ROWS, t), lambda b, i: (0, 0, b)),
            pl.BlockSpec((1,) + kn.shape[1:], lambda b, i: (b, 0, 0, 0)),
            pl.BlockSpec((1, SUBLANES, LANES), lambda b, i: (b * tiles + i + t0, 0, 0)),
        ],
        out_specs=pl.BlockSpec((1, tq, MLA_WIDTH), lambda b, i: (b, i, 0)),
        scratch_shapes=[pltpu.VMEM((MLA_HEADS, t, tq), F32), pltpu.VMEM((MLA_HEADS, t, tq), BF16)],
        compiler_params=pltpu.CompilerParams(
            dimension_semantics=("parallel", "parallel"), vmem_limit_bytes=VMEM_LIMIT_BYTES),
        name="attention",
    )(qt, k, vt, kn, qn)


def _ssd_role(act, cs, side, direction, dskip_ref, y_ref, row0, h_ref):
    tc = SSD_CHUNK
    n = SSD_STATE
    xs = act[:, :SSD_WIDTH]
    bm = act[:, SSD_WIDTH:SSD_WIDTH + SSD_GN].astype(F32)
    cm = act[:, SSD_WIDTH + SSD_GN:]
    bm_t = bm.T

    r_i = lax.broadcasted_iota(jnp.int32, (tc, tc), 0)
    c_i = lax.broadcasted_iota(jnp.int32, (tc, tc), 1)
    tri = (c_i <= r_i) if direction == 0 else (c_i >= r_i)
    end = tc - 1 if direction == 0 else 0
    nd = 2 * SSD_HEADS
    cs_t, dt_t, w_t = side[0:nd], side[nd:2 * nd], side[2 * nd:3 * nd]
    dec_tot = jnp.exp(cs[end:end + 1, :])
    cm_f = cm.astype(F32)
    lane = lax.broadcasted_iota(jnp.int32, (1, LANES), 1)
    low_half = lane < SSD_HEAD_DIM
    zeros_h = jnp.zeros((n, LANES), BF16)
    zeros_lhs = jnp.zeros((n, tc), F32)

    for g in range(SSD_GROUPS):
        cb = _dot(cm[:, g * n:(g + 1) * n], bm_t[g * n:(g + 1) * n, :].astype(BF16))
        bt_g = bm_t[g * n:(g + 1) * n, :]
        for pr in range(HEADS_PER_GROUP // 2):
            slab = g * (HEADS_PER_GROUP // 2) + pr
            h_pair = h_ref[direction, g, :, pr * LANES:(pr + 1) * LANES]
            h_b = h_pair.astype(BF16)
            xs_pair = xs[:, slab * LANES:(slab + 1) * LANES]
            rhs = jnp.concatenate([xs_pair] + [h_b if gg == g else zeros_h for gg in range(SSD_GROUPS)], axis=0)
            lhs_rows = []
            cols = []
            for hh in range(2):
                r = direction * SSD_HEADS + 2 * slab + hh
                col = DT_LANE0 + r
                cols.append(col)
                cs_col = jnp.broadcast_to(cs[:, col:col + 1], (tc, tc))
                lm = jnp.exp(jnp.where(tri, cs_col - cs_t[r:r + 1, :], -jnp.inf))
                top = jnp.concatenate([cb * dt_t[r:r + 1, :] * lm, cm_f * jnp.exp(cs_col)], axis=1)
                bot = jnp.concatenate([bt_g * w_t[r:r + 1, :], zeros_lhs], axis=1)
                lhs_rows += [top, bot]
            lhs = jnp.concatenate(lhs_rows, axis=0).astype(BF16)
            out = _dot(lhs, rhs)
            m = tc + n
            y_pair = jnp.where(low_half, out[0:tc], out[m:m + tc])
            if direction == 0:
                y_pair = y_pair + dskip_ref[:, slab * LANES:(slab + 1) * LANES] * xs_pair.astype(F32)
            y_ref[0, row0:row0 + tc, slab * LANES:(slab + 1) * LANES] = y_pair
            h_new = jnp.where(low_half, out[tc:m], out[m + tc:2 * m])
            keep = jnp.where(low_half, dec_tot[:, cols[0]:cols[0] + 1], dec_tot[:, cols[1]:cols[1] + 1])
            h_ref[direction, g, :, pr * LANES:(pr + 1) * LANES] = h_pair * keep + h_new


def _bwd_block(i, n_blocks, ctx_blocks):
    return jnp.where(i < ctx_blocks, ctx_blocks - 1 - i, n_blocks - 1 + ctx_blocks - i)


def _ssd_kernel(af_ref, ab_ref, csf_ref, csb_ref, sidef_ref, sideb_ref, dskip_ref, yf_ref, yb_ref, h_ref):
    @pl.when(pl.program_id(1) == 0)
    def _():
        h_ref[...] = jnp.zeros_like(h_ref)

    dskip = dskip_ref.at[0]
    tc = SSD_CHUNK
    n_sub = af_ref.shape[1] // tc
    for j in range(n_sub):
        r0 = j * tc
        _ssd_role(af_ref[0, r0:r0 + tc, :], csf_ref[0, r0:r0 + tc, :], sidef_ref[0, j], 0, dskip, yf_ref, r0, h_ref)
        jb = n_sub - 1 - j
        r1 = jb * tc
        _ssd_role(ab_ref[0, r1:r1 + tc, :], csb_ref[0, r1:r1 + tc, :], sideb_ref[0, jb], 1, dskip, yb_ref, r1, h_ref)


def _ssd_scan(act, cs, side, layer, dskip, n_ctx):
    bsz, t, _ = act.shape
    tc = SSD_BLOCK
    n_blocks = t // tc
    ctx_blocks = n_ctx // tc
    assert n_ctx % tc == 0 and t % tc == 0

    def fwd(i):
        return i

    def bwd(i):
        return _bwd_block(i, n_blocks, ctx_blocks)

    def main(order, width):
        return pl.BlockSpec((1, tc, width), lambda b, i: (b, order(i), 0))

    def side_spec(order):
        return pl.BlockSpec((1, tc // SSD_CHUNK, SIDE_ROWS, SSD_CHUNK), lambda b, i: (b, order(i), 0, 0))

    def const(shape):
        return pl.BlockSpec((1,) + shape[1:], lambda b, i: (layer,) + (0,) * (len(shape) - 1))

    y_shape = jax.ShapeDtypeStruct((bsz, t, SSD_WIDTH), F32)
    return pl.pallas_call(
        _ssd_kernel,
        out_shape=(y_shape, y_shape),
        grid=(bsz, n_blocks),
        in_specs=[
            main(fwd, SSD_XBC), main(bwd, SSD_XBC), main(fwd, LANES), main(bwd, LANES),
            side_spec(fwd), side_spec(bwd),
            const(dskip.shape),
        ],
        out_specs=(main(fwd, SSD_WIDTH), main(bwd, SSD_WIDTH)),
        scratch_shapes=[pltpu.VMEM((2, SSD_GROUPS, SSD_STATE, HEADS_PER_GROUP * SSD_HEAD_DIM), F32)],
        compiler_params=pltpu.CompilerParams(
            dimension_semantics=("arbitrary", "arbitrary"), vmem_limit_bytes=VMEM_LIMIT_BYTES),
        name="ssd_scan",
    )(act, act, cs, cs, side, side, dskip)


def _mix_gather(att_ref, ysc_ref, z_ref, yf_ref, yb_ref, normg_ref, ycat_ref):
    gated = (yf_ref[...] + yb_ref[...]) * _silu(z_ref[...].astype(F32))
    gw = SSD_WIDTH // SSD_GROUPS
    w = SC_WIDTH
    ycat_ref[:, 0:MLA_WIDTH] = att_ref[...]
    ycat_ref[:, MLA_WIDTH:MLA_WIDTH + w] = ysc_ref[...]
    c0 = MLA_WIDTH + w
    for g in range(SSD_GROUPS):
        gg = gated[:, g * gw:(g + 1) * gw]
        gg = gg * lax.rsqrt(jnp.mean(gg * gg, axis=-1, keepdims=True) + EPS) * normg_ref[:, g * gw:(g + 1) * gw]
        ycat_ref[:, c0 + g * gw:c0 + (g + 1) * gw] = gg.astype(BF16)


def _mix_residual(x, mod_ref, y_ref, gpost_ref, gpre2_ref, x1_ref, h2_ref):
    gate1 = mod_ref[0, 2:3, :]
    shift2 = mod_ref[0, 3:4, :]
    scale2 = mod_ref[0, 4:5, :]
    x1 = x + gate1 * _rms(y_ref[...], gpost_ref[...])
    x1_ref[...] = x1
    h2_ref[...] = (_rms(x1, gpre2_ref[...]) * (1.0 + scale2) + shift2).astype(BF16)


def _mix_mlp(h2_ref, w1_ref, w2_ref, acc_ref):
    d_ff = w1_ref.shape[1]
    for c in range(d_ff // FF_CHUNK):
        a = _dot(h2_ref[...], w1_ref[:, c * FF_CHUNK:(c + 1) * FF_CHUNK])
        r = jnp.square(jnp.maximum(a, 0.0)).astype(BF16)
        part = _dot(r, w2_ref[c * FF_CHUNK:(c + 1) * FF_CHUNK, :])
        if c == 0:
            acc_ref[...] = part
        else:
            acc_ref[...] += part


N_MIX_PARAMS = 7
N_MIX_TOKEN_INPUTS = 6


def _mix_ffn_kernel(*refs, layout, latent_only, split):
    n_x = 2 if split else 1
    per_half = n_x + N_MIX_TOKEN_INPUTS
    n_in = TILES_PER_STEP * per_half
    normg, wout, gpost, gpre2, w1, w2, gpost2 = [r.at[0] for r in refs[n_in:n_in + N_MIX_PARAMS]]
    o_ref = refs[n_in + N_MIX_PARAMS]
    scratch = refs[n_in + N_MIX_PARAMS + 1:]
    per_tile = len(scratch) // TILES_PER_STEP
    tiles = [scratch[h * per_tile:(h + 1) * per_tile] for h in range(TILES_PER_STEP)]
    tm = TOKEN_TILE
    halves = [refs[h * per_half:(h + 1) * per_half] for h in range(TILES_PER_STEP)]
    mods = [half[n_x].at[0] for half in halves]
    for half, (ycat_ref, y_ref, _, _, _) in zip(halves, tiles):
        _mix_gather(*half[n_x + 1:], normg, ycat_ref)
        y_ref[...] = _dot(ycat_ref[...], wout[...])
    for h, (half, (_, y_ref, acc_ref, x1_ref, h2_ref)) in enumerate(zip(halves, tiles)):
        u = layout.unified_tile(pl.program_id(0) * TILES_PER_STEP + h, latent_only)
        x = layout.select_source(u, half[:n_x])
        _mix_residual(x, mods[h], y_ref, gpost, gpre2, x1_ref, h2_ref)
        _mix_mlp(h2_ref, w1, w2, acc_ref)
    for h, (_, _, acc_ref, x1_ref, _) in enumerate(tiles):
        gate2 = mods[h][0, 5:6, :]
        o_ref[h * tm:(h + 1) * tm, :] = x1_ref[...] + gate2 * _rms(acc_ref[...], gpost2[...])


def _mix_ffn(src, mod, layer, att, ysc, z, yf, yb, normg, wout, gpost, gpre2, w1, w2, gpost2, layout,
             latent_only):
    tm = TOKEN_TILE
    d = src[0].shape[-1]
    n_out_tiles = layout.bsz * (layout.lat_tiles if latent_only else layout.tiles_per_batch)
    assert n_out_tiles % TILES_PER_STEP == 0
    att_unified = att.shape[0] == layout.bsz * layout.tiles_per_batch * tm

    in_specs, operands = [], []
    for h in range(TILES_PER_STEP):
        def u_of(j, h=h):
            return layout.unified_tile(j * TILES_PER_STEP + h, latent_only)

        def tok(width, tile_of=u_of):
            return pl.BlockSpec((tm, width), lambda j: (tile_of(j), 0))

        in_specs += layout.source_specs(u_of, d, len(src) == 2)
        in_specs += [
            pl.BlockSpec((1, 1, N_MOD, d), lambda j, u_of=u_of: (layer, layout.mod_row(u_of(j)), 0, 0)),
            tok(MLA_WIDTH) if att_unified else tok(MLA_WIDTH, lambda j, h=h: j * TILES_PER_STEP + h),
            tok(SC_WIDTH), tok(SSD_WIDTH), tok(SSD_WIDTH), tok(SSD_WIDTH),
        ]
        operands += list(src) + [mod, att, ysc, z, yf, yb]
    consts = [normg, wout, gpost, gpre2, w1, w2, gpost2]
    in_specs += [_layer_spec(c, layer, single_buffer=True) for c in consts]
    return pl.pallas_call(
        functools.partial(_mix_ffn_kernel, layout=layout, latent_only=latent_only, split=len(src) == 2),
        out_shape=jax.ShapeDtypeStruct((n_out_tiles * tm, d), F32),
        grid=(n_out_tiles // TILES_PER_STEP,),
        in_specs=in_specs,
        out_specs=pl.BlockSpec((TILES_PER_STEP * tm, d), lambda j: (j, 0)),
        scratch_shapes=[pltpu.VMEM((tm, d), BF16), pltpu.VMEM((tm, d), F32), pltpu.VMEM((tm, d), F32),
                        pltpu.VMEM((tm, d), F32), pltpu.VMEM((tm, d), BF16)] * TILES_PER_STEP,
        compiler_params=pltpu.CompilerParams(
            dimension_semantics=("parallel",), vmem_limit_bytes=VMEM_LIMIT_BYTES),
        name="mix_ffn",
    )(*operands, *consts)


def _rope_tables(n_ctx, seq):
    f32 = np.float32
    half = QK_ROPE // 2
    inv_freq = (f32(ROPE_THETA) ** (-np.arange(0, half, 2, dtype=f32) / f32(half))).astype(f32)
    rows = seq // GRID_W
    row = np.repeat(np.arange(rows, dtype=f32), GRID_W)
    col = np.tile(np.arange(GRID_W, dtype=f32), rows)
    ang_r = row[:, None] * inv_freq
    ang_c = col[:, None] * inv_freq
    ang = np.concatenate([ang_r, ang_r, ang_c, ang_c], axis=-1).astype(f32)
    t = n_ctx + seq
    cos_t = np.ones((t, LANES), f32)
    sin_t = np.zeros((t, LANES), f32)
    cos_t[n_ctx:, ROPE_LANE0:ROPE_LANE0 + QK_ROPE] = np.cos(ang)
    sin_t[n_ctx:, ROPE_LANE0:ROPE_LANE0 + QK_ROPE] = np.sin(ang)
    return jnp.asarray(cos_t), jnp.asarray(sin_t)


def kernel(x, c, ctx, c_ctx, w_mod, b_mod, g_pre_mix, w_in, mla_q_norm, w_uq, mla_kv_norm, w_ukv, sc_conv_w, ssd_conv_w, ssd_conv_b, ssd_a_log, ssd_dt_bias, ssd_d, ssd_norm, w_out, g_post_mix, g_pre_ffn, w_ff1, w_ff2, g_post_ffn):
    bsz, seq, d = x.shape
    n_ctx = ctx.shape[1]
    depth = w_mod.shape[0]
    assert n_ctx == TOKEN_TILE and seq % TOKEN_TILE == 0 and seq % GRID_W == 0
    layout = _TokenLayout(bsz, n_ctx, seq)
    t = n_ctx + seq

    cvec = jnp.zeros((SUBLANES, d), F32).at[:bsz].set(c).at[layout.ctx_row].set(c_ctx)
    mod_all = _modulation(cvec, w_mod, b_mod).reshape(depth, SUBLANES, N_MOD, d)
    cos_t, sin_t = _rope_tables(n_ctx, seq)

    src = (ctx.reshape(bsz * n_ctx, d), x.reshape(bsz * seq, d))

    def rows3(a):
        return a.reshape(depth, 1, -1)

    def dt_lanes(a):
        flat = a.reshape(depth, 1, -1)
        return jnp.pad(flat, ((0, 0), (0, 0), (DT_LANE0, LANES - DT_LANE0 - flat.shape[-1])))

    alog, dtbias = dt_lanes(ssd_a_log), dt_lanes(ssd_dt_bias)
    dskip = rows3(jnp.repeat(ssd_d, SSD_HEAD_DIM, axis=-1))
    w_out_b, w_ff1_b, w_ff2_b = w_out.astype(BF16), w_ff1.astype(BF16), w_ff2.astype(BF16)
    w_in_t = jnp.swapaxes(w_in, 1, 2)

    def per_batch(a):
        return a.reshape((bsz, t) + a.shape[1:])

    for i in range(depth):
        last = i == depth - 1
        qt, k, vt, kn, qn, ysc, z, act, cs, side = _inproj(
            src, mod_all, i, rows3(g_pre_mix), w_in_t, rows3(mla_q_norm), w_uq, rows3(mla_kv_norm), w_ukv,
            cos_t, sin_t, sc_conv_w, ssd_conv_w, rows3(ssd_conv_b), alog, dtbias, layout)
        att = _attention(qt, per_batch(k), vt, kn.reshape(bsz, layout.tiles_per_batch, SUBLANES, LANES), qn,
                         n_ctx, ctx_queries=not last)
        yf, yb = _ssd_scan(per_batch(act), per_batch(cs), side.reshape(bsz, t // SSD_CHUNK, SIDE_ROWS, SSD_CHUNK),
                           i, dskip, n_ctx)
        out = _mix_ffn(src, mod_all, i, att.reshape(-1, MLA_WIDTH), ysc, z,
                       yf.reshape(-1, SSD_WIDTH), yb.reshape(-1, SSD_WIDTH), rows3(ssd_norm),
                       w_out_b, rows3(g_post_mix), rows3(g_pre_ffn), w_ff1_b, w_ff2_b, rows3(g_post_ffn),
                       layout, latent_only=last)
        src = (out,)
    return out.reshape(bsz, seq, d)
```

```python
import functools
import math

import jax
import jax.numpy as jnp
import numpy as np
from jax import lax
from jax.experimental import pallas as pl
from jax.experimental.pallas import tpu as pltpu

F32 = jnp.float32
BF16 = jnp.bfloat16

GRID_W = 64
EPS = 1e-6
N_MOD = 6
MLA_HEADS = 4
Q_LORA = 256
KV_LORA = 128
QK_NOPE = 64
QK_ROPE = 32
V_HEAD = 64
MLA_WIDTH = MLA_HEADS * V_HEAD
MLA_SCALE = (QK_NOPE + QK_ROPE) ** -0.5
ROPE_THETA = 10000.0
SC_WIDTH = 256
SSD_HEADS = 8
SSD_HEAD_DIM = 64
SSD_WIDTH = SSD_HEADS * SSD_HEAD_DIM
SSD_GROUPS = 2
SSD_STATE = 64
SSD_CHUNK = 128
SSD_GN = SSD_GROUPS * SSD_STATE
SSD_XBC = SSD_WIDTH + 2 * SSD_GN
HEADS_PER_GROUP = SSD_HEADS // SSD_GROUPS
IN_MLA = Q_LORA + KV_LORA + QK_ROPE
IN_SC = 3 * SC_WIDTH

LANES = 128
SUBLANES = 8
VMEM_LIMIT_BYTES = 56 * 1024 * 1024

HEAD_PAD = LANES
V_ROWS = V_HEAD + 16
ROPE_LANE0 = QK_NOPE
TOKEN_TILE = 256
TILES_PER_STEP = 2
INPROJ_TILES_PER_STEP = 4
SSD_BLOCK = 2 * SSD_CHUNK
FF_CHUNK = 1024

C_Q0 = 0
C_KV0 = C_Q0 + Q_LORA
C_KR0 = C_KV0 + KV_LORA
C_SC0 = C_KR0 + LANES
C_Z0 = C_SC0 + IN_SC
C_XBC0 = C_Z0 + SSD_WIDTH
D_IN_PAD = C_XBC0 + SSD_XBC
DT_LANE0 = QK_ROPE

Q_PRESCALE = MLA_SCALE * math.log2(math.e)

SHIFT_HEADROOM = 60.0
SHIFT_MAX_BOUND = 90.0
BOUND_SLACK = 1.0 + 2.0 ** -8


def _rms(x, g):
    return x * lax.rsqrt(jnp.mean(x * x, axis=-1, keepdims=True) + EPS) * g


def _silu(x):
    return x / (1.0 + jnp.exp(-x))


def _dot(a, b):
    return jnp.dot(a, b, preferred_element_type=F32)


class _TokenLayout:
    def __init__(self, bsz, n_ctx, seq):
        tm = TOKEN_TILE
        assert n_ctx % tm == 0 and seq % tm == 0
        self.bsz = bsz
        self.ctx_tiles = n_ctx // tm
        self.lat_tiles = seq // tm
        self.tiles_per_batch = self.ctx_tiles + self.lat_tiles
        self.n_tiles = bsz * self.tiles_per_batch
        self.ctx_row = bsz

    def unified_tile(self, o, latent_only):
        if not latent_only:
            return o
        return o + (o // self.lat_tiles + 1) * self.ctx_tiles

    def coords(self, u):
        b = u // self.tiles_per_batch
        return b, u - b * self.tiles_per_batch

    def mod_row(self, u):
        b, i = self.coords(u)
        return jnp.where(i < self.ctx_tiles, self.ctx_row, b)

    def seq_edges(self, u):
        _, i = self.coords(u)
        first = (i == 0) | (i == self.ctx_tiles)
        last = (i == self.ctx_tiles - 1) | (i == self.tiles_per_batch - 1)
        return first, last

    def _latent_tile(self, u):
        b, i = self.coords(u)
        return b * self.lat_tiles + jnp.maximum(i - self.ctx_tiles, 0)

    def _context_tile(self, u):
        b, i = self.coords(u)
        return b * self.ctx_tiles + jnp.minimum(i, self.ctx_tiles - 1)

    def source_specs(self, u_of, d, split):
        tm = TOKEN_TILE
        if not split:
            return [pl.BlockSpec((tm, d), lambda j: (u_of(j), 0))]
        return [pl.BlockSpec((tm, d), lambda j: (self._context_tile(u_of(j)), 0)),
                pl.BlockSpec((tm, d), lambda j: (self._latent_tile(u_of(j)), 0))]

    def halo_specs(self, u_of, d, split):
        rb = TOKEN_TILE // SUBLANES
        if split:
            assert self.ctx_tiles == 1
            tile_of, n_rb = (lambda j: self._latent_tile(u_of(j))), self.bsz * self.lat_tiles * rb
        else:
            tile_of, n_rb = u_of, self.n_tiles * rb
        return [pl.BlockSpec((SUBLANES, d), lambda j: (jnp.maximum(tile_of(j) * rb - 1, 0), 0)),
                pl.BlockSpec((SUBLANES, d), lambda j: (jnp.minimum((tile_of(j) + 1) * rb, n_rb - 1), 0))]

    def select_source(self, u, refs):
        if len(refs) == 1:
            return refs[0][...]
        _, i = self.coords(u)
        return jnp.where(i < self.ctx_tiles, refs[0][...], refs[1][...])


def _mod_kernel(c_ref, w_ref, b_ref, o_ref):
    s = _silu(c_ref[...]).astype(BF16)
    o_ref[0] = _dot(s, w_ref[0].astype(BF16)) + b_ref[0]


def _modulation(cvec, w_mod, b_mod):
    depth, d, nd = w_mod.shape
    rows = cvec.shape[0]
    return pl.pallas_call(
        _mod_kernel,
        out_shape=jax.ShapeDtypeStruct((depth, rows, nd), F32),
        grid=(depth, nd // d),
        in_specs=[
            pl.BlockSpec((rows, d), lambda l, j: (0, 0)),
            pl.BlockSpec((1, d, d), lambda l, j: (l, 0, j)),
            pl.BlockSpec((1, 1, d), lambda l, j: (l, 0, j)),
        ],
        out_specs=pl.BlockSpec((1, rows, d), lambda l, j: (l, 0, j)),
        compiler_params=pltpu.CompilerParams(dimension_semantics=("parallel", "parallel")),
        name="modulation",
    )(cvec, w_mod, b_mod.reshape(depth, 1, nd))


def _conv3_rows(a_ext, w_ref, tm):
    te = a_ext.shape[0]
    lo, hi = SUBLANES, SUBLANES + tm
    prev = pltpu.roll(a_ext, 1, 0)[lo:hi]
    nxt = pltpu.roll(a_ext, te - 1, 0)[lo:hi]
    return w_ref[0:1, :] * prev + w_ref[1:2, :] * a_ext[lo:hi] + w_ref[2:3, :] * nxt


N_INPROJ_PARAMS = 11
N_INPROJ_TOKEN_INPUTS = 5


def _split3(x):
    hi = x.astype(BF16)
    r = x - hi.astype(F32)
    mid = r.astype(BF16)
    lo = (r - mid.astype(F32)).astype(BF16)
    return hi, mid, lo


SIDE_ROWS = 2 * 2 * SSD_HEADS


def _ssd_decay_tables(dt_raw, alog_ref, dtbias_ref, cs_ref, side_ref, slot):
    tc = SSD_CHUNK
    tm = dt_raw.shape[0]
    nd = 2 * SSD_HEADS
    a = -jnp.exp(alog_ref[...])
    xb = dt_raw + dtbias_ref[...]
    dt = jnp.maximum(xb, 0.0) + jnp.log(1.0 + jnp.exp(-jnp.abs(xb)))
    la = dt * a
    r_i = lax.broadcasted_iota(jnp.int32, (tc, tc), 0)
    c_i = lax.broadcasted_iota(jnp.int32, (tc, tc), 1)
    tri_f = jnp.where(c_i <= r_i, 1.0, 0.0).astype(BF16)
    tri_b = jnp.where(c_i >= r_i, 1.0, 0.0).astype(BF16)
    lane = lax.broadcasted_iota(jnp.int32, (1, LANES), 1)
    fwd_lane = lane < DT_LANE0 + SSD_HEADS
    row = lax.broadcasted_iota(jnp.int32, (nd, 1), 0)
    for c in range(tm // tc):
        parts = _split3(la[c * tc:(c + 1) * tc])
        cs_f = sum(_dot(tri_f, p) for p in parts)
        cs_b = sum(_dot(tri_b, p) for p in parts)
        cs = jnp.where(fwd_lane, cs_f, cs_b)
        cs_ref[slot * tm + c * tc:slot * tm + (c + 1) * tc, :] = cs
        cs_t = cs.T[DT_LANE0:DT_LANE0 + nd]
        dt_t = dt[c * tc:(c + 1) * tc].T[DT_LANE0:DT_LANE0 + nd]
        cs_end = jnp.where(row < SSD_HEADS, cs_t[:, tc - 1:tc], cs_t[:, 0:1])
        w_t = jnp.exp(cs_end - cs_t) * dt_t
        side_ref[slot * (tm // tc) + c] = jnp.concatenate([cs_t - jnp.log(dt_t), w_t], axis=0)


def _rotate_cols(w):
    a, b, c, d = jnp.split(w, 4, axis=-1)
    return jnp.concatenate([-b, a, -d, c], axis=-1)


def _relayout_weights(win_raw, wuq_raw, wukv_raw, win_ref, wuq_ref, wukv_ref):
    d = win_raw.shape[1]
    w_kr = win_raw[Q_LORA + KV_LORA:IN_MLA, :]
    w_dt = win_raw[IN_MLA + IN_SC + SSD_WIDTH + SSD_XBC:, :]
    a, b, c, e = (w_kr[i * SUBLANES:(i + 1) * SUBLANES] for i in range(4))
    kr_block = jnp.concatenate([
        -b, a, -e, c,
        w_dt, jnp.zeros((LANES // 2 - QK_ROPE - w_dt.shape[0], d), F32),
        w_kr, jnp.zeros((LANES // 2 - QK_ROPE, d), F32)], axis=0)
    win_ref[:, C_KR0:C_SC0] = kr_block.T.astype(BF16)
    step = 2 * LANES
    for src0, dst0, n_rows in ((0, C_Q0, Q_LORA + KV_LORA), (IN_MLA, C_SC0, IN_SC + SSD_WIDTH + SSD_XBC)):
        for off in range(0, n_rows, step):
            rows = min(step, n_rows - off)
            blk = win_raw[src0 + off:src0 + off + rows, :]
            win_ref[:, dst0 + off:dst0 + off + rows] = blk.T.astype(BF16)

    dqk = QK_NOPE + QK_ROPE
    wq = wuq_raw[...]
    nq = wq.shape[0]
    plain, rot = [], []
    for h in range(MLA_HEADS):
        wh = wq[:, h * dqk:(h + 1) * dqk]
        pad = jnp.zeros((nq, HEAD_PAD - dqk), F32)
        plain += [wh, pad]
        rot += [jnp.zeros((nq, QK_NOPE), F32), _rotate_cols(wh[:, QK_NOPE:]), pad]
    wuq_ref[...] = jnp.concatenate(plain + rot, axis=1).astype(BF16)

    dkv = QK_NOPE + V_HEAD
    wkv = wukv_raw[...]
    nkv = wkv.shape[0]
    ks, vs = [], []
    for h in range(MLA_HEADS):
        wh = wkv[:, h * dkv:(h + 1) * dkv]
        ks += [wh[:, :QK_NOPE], jnp.zeros((nkv, HEAD_PAD - QK_NOPE), F32)]
        vs.append(wh[:, QK_NOPE:])
    wukv_ref[...] = jnp.concatenate(ks + vs, axis=1).astype(BF16)


def _inproj_kernel(*refs, layout, split, tiles):
    n_x = 2 if split else 1
    per_half = n_x + N_INPROJ_TOKEN_INPUTS
    n_in = tiles * per_half
    (gpre, win_raw, qg, wuq_raw, kvg, wukv_raw, scw, cw, cb, alog, dtbias) = [
        r.at[0] for r in refs[n_in:n_in + N_INPROJ_PARAMS]]
    n_out = len(refs) - n_in - N_INPROJ_PARAMS - 3
    outs = refs[n_in + N_INPROJ_PARAMS:n_in + N_INPROJ_PARAMS + n_out]
    win_ref, wuq_ref, wukv_ref = refs[n_in + N_INPROJ_PARAMS + n_out:]

    @pl.when(pl.program_id(0) == 0)
    def _():
        _relayout_weights(win_raw, wuq_raw, wukv_raw, win_ref, wuq_ref, wukv_ref)

    params = (qg, wuq_ref, kvg, wukv_ref, scw, cw, cb, alog, dtbias)
    for h in range(tiles):
        half = refs[h * per_half:(h + 1) * per_half]
        u = pl.program_id(0) * tiles + h
        x = layout.select_source(u, half[:n_x])
        xp_ref, xn_ref, mod_ref, cos_ref, sin_ref = half[n_x:]
        h_ext = _inproj_normalise(x, xp_ref, xn_ref, mod_ref.at[0], gpre, layout.seq_edges(u))
        u_ext = _dot(h_ext, win_ref[...])
        _inproj_finish(u_ext, cos_ref, sin_ref, params, outs, h)


def _inproj_normalise(x, xp_ref, xn_ref, mod_ref, gpre_ref, seq_edges):
    first_of_seq, last_of_seq = seq_edges
    shift = mod_ref[0, 0:1, :]
    scale = mod_ref[0, 1:2, :]

    def norm_mod(xx):
        return _rms(xx, gpre_ref[...]) * (1.0 + scale) + shift

    h_prev = jnp.where(first_of_seq, 0.0, norm_mod(xp_ref[...]))
    h_next = jnp.where(last_of_seq, 0.0, norm_mod(xn_ref[...]))
    return jnp.concatenate([h_prev, norm_mod(x), h_next], axis=0).astype(BF16)


def _inproj_finish(u_ext, cos_ref, sin_ref, params, outs, slot):
    qg_ref, wuq_ref, kvg_ref, wukv_ref, scw_ref, cw_ref, cb_ref, alog_ref, dtbias_ref = params
    qt_ref, k_ref, vt_ref, kn_ref, qn_ref, ysc_ref, z_ref, act_ref, cs_ref, side_ref = outs
    tm = u_ext.shape[0] - 2 * SUBLANES
    rows = slice(slot * tm, (slot + 1) * tm)
    own = slice(SUBLANES, SUBLANES + tm)

    z_ref[rows, :] = u_ext[own, C_Z0:C_XBC0].astype(z_ref.dtype)
    lane = lax.broadcasted_iota(jnp.int32, (1, LANES), 1)
    krb = u_ext[own, C_KR0:C_SC0]
    dt_raw = jnp.where((lane >= DT_LANE0) & (lane < DT_LANE0 + 2 * SSD_HEADS), krb, 0.0)
    _ssd_decay_tables(dt_raw, alog_ref, dtbias_ref, cs_ref, side_ref, slot)

    conv = _conv3_rows(u_ext[:, C_XBC0:D_IN_PAD], cw_ref, tm) + cb_ref[...]
    act_ref[rows, :] = _silu(conv).astype(act_ref.dtype)

    w = SC_WIDTH
    prod = u_ext[:, C_SC0 + w:C_SC0 + 2 * w] * u_ext[:, C_SC0 + 2 * w:C_Z0]
    ysc_ref[rows, :] = (u_ext[own, C_SC0:C_SC0 + w] * _conv3_rows(prod, scw_ref, tm)).astype(ysc_ref.dtype)

    cos = cos_ref[...]
    sin = sin_ref[...]
    nh = MLA_HEADS
    cos_h = jnp.concatenate([cos] * nh, axis=1)
    sin_h = jnp.concatenate([sin] * nh, axis=1)

    def head_norm_maxima(a_f32):
        out = []
        for h in range(nh):
            a_h = a_f32[:, h * HEAD_PAD:(h + 1) * HEAD_PAD]
            n2 = jnp.max(jnp.sum(a_h * a_h, axis=1, keepdims=True), axis=0, keepdims=True)
            out.append(jnp.broadcast_to(n2, (1, LANES)))
        return jnp.concatenate(out + [jnp.zeros((SUBLANES - nh, LANES), F32)], axis=0)

    cq = _rms(u_ext[own, C_Q0:C_KV0], qg_ref[...]).astype(BF16)
    q2 = _dot(cq, wuq_ref[...])
    qw = nh * HEAD_PAD
    q_b = ((q2[:, :qw] * cos_h + q2[:, qw:] * sin_h) * Q_PRESCALE).astype(BF16)
    q_f = q_b.astype(F32)
    qt_ref[:, rows] = q_f.T.astype(BF16)
    qn_ref[slot] = head_norm_maxima(q_f)

    ckv = _rms(u_ext[own, C_KV0:C_KR0], kvg_ref[...]).astype(BF16)
    kv = _dot(ckv, wukv_ref[...])
    rope_lane = (lane >= ROPE_LANE0) & (lane < ROPE_LANE0 + QK_ROPE)
    cos_k = jnp.where(rope_lane, cos, 0.0)
    kr = krb * cos_k + pltpu.roll(krb, LANES // 2, 1) * sin
    k = kv[:, :qw] + jnp.concatenate([kr] * nh, axis=1)
    k_b = k.astype(BF16)
    k_ref[rows, :] = k_b
    kn_ref[slot] = head_norm_maxima(k_b.astype(F32))
    v_t = kv[:, qw:].T.astype(BF16)
    for h in range(nh):
        vt_ref[h, 0:V_HEAD, rows] = v_t[h * V_HEAD:(h + 1) * V_HEAD]
        vt_ref[h, V_HEAD:V_ROWS, rows] = jnp.ones((V_ROWS - V_HEAD, tm), BF16)


def _layer_spec(a, layer, single_buffer=False):
    block = (1,) + a.shape[1:]
    index_map = lambda j: (layer,) + (0,) * (a.ndim - 1)
    if single_buffer:
        return pl.BlockSpec(block, index_map, pipeline_mode=pl.Buffered(1))
    return pl.BlockSpec(block, index_map)


def _inproj(src, mod, layer, gpre, win, qg, wuq, kvg, wukv, cos_t, sin_t, scw, cw, cb, alog, dtbias, layout):
    tm = TOKEN_TILE
    d = src[0].shape[-1]
    split = len(src) == 2
    qw = MLA_HEADS * HEAD_PAD
    nt = layout.n_tiles
    tiles = INPROJ_TILES_PER_STEP
    assert nt % tiles == 0
    rows = nt * tm

    in_specs, operands = [], []
    for h in range(tiles):
        def u_of(j, h=h):
            return j * tiles + h

        def rope(j, u_of=u_of):
            return (layout.coords(u_of(j))[1], 0)

        in_specs += layout.source_specs(u_of, d, split) + layout.halo_specs(u_of, d, split)
        in_specs += [
            pl.BlockSpec((1, 1, N_MOD, d), lambda j, u_of=u_of: (layer, layout.mod_row(u_of(j)), 0, 0)),
            pl.BlockSpec((tm, LANES), rope), pl.BlockSpec((tm, LANES), rope),
        ]
        operands += list(src) + [src[-1], src[-1], mod, cos_t, sin_t]
    consts = [gpre, win, qg, wuq, kvg, wukv, scw, cw, cb, alog, dtbias]
    in_specs += [_layer_spec(c, layer, single_buffer=c is win) for c in consts]
    step_rows = tiles * tm
    chunks_per_step = step_rows // SSD_CHUNK

    def tok(width):
        return pl.BlockSpec((step_rows, width), lambda j: (j, 0))

    norm_shape = jax.ShapeDtypeStruct((nt, SUBLANES, LANES), F32)
    norm_spec = pl.BlockSpec((tiles, SUBLANES, LANES), lambda j: (j, 0, 0))
    out_shape = (
        jax.ShapeDtypeStruct((qw, rows), BF16),
        jax.ShapeDtypeStruct((rows, qw), BF16),
        jax.ShapeDtypeStruct((MLA_HEADS, V_ROWS, rows), BF16),
        norm_shape,
        norm_shape,
        jax.ShapeDtypeStruct((rows, SC_WIDTH), BF16),
        jax.ShapeDtypeStruct((rows, SSD_WIDTH), BF16),
        jax.ShapeDtypeStruct((rows, SSD_XBC), BF16),
        jax.ShapeDtypeStruct((rows, LANES), F32),
        jax.ShapeDtypeStruct((rows // SSD_CHUNK, SIDE_ROWS, SSD_CHUNK), F32),
    )
    return pl.pallas_call(
        functools.partial(_inproj_kernel, layout=layout, split=split, tiles=tiles),
        out_shape=out_shape,
        grid=(nt // tiles,),
        in_specs=in_specs,
        out_specs=(pl.BlockSpec((qw, step_rows), lambda j: (0, j)), tok(qw),
                   pl.BlockSpec((MLA_HEADS, V_ROWS, step_rows), lambda j: (0, 0, j)),
                   norm_spec, norm_spec,
                   tok(SC_WIDTH), tok(SSD_WIDTH), tok(SSD_XBC), tok(LANES),
                   pl.BlockSpec((chunks_per_step, SIDE_ROWS, SSD_CHUNK), lambda j: (j, 0, 0))),
        scratch_shapes=[pltpu.VMEM((d, D_IN_PAD), BF16),
                        pltpu.VMEM((Q_LORA, 2 * qw), BF16),
                        pltpu.VMEM((KV_LORA, qw + MLA_WIDTH), BF16)],
        compiler_params=pltpu.CompilerParams(
            dimension_semantics=("arbitrary",), vmem_limit_bytes=VMEM_LIMIT_BYTES),
        name="inproj",
    )(*operands, *consts)


def _attn_kernel(*refs, n_keys, n_ctx, ctx_queries, q_tiles):
    qt_refs = refs[:q_tiles]
    k_ref, vt_ref, kn_ref = refs[q_tiles:q_tiles + 3]
    qn_refs = refs[q_tiles + 3:2 * q_tiles + 3]
    o_ref, s_ref, p_ref = refs[2 * q_tiles + 3:]
    i = pl.program_id(1)
    kn = jnp.max(kn_ref[0], axis=0)
    for t in range(q_tiles):
        _attend_tile(qt_refs[t], qn_refs[t], k_ref, vt_ref, kn, o_ref, t, s_ref, p_ref, i,
                     n_keys=n_keys, n_ctx=n_ctx, ctx_queries=ctx_queries)


def _attend_tile(q_t, qn_ref, k_ref, vt_ref, kn, o_ref, slot, s_ref, p_ref, i, *, n_keys, n_ctx, ctx_queries):
    tq = q_t.shape[1]

    def finish(outs):
        o_ref[0, slot * tq:(slot + 1) * tq, :] = jnp.concatenate(outs, axis=0).T.astype(o_ref.dtype)

    def head_out(ov):
        return ov[0:V_HEAD] / ov[V_HEAD:V_HEAD + 1]

    def attend_two_pass(nk):
        def scores(h):
            s_ref[h, 0:nk, :] = _dot(k_ref[0, 0:nk, h * HEAD_PAD:(h + 1) * HEAD_PAD],
                                     q_t[h * HEAD_PAD:(h + 1) * HEAD_PAD, :])

        def probs(h):
            s = s_ref[h, 0:nk, :]
            p_ref[h, 0:nk, :] = jnp.exp2(s - jnp.max(s, axis=0, keepdims=True)).astype(BF16)

        def values(h):
            return head_out(_dot(vt_ref[h, :, 0:nk], p_ref[h, 0:nk, :]))

        scores(0), scores(1)
        scores(2), scores(3), probs(0), probs(1)
        outs = [values(0), values(1)]
        probs(2), probs(3)
        outs += [values(2), values(3)]
        finish(outs)

    def attend_one_pass(nk, shift):
        outs = []
        for pair in range(MLA_HEADS // 2):
            heads = (2 * pair, 2 * pair + 1)
            s = [_dot(k_ref[0, 0:nk, h * HEAD_PAD:(h + 1) * HEAD_PAD],
                      q_t[h * HEAD_PAD:(h + 1) * HEAD_PAD, :]) for h in heads]
            p = [jnp.exp2(sh - shift[h]).astype(BF16) for h, sh in zip(heads, s)]
            outs += [head_out(_dot(vt_ref[h, :, 0:nk], ph)) for h, ph in zip(heads, p)]
        finish(outs)

    def attend(nk):
        bound = jnp.sqrt(qn_ref[0] * kn) * BOUND_SLACK
        one_pass = jnp.max(bound) <= SHIFT_MAX_BOUND

        @pl.when(one_pass)
        def _():
            attend_one_pass(nk, [bound[h:h + 1, 0:1] - SHIFT_HEADROOM for h in range(MLA_HEADS)])

        @pl.when(jnp.logical_not(one_pass))
        def _():
            attend_two_pass(nk)

    if ctx_queries:
        @pl.when(i == 0)
        def _():
            attend_two_pass(n_ctx)

        @pl.when(i > 0)
        def _():
            attend(n_keys)
    else:
        attend(n_keys)


def _attention(qt, k, vt, kn, qn, n_ctx, ctx_queries):
    bsz, t, qw = k.shape
    tq = TOKEN_TILE
    tiles = t // tq
    t0 = 0 if ctx_queries else n_ctx // tq
    nq = tiles - t0
    q_tiles = 1 if ctx_queries else 2
    assert nq % q_tiles == 0

    def tile_of(b, i, s):
        return b * tiles + i * q_tiles + s + t0

    return pl.pallas_call(
        functools.partial(_attn_kernel, n_keys=t, n_ctx=n_ctx, ctx_queries=ctx_queries, q_tiles=q_tiles),
        out_shape=jax.ShapeDtypeStruct((bsz, nq * tq, MLA_WIDTH), BF16),
        grid=(bsz, nq // q_tiles),
        in_specs=[pl.BlockSpec((qw, tq), lambda b, i, s=s: (0, tile_of(b, i, s))) for s in range(q_tiles)] + [
            pl.BlockSpec((1, t, qw), lambda b, i: (b, 0, 0)),
            pl.BlockSpec((MLA_HEADS, V_ROWS, t), lambda b, i: (0, 0, b)),
            pl.BlockSpec((1,) + kn.shape[1:], lambda b, i: (b, 0, 0, 0)),
        ] + [pl.BlockSpec((1, SUBLANES, LANES), lambda b, i, s=s: (tile_of(b, i, s), 0, 0)) for s in range(q_tiles)],
        out_specs=pl.BlockSpec((1, q_tiles * tq, MLA_WIDTH), lambda b, i: (b, i, 0)),
        scratch_shapes=[pltpu.VMEM((MLA_HEADS, t, tq), F32), pltpu.VMEM((MLA_HEADS, t, tq), BF16)],
        compiler_params=pltpu.CompilerParams(
            dimension_semantics=("parallel", "parallel"), vmem_limit_bytes=VMEM_LIMIT_BYTES),
        name="attention",
    )(*([qt] * q_tiles), k, vt, kn, *([qn] * q_tiles))


def _ssd_role(act, cs, side, direction, dskip_ref, y_ref, row0, h_ref):
    tc = SSD_CHUNK
    n = SSD_STATE
    xs = act[:, :SSD_WIDTH]
    bm = act[:, SSD_WIDTH:SSD_WIDTH + SSD_GN].astype(F32)
    cm = act[:, SSD_WIDTH + SSD_GN:]
    bm_t = bm.T

    r_i = lax.broadcasted_iota(jnp.int32, (tc, tc), 0)
    c_i = lax.broadcasted_iota(jnp.int32, (tc, tc), 1)
    tri = (c_i <= r_i) if direction == 0 else (c_i >= r_i)
    end = tc - 1 if direction == 0 else 0
    nd = 2 * SSD_HEADS
    csd_t, w_t = side[0:nd], side[nd:2 * nd]
    dec_tot = jnp.exp(cs[end:end + 1, :])
    cm_f = cm.astype(F32)
    lane = lax.broadcasted_iota(jnp.int32, (1, LANES), 1)
    low_half = lane < SSD_HEAD_DIM
    zeros_h = jnp.zeros((n, LANES), BF16)
    zeros_lhs = jnp.zeros((n, tc), BF16)

    for g in range(SSD_GROUPS):
        cb = _dot(cm[:, g * n:(g + 1) * n], bm_t[g * n:(g + 1) * n, :].astype(BF16))
        bt_g = bm_t[g * n:(g + 1) * n, :]
        for pr in range(HEADS_PER_GROUP // 2):
            slab = g * (HEADS_PER_GROUP // 2) + pr
            h_pair = h_ref[direction, g, :, pr * LANES:(pr + 1) * LANES]
            h_b = h_pair.astype(BF16)
            xs_pair = xs[:, slab * LANES:(slab + 1) * LANES]
            rhs = jnp.concatenate([xs_pair] + [h_b if gg == g else zeros_h for gg in range(SSD_GROUPS)], axis=0)
            lhs_rows = []
            cols = []
            for hh in range(2):
                r = direction * SSD_HEADS + 2 * slab + hh
                col = DT_LANE0 + r
                cols.append(col)
                cs_col = jnp.broadcast_to(cs[:, col:col + 1], (tc, tc))
                lmd = jnp.exp(jnp.where(tri, cs_col - csd_t[r:r + 1, :], -jnp.inf))
                top = jnp.concatenate([(cb * lmd).astype(BF16), (cm_f * jnp.exp(cs_col)).astype(BF16)], axis=1)
                bot = jnp.concatenate([(bt_g * w_t[r:r + 1, :]).astype(BF16), zeros_lhs], axis=1)
                lhs_rows += [top, bot]
            lhs = jnp.concatenate(lhs_rows, axis=0)
            out = _dot(lhs, rhs)
            m = tc + n
            y_pair = jnp.where(low_half, out[0:tc], out[m:m + tc])
            if direction == 0:
                y_pair = y_pair + dskip_ref[:, slab * LANES:(slab + 1) * LANES] * xs_pair.astype(F32)
            y_ref[0, row0:row0 + tc, slab * LANES:(slab + 1) * LANES] = y_pair
            h_new = jnp.where(low_half, out[tc:m], out[m + tc:2 * m])
            keep = jnp.where(low_half, dec_tot[:, cols[0]:cols[0] + 1], dec_tot[:, cols[1]:cols[1] + 1])
            h_ref[direction, g, :, pr * LANES:(pr + 1) * LANES] = h_pair * keep + h_new


def _bwd_block(i, n_blocks, ctx_blocks):
    return jnp.where(i < ctx_blocks, ctx_blocks - 1 - i, n_blocks - 1 + ctx_blocks - i)


def _ssd_kernel(af_ref, ab_ref, csf_ref, csb_ref, sidef_ref, sideb_ref, dskip_ref, yf_ref, yb_ref, h_ref):
    @pl.when(pl.program_id(1) == 0)
    def _():
        h_ref[...] = jnp.zeros_like(h_ref)

    dskip = dskip_ref.at[0]
    tc = SSD_CHUNK
    n_sub = af_ref.shape[1] // tc
    for j in range(n_sub):
        r0 = j * tc
        _ssd_role(af_ref[0, r0:r0 + tc, :], csf_ref[0, r0:r0 + tc, :], sidef_ref[0, j], 0, dskip, yf_ref, r0, h_ref)
        jb = n_sub - 1 - j
        r1 = jb * tc
        _ssd_role(ab_ref[0, r1:r1 + tc, :], csb_ref[0, r1:r1 + tc, :], sideb_ref[0, jb], 1, dskip, yb_ref, r1, h_ref)


def _ssd_scan(act, cs, side, layer, dskip, n_ctx):
    bsz, t, _ = act.shape
    tc = SSD_BLOCK
    n_blocks = t // tc
    ctx_blocks = n_ctx // tc
    assert n_ctx % tc == 0 and t % tc == 0

    def fwd(i):
        return i

    def bwd(i):
        return _bwd_block(i, n_blocks, ctx_blocks)

    def main(order, width):
        return pl.BlockSpec((1, tc, width), lambda b, i: (b, order(i), 0))

    def side_spec(order):
        return pl.BlockSpec((1, tc // SSD_CHUNK, SIDE_ROWS, SSD_CHUNK), lambda b, i: (b, order(i), 0, 0))

    def const(shape):
        return pl.BlockSpec((1,) + shape[1:], lambda b, i: (layer,) + (0,) * (len(shape) - 1))

    y_shape = jax.ShapeDtypeStruct((bsz, t, SSD_WIDTH), F32)
    return pl.pallas_call(
        _ssd_kernel,
        out_shape=(y_shape, y_shape),
        grid=(bsz, n_blocks),
        in_specs=[
            main(fwd, SSD_XBC), main(bwd, SSD_XBC), main(fwd, LANES), main(bwd, LANES),
            side_spec(fwd), side_spec(bwd),
            const(dskip.shape),
        ],
        out_specs=(main(fwd, SSD_WIDTH), main(bwd, SSD_WIDTH)),
        scratch_shapes=[pltpu.VMEM((2, SSD_GROUPS, SSD_STATE, HEADS_PER_GROUP * SSD_HEAD_DIM), F32)],
        compiler_params=pltpu.CompilerParams(
            dimension_semantics=("arbitrary", "arbitrary"), vmem_limit_bytes=VMEM_LIMIT_BYTES),
        name="ssd_scan",
    )(act, act, cs, cs, side, side, dskip)


def _mix_gather(att_ref, ysc_ref, z_ref, yf_ref, yb_ref, normg_ref, ycat_ref):
    gated = (yf_ref[...] + yb_ref[...]) * _silu(z_ref[...].astype(F32))
    gw = SSD_WIDTH // SSD_GROUPS
    w = SC_WIDTH
    ycat_ref[:, 0:MLA_WIDTH] = att_ref[...]
    ycat_ref[:, MLA_WIDTH:MLA_WIDTH + w] = ysc_ref[...]
    c0 = MLA_WIDTH + w
    for g in range(SSD_GROUPS):
        gg = gated[:, g * gw:(g + 1) * gw]
        gg = gg * lax.rsqrt(jnp.mean(gg * gg, axis=-1, keepdims=True) + EPS) * normg_ref[:, g * gw:(g + 1) * gw]
        ycat_ref[:, c0 + g * gw:c0 + (g + 1) * gw] = gg.astype(BF16)


def _mix_residual(x, mod_ref, y_ref, gpost_ref, gpre2_ref, x1_ref, h2_ref):
    gate1 = mod_ref[0, 2:3, :]
    shift2 = mod_ref[0, 3:4, :]
    scale2 = mod_ref[0, 4:5, :]
    x1 = x + gate1 * _rms(y_ref[...], gpost_ref[...])
    x1_ref[...] = x1
    h2_ref[...] = (_rms(x1, gpre2_ref[...]) * (1.0 + scale2) + shift2).astype(BF16)


def _mix_mlp(h2_ref, w1_ref, w2_ref, acc_ref):
    d_ff = w1_ref.shape[1]
    for c in range(d_ff // FF_CHUNK):
        a = _dot(h2_ref[...], w1_ref[:, c * FF_CHUNK:(c + 1) * FF_CHUNK])
        r = jnp.square(jnp.maximum(a, 0.0)).astype(BF16)
        part = _dot(r, w2_ref[c * FF_CHUNK:(c + 1) * FF_CHUNK, :])
        if c == 0:
            acc_ref[...] = part
        else:
            acc_ref[...] += part


N_MIX_PARAMS = 7
N_MIX_TOKEN_INPUTS = 6


def _mix_ffn_kernel(*refs, layout, latent_only, split):
    n_x = 2 if split else 1
    per_half = n_x + N_MIX_TOKEN_INPUTS
    n_in = TILES_PER_STEP * per_half
    normg, wout, gpost, gpre2, w1, w2, gpost2 = [r.at[0] for r in refs[n_in:n_in + N_MIX_PARAMS]]
    o_ref = refs[n_in + N_MIX_PARAMS]
    scratch = refs[n_in + N_MIX_PARAMS + 1:]
    per_tile = len(scratch) // TILES_PER_STEP
    tiles = [scratch[h * per_tile:(h + 1) * per_tile] for h in range(TILES_PER_STEP)]
    tm = TOKEN_TILE
    halves = [refs[h * per_half:(h + 1) * per_half] for h in range(TILES_PER_STEP)]
    mods = [half[n_x].at[0] for half in halves]
    for half, (ycat_ref, y_ref, _, _, _) in zip(halves, tiles):
        _mix_gather(*half[n_x + 1:], normg, ycat_ref)
        y_ref[...] = _dot(ycat_ref[...], wout[...])
    for h, (half, (_, y_ref, acc_ref, x1_ref, h2_ref)) in enumerate(zip(halves, tiles)):
        u = layout.unified_tile(pl.program_id(0) * TILES_PER_STEP + h, latent_only)
        x = layout.select_source(u, half[:n_x])
        _mix_residual(x, mods[h], y_ref, gpost, gpre2, x1_ref, h2_ref)
        _mix_mlp(h2_ref, w1, w2, acc_ref)
    for h, (_, _, acc_ref, x1_ref, _) in enumerate(tiles):
        gate2 = mods[h][0, 5:6, :]
        o_ref[h * tm:(h + 1) * tm, :] = x1_ref[...] + gate2 * _rms(acc_ref[...], gpost2[...])


def _mix_ffn(src, mod, layer, att, ysc, z, yf, yb, normg, wout, gpost, gpre2, w1, w2, gpost2, layout,
             latent_only):
    tm = TOKEN_TILE
    d = src[0].shape[-1]
    n_out_tiles = layout.bsz * (layout.lat_tiles if latent_only else layout.tiles_per_batch)
    assert n_out_tiles % TILES_PER_STEP == 0
    att_unified = att.shape[0] == layout.bsz * layout.tiles_per_batch * tm

    in_specs, operands = [], []
    for h in range(TILES_PER_STEP):
        def u_of(j, h=h):
            return layout.unified_tile(j * TILES_PER_STEP + h, latent_only)

        def tok(width, tile_of=u_of):
            return pl.BlockSpec((tm, width), lambda j: (tile_of(j), 0))

        in_specs += layout.source_specs(u_of, d, len(src) == 2)
        in_specs += [
            pl.BlockSpec((1, 1, N_MOD, d), lambda j, u_of=u_of: (layer, layout.mod_row(u_of(j)), 0, 0)),
            tok(MLA_WIDTH) if att_unified else tok(MLA_WIDTH, lambda j, h=h: j * TILES_PER_STEP + h),
            tok(SC_WIDTH), tok(SSD_WIDTH), tok(SSD_WIDTH), tok(SSD_WIDTH),
        ]
        operands += list(src) + [mod, att, ysc, z, yf, yb]
    consts = [normg, wout, gpost, gpre2, w1, w2, gpost2]
    in_specs += [_layer_spec(c, layer, single_buffer=True) for c in consts]
    return pl.pallas_call(
        functools.partial(_mix_ffn_kernel, layout=layout, latent_only=latent_only, split=len(src) == 2),
        out_shape=jax.ShapeDtypeStruct((n_out_tiles * tm, d), F32),
        grid=(n_out_tiles // TILES_PER_STEP,),
        in_specs=in_specs,
        out_specs=pl.BlockSpec((TILES_PER_STEP * tm, d), lambda j: (j, 0)),
        scratch_shapes=[pltpu.VMEM((tm, d), BF16), pltpu.VMEM((tm, d), F32), pltpu.VMEM((tm, d), F32),
                        pltpu.VMEM((tm, d), F32), pltpu.VMEM((tm, d), BF16)] * TILES_PER_STEP,
        compiler_params=pltpu.CompilerParams(
            dimension_semantics=("parallel",), vmem_limit_bytes=VMEM_LIMIT_BYTES),
        name="mix_ffn",
    )(*operands, *consts)


def _rope_tables(n_ctx, seq):
    f32 = np.float32
    half = QK_ROPE // 2
    inv_freq = (f32(ROPE_THETA) ** (-np.arange(0, half, 2, dtype=f32) / f32(half))).astype(f32)
    rows = seq // GRID_W
    row = np.repeat(np.arange(rows, dtype=f32), GRID_W)
    col = np.tile(np.arange(GRID_W, dtype=f32), rows)
    ang_r = row[:, None] * inv_freq
    ang_c = col[:, None] * inv_freq
    ang = np.concatenate([ang_r, ang_r, ang_c, ang_c], axis=-1).astype(f32)
    t = n_ctx + seq
    cos_t = np.ones((t, LANES), f32)
    sin_t = np.zeros((t, LANES), f32)
    cos_t[n_ctx:, ROPE_LANE0:ROPE_LANE0 + QK_ROPE] = np.cos(ang)
    sin_t[n_ctx:, ROPE_LANE0:ROPE_LANE0 + QK_ROPE] = np.sin(ang)
    return jnp.asarray(cos_t), jnp.asarray(sin_t)


def kernel(x, c, ctx, c_ctx, w_mod, b_mod, g_pre_mix, w_in, mla_q_norm, w_uq, mla_kv_norm, w_ukv, sc_conv_w, ssd_conv_w, ssd_conv_b, ssd_a_log, ssd_dt_bias, ssd_d, ssd_norm, w_out, g_post_mix, g_pre_ffn, w_ff1, w_ff2, g_post_ffn):
    bsz, seq, d = x.shape
    n_ctx = ctx.shape[1]
    depth = w_mod.shape[0]
    assert n_ctx == TOKEN_TILE and seq % TOKEN_TILE == 0 and seq % GRID_W == 0
    layout = _TokenLayout(bsz, n_ctx, seq)
    t = n_ctx + seq

    cvec = jnp.zeros((SUBLANES, d), F32).at[:bsz].set(c).at[layout.ctx_row].set(c_ctx)
    mod_all = _modulation(cvec, w_mod, b_mod).reshape(depth, SUBLANES, N_MOD, d)
    cos_t, sin_t = _rope_tables(n_ctx, seq)

    src = (ctx.reshape(bsz * n_ctx, d), x.reshape(bsz * seq, d))

    def rows3(a):
        return a.reshape(depth, 1, -1)

    def dt_lanes(a):
        flat = a.reshape(depth, 1, -1)
        return jnp.pad(flat, ((0, 0), (0, 0), (DT_LANE0, LANES - DT_LANE0 - flat.shape[-1])))

    alog, dtbias = dt_lanes(ssd_a_log), dt_lanes(ssd_dt_bias)
    dskip = rows3(jnp.repeat(ssd_d, SSD_HEAD_DIM, axis=-1))
    w_out_b, w_ff1_b, w_ff2_b = w_out.astype(BF16), w_ff1.astype(BF16), w_ff2.astype(BF16)
    w_in_t = jnp.swapaxes(w_in, 1, 2)

    def per_batch(a):
        return a.reshape((bsz, t) + a.shape[1:])

    for i in range(depth):
        last = i == depth - 1
        qt, k, vt, kn, qn, ysc, z, act, cs, side = _inproj(
            src, mod_all, i, rows3(g_pre_mix), w_in_t, rows3(mla_q_norm), w_uq, rows3(mla_kv_norm), w_ukv,
            cos_t, sin_t, sc_conv_w, ssd_conv_w, rows3(ssd_conv_b), alog, dtbias, layout)
        att = _attention(qt, per_batch(k), vt, kn.reshape(bsz, layout.tiles_per_batch, SUBLANES, LANES), qn,
                         n_ctx, ctx_queries=not last)
        yf, yb = _ssd_scan(per_batch(act), per_batch(cs), side.reshape(bsz, t // SSD_CHUNK, SIDE_ROWS, SSD_CHUNK),
                           i, dskip, n_ctx)
        out = _mix_ffn(src, mod_all, i, att.reshape(-1, MLA_WIDTH), ysc, z,
                       yf.reshape(-1, SSD_WIDTH), yb.reshape(-1, SSD_WIDTH), rows3(ssd_norm),
                       w_out_b, rows3(g_post_mix), rows3(g_pre_ffn), w_ff1_b, w_ff2_b, rows3(g_post_ffn),
                       layout, latent_only=last)
        src = (out,)
    return out.reshape(bsz, seq, d)
```

```python
import functools
import math

import jax
import jax.numpy as jnp
import numpy as np
from jax import lax
from jax.experimental import pallas as pl
from jax.experimental.pallas import tpu as pltpu

F32 = jnp.float32
BF16 = jnp.bfloat16

GRID_W = 64
EPS = 1e-6
N_MOD = 6
MLA_HEADS = 4
Q_LORA = 256
KV_LORA = 128
QK_NOPE = 64
QK_ROPE = 32
V_HEAD = 64
MLA_WIDTH = MLA_HEADS * V_HEAD
MLA_SCALE = (QK_NOPE + QK_ROPE) ** -0.5
ROPE_THETA = 10000.0
SC_WIDTH = 256
SSD_HEADS = 8
SSD_HEAD_DIM = 64
SSD_WIDTH = SSD_HEADS * SSD_HEAD_DIM
SSD_GROUPS = 2
SSD_STATE = 64
SSD_CHUNK = 128
SSD_GN = SSD_GROUPS * SSD_STATE
SSD_XBC = SSD_WIDTH + 2 * SSD_GN
HEADS_PER_GROUP = SSD_HEADS // SSD_GROUPS
IN_MLA = Q_LORA + KV_LORA + QK_ROPE
IN_SC = 3 * SC_WIDTH

LANES = 128
SUBLANES = 8
VMEM_LIMIT_BYTES = 56 * 1024 * 1024

HEAD_PAD = LANES
V_ROWS = V_HEAD + 16
ROPE_LANE0 = QK_NOPE
TOKEN_TILE = 256
TILES_PER_STEP = 2
INPROJ_TILES_PER_STEP = 4
SSD_BLOCK = 2 * SSD_CHUNK
FF_CHUNK = 1024

C_Q0 = 0
C_KV0 = C_Q0 + Q_LORA
C_KR0 = C_KV0 + KV_LORA
C_SC0 = C_KR0 + LANES
C_Z0 = C_SC0 + IN_SC
C_XBC0 = C_Z0 + SSD_WIDTH
D_IN_PAD = C_XBC0 + SSD_XBC
DT_LANE0 = QK_ROPE

Q_PRESCALE = MLA_SCALE * math.log2(math.e)

SHIFT_HEADROOM = 60.0
SHIFT_MAX_BOUND = 90.0
BOUND_SLACK = 1.0 + 2.0 ** -8


def _rms(x, g):
    return x * lax.rsqrt(jnp.mean(x * x, axis=-1, keepdims=True) + EPS) * g


def _silu(x):
    return x / (1.0 + jnp.exp(-x))


def _dot(a, b):
    return jnp.dot(a, b, preferred_element_type=F32)


class _TokenLayout:
    def __init__(self, bsz, n_ctx, seq):
        tm = TOKEN_TILE
        assert n_ctx % tm == 0 and seq % tm == 0
        self.bsz = bsz
        self.ctx_tiles = n_ctx // tm
        self.lat_tiles = seq // tm
        self.tiles_per_batch = self.ctx_tiles + self.lat_tiles
        self.n_tiles = bsz * self.tiles_per_batch
        self.ctx_row = bsz

    def unified_tile(self, o, latent_only):
        if not latent_only:
            return o
        return o + (o // self.lat_tiles + 1) * self.ctx_tiles

    def coords(self, u):
        b = u // self.tiles_per_batch
        return b, u - b * self.tiles_per_batch

    def mod_row(self, u):
        b, i = self.coords(u)
        return jnp.where(i < self.ctx_tiles, self.ctx_row, b)

    def seq_edges(self, u):
        _, i = self.coords(u)
        first = (i == 0) | (i == self.ctx_tiles)
        last = (i == self.ctx_tiles - 1) | (i == self.tiles_per_batch - 1)
        return first, last

    def _latent_tile(self, u):
        b, i = self.coords(u)
        return b * self.lat_tiles + jnp.maximum(i - self.ctx_tiles, 0)

    def _context_tile(self, u):
        b, i = self.coords(u)
        return b * self.ctx_tiles + jnp.minimum(i, self.ctx_tiles - 1)

    def source_specs(self, u_of, d, split):
        tm = TOKEN_TILE
        if not split:
            return [pl.BlockSpec((tm, d), lambda j: (u_of(j), 0))]
        return [pl.BlockSpec((tm, d), lambda j: (self._context_tile(u_of(j)), 0)),
                pl.BlockSpec((tm, d), lambda j: (self._latent_tile(u_of(j)), 0))]

    def halo_specs(self, u_of, d, split):
        rb = TOKEN_TILE // SUBLANES
        if split:
            assert self.ctx_tiles == 1
            tile_of, n_rb = (lambda j: self._latent_tile(u_of(j))), self.bsz * self.lat_tiles * rb
        else:
            tile_of, n_rb = u_of, self.n_tiles * rb
        return [pl.BlockSpec((SUBLANES, d), lambda j: (jnp.maximum(tile_of(j) * rb - 1, 0), 0)),
                pl.BlockSpec((SUBLANES, d), lambda j: (jnp.minimum((tile_of(j) + 1) * rb, n_rb - 1), 0))]

    def select_source(self, u, refs):
        if len(refs) == 1:
            return refs[0][...]
        _, i = self.coords(u)
        return jnp.where(i < self.ctx_tiles, refs[0][...], refs[1][...])


def _mod_kernel(c_ref, w_ref, b_ref, o_ref):
    s = _silu(c_ref[...]).astype(BF16)
    o_ref[0] = _dot(s, w_ref[0].astype(BF16)) + b_ref[0]


def _modulation(cvec, w_mod, b_mod):
    depth, d, nd = w_mod.shape
    rows = cvec.shape[0]
    return pl.pallas_call(
        _mod_kernel,
        out_shape=jax.ShapeDtypeStruct((depth, rows, nd), F32),
        grid=(depth, nd // d),
        in_specs=[
            pl.BlockSpec((rows, d), lambda l, j: (0, 0)),
            pl.BlockSpec((1, d, d), lambda l, j: (l, 0, j)),
            pl.BlockSpec((1, 1, d), lambda l, j: (l, 0, j)),
        ],
        out_specs=pl.BlockSpec((1, rows, d), lambda l, j: (l, 0, j)),
        compiler_params=pltpu.CompilerParams(dimension_semantics=("parallel", "parallel")),
        name="modulation",
    )(cvec, w_mod, b_mod.reshape(depth, 1, nd))


def _conv3_rows(a_ext, w_ref, tm):
    te = a_ext.shape[0]
    lo, hi = SUBLANES, SUBLANES + tm
    prev = pltpu.roll(a_ext, 1, 0)[lo:hi]
    nxt = pltpu.roll(a_ext, te - 1, 0)[lo:hi]
    return w_ref[0:1, :] * prev + w_ref[1:2, :] * a_ext[lo:hi] + w_ref[2:3, :] * nxt


N_INPROJ_PARAMS = 11
N_INPROJ_TOKEN_INPUTS = 5


def _split3(x):
    hi = x.astype(BF16)
    r = x - hi.astype(F32)
    mid = r.astype(BF16)
    lo = (r - mid.astype(F32)).astype(BF16)
    return hi, mid, lo


SIDE_ROWS = 2 * 2 * SSD_HEADS


def _ssd_decay_tables(dt_raw, alog_ref, dtbias_ref, cs_ref, side_ref, slot):
    tc = SSD_CHUNK
    tm = dt_raw.shape[0]
    nd = 2 * SSD_HEADS
    a = -jnp.exp(alog_ref[...])
    xb = dt_raw + dtbias_ref[...]
    dt = jnp.maximum(xb, 0.0) + jnp.log(1.0 + jnp.exp(-jnp.abs(xb)))
    la = dt * a
    r_i = lax.broadcasted_iota(jnp.int32, (tc, tc), 0)
    c_i = lax.broadcasted_iota(jnp.int32, (tc, tc), 1)
    tri_f = jnp.where(c_i <= r_i, 1.0, 0.0).astype(BF16)
    tri_b = jnp.where(c_i >= r_i, 1.0, 0.0).astype(BF16)
    lane = lax.broadcasted_iota(jnp.int32, (1, LANES), 1)
    fwd_lane = lane < DT_LANE0 + SSD_HEADS
    row = lax.broadcasted_iota(jnp.int32, (nd, 1), 0)
    for c in range(tm // tc):
        parts = _split3(la[c * tc:(c + 1) * tc])
        cs_f = sum(_dot(tri_f, p) for p in parts)
        cs_b = sum(_dot(tri_b, p) for p in parts)
        cs = jnp.where(fwd_lane, cs_f, cs_b)
        cs_ref[slot * tm + c * tc:slot * tm + (c + 1) * tc, :] = cs
        cs_t = cs.T[DT_LANE0:DT_LANE0 + nd]
        dt_t = dt[c * tc:(c + 1) * tc].T[DT_LANE0:DT_LANE0 + nd]
        cs_end = jnp.where(row < SSD_HEADS, cs_t[:, tc - 1:tc], cs_t[:, 0:1])
        w_t = jnp.exp(cs_end - cs_t) * dt_t
        side_ref[slot * (tm // tc) + c] = jnp.concatenate([cs_t - jnp.log(dt_t), w_t], axis=0)


def _rotate_cols(w):
    a, b, c, d = jnp.split(w, 4, axis=-1)
    return jnp.concatenate([-b, a, -d, c], axis=-1)


def _relayout_weights(win_raw, wuq_raw, wukv_raw, win_ref, wuq_ref, wukv_ref):
    d = win_raw.shape[1]
    w_kr = win_raw[Q_LORA + KV_LORA:IN_MLA, :]
    w_dt = win_raw[IN_MLA + IN_SC + SSD_WIDTH + SSD_XBC:, :]
    a, b, c, e = (w_kr[i * SUBLANES:(i + 1) * SUBLANES] for i in range(4))
    kr_block = jnp.concatenate([
        -b, a, -e, c,
        w_dt, jnp.zeros((LANES // 2 - QK_ROPE - w_dt.shape[0], d), F32),
        w_kr, jnp.zeros((LANES // 2 - QK_ROPE, d), F32)], axis=0)
    win_ref[:, C_KR0:C_SC0] = kr_block.T.astype(BF16)
    step = 2 * LANES
    for src0, dst0, n_rows in ((0, C_Q0, Q_LORA + KV_LORA), (IN_MLA, C_SC0, IN_SC + SSD_WIDTH + SSD_XBC)):
        for off in range(0, n_rows, step):
            rows = min(step, n_rows - off)
            blk = win_raw[src0 + off:src0 + off + rows, :]
            win_ref[:, dst0 + off:dst0 + off + rows] = blk.T.astype(BF16)

    dqk = QK_NOPE + QK_ROPE
    wq = wuq_raw[...]
    nq = wq.shape[0]
    plain, rot = [], []
    for h in range(MLA_HEADS):
        wh = wq[:, h * dqk:(h + 1) * dqk]
        pad = jnp.zeros((nq, HEAD_PAD - dqk), F32)
        plain += [wh, pad]
        rot += [jnp.zeros((nq, QK_NOPE), F32), _rotate_cols(wh[:, QK_NOPE:]), pad]
    wuq_ref[...] = jnp.concatenate(plain + rot, axis=1).astype(BF16)

    dkv = QK_NOPE + V_HEAD
    wkv = wukv_raw[...]
    nkv = wkv.shape[0]
    ks, vs = [], []
    for h in range(MLA_HEADS):
        wh = wkv[:, h * dkv:(h + 1) * dkv]
        ks += [wh[:, :QK_NOPE], jnp.zeros((nkv, HEAD_PAD - QK_NOPE), F32)]
        vs.append(wh[:, QK_NOPE:])
    wukv_ref[...] = jnp.concatenate(ks + vs, axis=1).astype(BF16)


def _inproj_kernel(*refs, layout, split, tiles):
    n_x = 2 if split else 1
    per_half = n_x + N_INPROJ_TOKEN_INPUTS
    n_in = tiles * per_half
    (gpre, win_raw, qg, wuq_raw, kvg, wukv_raw, scw, cw, cb, alog, dtbias) = [
        r.at[0] for r in refs[n_in:n_in + N_INPROJ_PARAMS]]
    n_out = len(refs) - n_in - N_INPROJ_PARAMS - 3
    outs = refs[n_in + N_INPROJ_PARAMS:n_in + N_INPROJ_PARAMS + n_out]
    win_ref, wuq_ref, wukv_ref = refs[n_in + N_INPROJ_PARAMS + n_out:]

    @pl.when(pl.program_id(0) == 0)
    def _():
        _relayout_weights(win_raw, wuq_raw, wukv_raw, win_ref, wuq_ref, wukv_ref)

    params = (qg, wuq_ref, kvg, wukv_ref, scw, cw, cb, alog, dtbias)
    for h in range(tiles):
        half = refs[h * per_half:(h + 1) * per_half]
        u = pl.program_id(0) * tiles + h
        x = layout.select_source(u, half[:n_x])
        xp_ref, xn_ref, mod_ref, cos_ref, sin_ref = half[n_x:]
        h_ext = _inproj_normalise(x, xp_ref, xn_ref, mod_ref.at[0], gpre, layout.seq_edges(u))
        u_ext = _dot(h_ext, win_ref[...])
        _inproj_finish(u_ext, cos_ref, sin_ref, params, outs, h)


def _inproj_normalise(x, xp_ref, xn_ref, mod_ref, gpre_ref, seq_edges):
    first_of_seq, last_of_seq = seq_edges
    shift = mod_ref[0, 0:1, :]
    scale = mod_ref[0, 1:2, :]

    def norm_mod(xx):
        return _rms(xx, gpre_ref[...]) * (1.0 + scale) + shift

    h_prev = jnp.where(first_of_seq, 0.0, norm_mod(xp_ref[...]))
    h_next = jnp.where(last_of_seq, 0.0, norm_mod(xn_ref[...]))
    return jnp.concatenate([h_prev, norm_mod(x), h_next], axis=0).astype(BF16)


def _inproj_finish(u_ext, cos_ref, sin_ref, params, outs, slot):
    qg_ref, wuq_ref, kvg_ref, wukv_ref, scw_ref, cw_ref, cb_ref, alog_ref, dtbias_ref = params
    qt_ref, k_ref, vt_ref, kn_ref, qn_ref, ysc_ref, z_ref, act_ref, cs_ref, side_ref = outs
    tm = u_ext.shape[0] - 2 * SUBLANES
    rows = slice(slot * tm, (slot + 1) * tm)
    own = slice(SUBLANES, SUBLANES + tm)

    z_ref[rows, :] = u_ext[own, C_Z0:C_XBC0].astype(z_ref.dtype)
    lane = lax.broadcasted_iota(jnp.int32, (1, LANES), 1)
    krb = u_ext[own, C_KR0:C_SC0]
    dt_raw = jnp.where((lane >= DT_LANE0) & (lane < DT_LANE0 + 2 * SSD_HEADS), krb, 0.0)
    _ssd_decay_tables(dt_raw, alog_ref, dtbias_ref, cs_ref, side_ref, slot)

    conv = _conv3_rows(u_ext[:, C_XBC0:D_IN_PAD], cw_ref, tm) + cb_ref[...]
    act = _silu(conv).astype(act_ref.dtype)
    for s in range(SSD_XBC // LANES):
        act_ref[s, rows, :] = act[:, s * LANES:(s + 1) * LANES]

    w = SC_WIDTH
    prod = u_ext[:, C_SC0 + w:C_SC0 + 2 * w] * u_ext[:, C_SC0 + 2 * w:C_Z0]
    ysc_ref[rows, :] = (u_ext[own, C_SC0:C_SC0 + w] * _conv3_rows(prod, scw_ref, tm)).astype(ysc_ref.dtype)

    cos = cos_ref[...]
    sin = sin_ref[...]
    nh = MLA_HEADS
    cos_h = jnp.concatenate([cos] * nh, axis=1)
    sin_h = jnp.concatenate([sin] * nh, axis=1)

    def head_norm_maxima(a_f32):
        out = []
        for h in range(nh):
            a_h = a_f32[:, h * HEAD_PAD:(h + 1) * HEAD_PAD]
            n2 = jnp.max(jnp.sum(a_h * a_h, axis=1, keepdims=True), axis=0, keepdims=True)
            out.append(jnp.broadcast_to(n2, (1, LANES)))
        return jnp.concatenate(out + [jnp.zeros((SUBLANES - nh, LANES), F32)], axis=0)

    cq = _rms(u_ext[own, C_Q0:C_KV0], qg_ref[...]).astype(BF16)
    q2 = _dot(cq, wuq_ref[...])
    qw = nh * HEAD_PAD
    q_b = ((q2[:, :qw] * cos_h + q2[:, qw:] * sin_h) * Q_PRESCALE).astype(BF16)
    q_f = q_b.astype(F32)
    qt_ref[:, rows] = q_f.T.astype(BF16)
    qn_ref[slot] = head_norm_maxima(q_f)

    ckv = _rms(u_ext[own, C_KV0:C_KR0], kvg_ref[...]).astype(BF16)
    kv = _dot(ckv, wukv_ref[...])
    rope_lane = (lane >= ROPE_LANE0) & (lane < ROPE_LANE0 + QK_ROPE)
    cos_k = jnp.where(rope_lane, cos, 0.0)
    kr = krb * cos_k + pltpu.roll(krb, LANES // 2, 1) * sin
    k = kv[:, :qw] + jnp.concatenate([kr] * nh, axis=1)
    k_b = k.astype(BF16)
    for h in range(nh):
        k_ref[h, rows, :] = k_b[:, h * HEAD_PAD:(h + 1) * HEAD_PAD]
    kn_ref[slot] = head_norm_maxima(k_b.astype(F32))
    v_t = kv[:, qw:].T.astype(BF16)
    for h in range(nh):
        vt_ref[h, 0:V_HEAD, rows] = v_t[h * V_HEAD:(h + 1) * V_HEAD]
        vt_ref[h, V_HEAD:V_ROWS, rows] = jnp.ones((V_ROWS - V_HEAD, tm), BF16)


def _layer_spec(a, layer, single_buffer=False):
    block = (1,) + a.shape[1:]
    index_map = lambda j: (layer,) + (0,) * (a.ndim - 1)
    if single_buffer:
        return pl.BlockSpec(block, index_map, pipeline_mode=pl.Buffered(1))
    return pl.BlockSpec(block, index_map)


def _inproj(src, mod, layer, gpre, win, qg, wuq, kvg, wukv, cos_t, sin_t, scw, cw, cb, alog, dtbias, layout):
    tm = TOKEN_TILE
    d = src[0].shape[-1]
    split = len(src) == 2
    qw = MLA_HEADS * HEAD_PAD
    nt = layout.n_tiles
    tiles = INPROJ_TILES_PER_STEP
    assert nt % tiles == 0
    rows = nt * tm

    in_specs, operands = [], []
    for h in range(tiles):
        def u_of(j, h=h):
            return j * tiles + h

        def rope(j, u_of=u_of):
            return (layout.coords(u_of(j))[1], 0)

        in_specs += layout.source_specs(u_of, d, split) + layout.halo_specs(u_of, d, split)
        in_specs += [
            pl.BlockSpec((1, 1, N_MOD, d), lambda j, u_of=u_of: (layer, layout.mod_row(u_of(j)), 0, 0)),
            pl.BlockSpec((tm, LANES), rope), pl.BlockSpec((tm, LANES), rope),
        ]
        operands += list(src) + [src[-1], src[-1], mod, cos_t, sin_t]
    consts = [gpre, win, qg, wuq, kvg, wukv, scw, cw, cb, alog, dtbias]
    in_specs += [_layer_spec(c, layer, single_buffer=c is win) for c in consts]
    step_rows = tiles * tm
    chunks_per_step = step_rows // SSD_CHUNK

    def tok(width):
        return pl.BlockSpec((step_rows, width), lambda j: (j, 0))

    norm_shape = jax.ShapeDtypeStruct((nt, SUBLANES, LANES), F32)
    norm_spec = pl.BlockSpec((tiles, SUBLANES, LANES), lambda j: (j, 0, 0))
    out_shape = (
        jax.ShapeDtypeStruct((qw, rows), BF16),
        jax.ShapeDtypeStruct((MLA_HEADS, rows, HEAD_PAD), BF16),
        jax.ShapeDtypeStruct((MLA_HEADS, V_ROWS, rows), BF16),
        norm_shape,
        norm_shape,
        jax.ShapeDtypeStruct((rows, SC_WIDTH), BF16),
        jax.ShapeDtypeStruct((rows, SSD_WIDTH), BF16),
        jax.ShapeDtypeStruct((SSD_XBC // LANES, rows, LANES), BF16),
        jax.ShapeDtypeStruct((rows, LANES), F32),
        jax.ShapeDtypeStruct((rows // SSD_CHUNK, SIDE_ROWS, SSD_CHUNK), F32),
    )
    return pl.pallas_call(
        functools.partial(_inproj_kernel, layout=layout, split=split, tiles=tiles),
        out_shape=out_shape,
        grid=(nt // tiles,),
        in_specs=in_specs,
        out_specs=(pl.BlockSpec((qw, step_rows), lambda j: (0, j)),
                   pl.BlockSpec((MLA_HEADS, step_rows, HEAD_PAD), lambda j: (0, j, 0)),
                   pl.BlockSpec((MLA_HEADS, V_ROWS, step_rows), lambda j: (0, 0, j)),
                   norm_spec, norm_spec,
                   tok(SC_WIDTH), tok(SSD_WIDTH),
                   pl.BlockSpec((SSD_XBC // LANES, step_rows, LANES), lambda j: (0, j, 0)), tok(LANES),
                   pl.BlockSpec((chunks_per_step, SIDE_ROWS, SSD_CHUNK), lambda j: (j, 0, 0))),
        scratch_shapes=[pltpu.VMEM((d, D_IN_PAD), BF16),
                        pltpu.VMEM((Q_LORA, 2 * qw), BF16),
                        pltpu.VMEM((KV_LORA, qw + MLA_WIDTH), BF16)],
        compiler_params=pltpu.CompilerParams(
            dimension_semantics=("arbitrary",), vmem_limit_bytes=VMEM_LIMIT_BYTES),
        name="inproj",
    )(*operands, *consts)


def _attn_kernel(*refs, n_keys, n_ctx, ctx_queries, q_tiles):
    qt_refs = refs[:q_tiles]
    k_ref, vt_ref, kn_ref = refs[q_tiles:q_tiles + 3]
    qn_refs = refs[q_tiles + 3:2 * q_tiles + 3]
    o_ref, s_ref, p_ref = refs[2 * q_tiles + 3:]
    i = pl.program_id(1)
    kn = jnp.max(kn_ref[0], axis=0)
    for t in range(q_tiles):
        _attend_tile(qt_refs[t], qn_refs[t], k_ref, vt_ref, kn, o_ref, t, s_ref, p_ref, i,
                     n_keys=n_keys, n_ctx=n_ctx, ctx_queries=ctx_queries)


def _attend_tile(q_t, qn_ref, k_ref, vt_ref, kn, o_ref, slot, s_ref, p_ref, i, *, n_keys, n_ctx, ctx_queries):
    tq = q_t.shape[1]

    def finish(outs):
        o_ref[0, slot * tq:(slot + 1) * tq, :] = jnp.concatenate(outs, axis=0).T.astype(o_ref.dtype)

    def head_out(ov):
        return ov[0:V_HEAD] / ov[V_HEAD:V_HEAD + 1]

    def attend_two_pass(nk):
        def scores(h):
            s_ref[h, 0:nk, :] = _dot(k_ref[h, 0:nk, :],
                                     q_t[h * HEAD_PAD:(h + 1) * HEAD_PAD, :])

        def probs(h):
            s = s_ref[h, 0:nk, :]
            p_ref[h, 0:nk, :] = jnp.exp2(s - jnp.max(s, axis=0, keepdims=True)).astype(BF16)

        def values(h):
            return head_out(_dot(vt_ref[h, :, 0:nk], p_ref[h, 0:nk, :]))

        scores(0), scores(1)
        scores(2), scores(3), probs(0), probs(1)
        outs = [values(0), values(1)]
        probs(2), probs(3)
        outs += [values(2), values(3)]
        finish(outs)

    def attend_one_pass(nk, shift):
        outs = []
        for pair in range(MLA_HEADS // 2):
            heads = (2 * pair, 2 * pair + 1)
            s = [_dot(k_ref[h, 0:nk, :],
                      q_t[h * HEAD_PAD:(h + 1) * HEAD_PAD, :]) for h in heads]
            p = [jnp.exp2(sh - shift[h]).astype(BF16) for h, sh in zip(heads, s)]
            outs += [head_out(_dot(vt_ref[h, :, 0:nk], ph)) for h, ph in zip(heads, p)]
        finish(outs)

    def attend(nk):
        bound = jnp.sqrt(qn_ref[0] * kn) * BOUND_SLACK
        one_pass = jnp.max(bound) <= SHIFT_MAX_BOUND

        @pl.when(one_pass)
        def _():
            attend_one_pass(nk, [bound[h:h + 1, 0:1] - SHIFT_HEADROOM for h in range(MLA_HEADS)])

        @pl.when(jnp.logical_not(one_pass))
        def _():
            attend_two_pass(nk)

    if ctx_queries:
        @pl.when(i == 0)
        def _():
            attend_two_pass(n_ctx)

        @pl.when(i > 0)
        def _():
            attend(n_keys)
    else:
        attend(n_keys)


def _attention(qt, k, vt, kn, qn, n_ctx, ctx_queries):
    bsz = kn.shape[0]
    t = k.shape[1] // bsz
    qw = qt.shape[0]
    tq = TOKEN_TILE
    tiles = t // tq
    t0 = 0 if ctx_queries else n_ctx // tq
    nq = tiles - t0
    q_tiles = 1 if ctx_queries else 2
    assert nq % q_tiles == 0

    def tile_of(b, i, s):
        return b * tiles + i * q_tiles + s + t0

    return pl.pallas_call(
        functools.partial(_attn_kernel, n_keys=t, n_ctx=n_ctx, ctx_queries=ctx_queries, q_tiles=q_tiles),
        out_shape=jax.ShapeDtypeStruct((bsz, nq * tq, MLA_WIDTH), BF16),
        grid=(bsz, nq // q_tiles),
        in_specs=[pl.BlockSpec((qw, tq), lambda b, i, s=s: (0, tile_of(b, i, s))) for s in range(q_tiles)] + [
            pl.BlockSpec((MLA_HEADS, t, HEAD_PAD), lambda b, i: (0, b, 0)),
            pl.BlockSpec((MLA_HEADS, V_ROWS, t), lambda b, i: (0, 0, b)),
            pl.BlockSpec((1,) + kn.shape[1:], lambda b, i: (b, 0, 0, 0)),
        ] + [pl.BlockSpec((1, SUBLANES, LANES), lambda b, i, s=s: (tile_of(b, i, s), 0, 0)) for s in range(q_tiles)],
        out_specs=pl.BlockSpec((1, q_tiles * tq, MLA_WIDTH), lambda b, i: (b, i, 0)),
        scratch_shapes=[pltpu.VMEM((MLA_HEADS, t, tq), F32), pltpu.VMEM((MLA_HEADS, t, tq), BF16)],
        compiler_params=pltpu.CompilerParams(
            dimension_semantics=("parallel", "parallel"), vmem_limit_bytes=VMEM_LIMIT_BYTES),
        name="attention",
    )(*([qt] * q_tiles), k, vt, kn, *([qn] * q_tiles))


def _ssd_role(act, cs, side, direction, dskip_ref, y_ref, row0, h_ref):
    tc = SSD_CHUNK
    n = SSD_STATE
    n_xs = SSD_WIDTH // LANES
    bm = act(n_xs).astype(F32)
    cm = act(n_xs + 1)
    bm_t = bm.T

    r_i = lax.broadcasted_iota(jnp.int32, (tc, tc), 0)
    c_i = lax.broadcasted_iota(jnp.int32, (tc, tc), 1)
    tri = (c_i <= r_i) if direction == 0 else (c_i >= r_i)
    end = tc - 1 if direction == 0 else 0
    nd = 2 * SSD_HEADS
    csd_t, w_t = side[0:nd], side[nd:2 * nd]
    dec_tot = jnp.exp(cs[end:end + 1, :])
    cm_f = cm.astype(F32)
    lane = lax.broadcasted_iota(jnp.int32, (1, LANES), 1)
    low_half = lane < SSD_HEAD_DIM
    zeros_h = jnp.zeros((n, LANES), BF16)
    zeros_lhs = jnp.zeros((n, tc), BF16)

    for g in range(SSD_GROUPS):
        cb = _dot(cm[:, g * n:(g + 1) * n], bm_t[g * n:(g + 1) * n, :].astype(BF16))
        bt_g = bm_t[g * n:(g + 1) * n, :]
        for pr in range(HEADS_PER_GROUP // 2):
            slab = g * (HEADS_PER_GROUP // 2) + pr
            h_pair = h_ref[direction, g, :, pr * LANES:(pr + 1) * LANES]
            h_b = h_pair.astype(BF16)
            xs_pair = act(slab)
            rhs = jnp.concatenate([xs_pair] + [h_b if gg == g else zeros_h for gg in range(SSD_GROUPS)], axis=0)
            lhs_rows = []
            cols = []
            for hh in range(2):
                r = direction * SSD_HEADS + 2 * slab + hh
                col = DT_LANE0 + r
                cols.append(col)
                cs_col = jnp.broadcast_to(cs[:, col:col + 1], (tc, tc))
                lmd = jnp.exp(jnp.where(tri, cs_col - csd_t[r:r + 1, :], -jnp.inf))
                top = jnp.concatenate([(cb * lmd).astype(BF16), (cm_f * jnp.exp(cs_col)).astype(BF16)], axis=1)
                bot = jnp.concatenate([(bt_g * w_t[r:r + 1, :]).astype(BF16), zeros_lhs], axis=1)
                lhs_rows += [top, bot]
            lhs = jnp.concatenate(lhs_rows, axis=0)
            out = _dot(lhs, rhs)
            m = tc + n
            y_pair = jnp.where(low_half, out[0:tc], out[m:m + tc])
            if direction == 0:
                y_pair = y_pair + dskip_ref[:, slab * LANES:(slab + 1) * LANES] * xs_pair.astype(F32)
            y_ref[0, row0:row0 + tc, slab * LANES:(slab + 1) * LANES] = y_pair
            h_new = jnp.where(low_half, out[tc:m], out[m + tc:2 * m])
            keep = jnp.where(low_half, dec_tot[:, cols[0]:cols[0] + 1], dec_tot[:, cols[1]:cols[1] + 1])
            h_ref[direction, g, :, pr * LANES:(pr + 1) * LANES] = h_pair * keep + h_new


def _bwd_block(i, n_blocks, ctx_blocks):
    return jnp.where(i < ctx_blocks, ctx_blocks - 1 - i, n_blocks - 1 + ctx_blocks - i)


def _ssd_kernel(af_ref, ab_ref, csf_ref, csb_ref, sidef_ref, sideb_ref, dskip_ref, yf_ref, yb_ref, h_ref):
    @pl.when(pl.program_id(1) == 0)
    def _():
        h_ref[...] = jnp.zeros_like(h_ref)

    dskip = dskip_ref.at[0]
    tc = SSD_CHUNK
    n_sub = csf_ref.shape[1] // tc
    for j in range(n_sub):
        r0 = j * tc
        _ssd_role(lambda s, r0=r0: af_ref[s, r0:r0 + tc, :], csf_ref[0, r0:r0 + tc, :], sidef_ref[0, j], 0,
                  dskip, yf_ref, r0, h_ref)
        jb = n_sub - 1 - j
        r1 = jb * tc
        _ssd_role(lambda s, r1=r1: ab_ref[s, r1:r1 + tc, :], csb_ref[0, r1:r1 + tc, :], sideb_ref[0, jb], 1,
                  dskip, yb_ref, r1, h_ref)


def _ssd_scan(act, cs, side, layer, dskip, n_ctx):
    bsz, t, _ = cs.shape
    tc = SSD_BLOCK
    n_blocks = t // tc
    ctx_blocks = n_ctx // tc
    assert n_ctx % tc == 0 and t % tc == 0

    def fwd(i):
        return i

    def bwd(i):
        return _bwd_block(i, n_blocks, ctx_blocks)

    def main(order, width):
        return pl.BlockSpec((1, tc, width), lambda b, i: (b, order(i), 0))

    def act_spec(order):
        return pl.BlockSpec((act.shape[0], tc, LANES), lambda b, i: (0, b * n_blocks + order(i), 0))

    def side_spec(order):
        return pl.BlockSpec((1, tc // SSD_CHUNK, SIDE_ROWS, SSD_CHUNK), lambda b, i: (b, order(i), 0, 0))

    def const(shape):
        return pl.BlockSpec((1,) + shape[1:], lambda b, i: (layer,) + (0,) * (len(shape) - 1))

    y_shape = jax.ShapeDtypeStruct((bsz, t, SSD_WIDTH), F32)
    return pl.pallas_call(
        _ssd_kernel,
        out_shape=(y_shape, y_shape),
        grid=(bsz, n_blocks),
        in_specs=[
            act_spec(fwd), act_spec(bwd), main(fwd, LANES), main(bwd, LANES),
            side_spec(fwd), side_spec(bwd),
            const(dskip.shape),
        ],
        out_specs=(main(fwd, SSD_WIDTH), main(bwd, SSD_WIDTH)),
        scratch_shapes=[pltpu.VMEM((2, SSD_GROUPS, SSD_STATE, HEADS_PER_GROUP * SSD_HEAD_DIM), F32)],
        compiler_params=pltpu.CompilerParams(
            dimension_semantics=("arbitrary", "arbitrary"), vmem_limit_bytes=VMEM_LIMIT_BYTES),
        name="ssd_scan",
    )(act, act, cs, cs, side, side, dskip)


def _mix_gather(att_ref, ysc_ref, z_ref, yf_ref, yb_ref, normg_ref, ycat_ref):
    gated = (yf_ref[...] + yb_ref[...]) * _silu(z_ref[...].astype(F32))
    gw = SSD_WIDTH // SSD_GROUPS
    w = SC_WIDTH
    ycat_ref[:, 0:MLA_WIDTH] = att_ref[...]
    ycat_ref[:, MLA_WIDTH:MLA_WIDTH + w] = ysc_ref[...]
    c0 = MLA_WIDTH + w
    for g in range(SSD_GROUPS):
        gg = gated[:, g * gw:(g + 1) * gw]
        gg = gg * lax.rsqrt(jnp.mean(gg * gg, axis=-1, keepdims=True) + EPS) * normg_ref[:, g * gw:(g + 1) * gw]
        ycat_ref[:, c0 + g * gw:c0 + (g + 1) * gw] = gg.astype(BF16)


def _mix_residual(x, mod_ref, y_ref, gpost_ref, gpre2_ref, x1_ref, h2_ref):
    gate1 = mod_ref[0, 2:3, :]
    shift2 = mod_ref[0, 3:4, :]
    scale2 = mod_ref[0, 4:5, :]
    x1 = x + gate1 * _rms(y_ref[...], gpost_ref[...])
    x1_ref[...] = x1
    h2_ref[...] = (_rms(x1, gpre2_ref[...]) * (1.0 + scale2) + shift2).astype(BF16)


def _mix_mlp(h2_ref, w1_ref, w2_ref, acc_ref):
    d_ff = w1_ref.shape[1]
    for c in range(d_ff // FF_CHUNK):
        a = _dot(h2_ref[...], w1_ref[:, c * FF_CHUNK:(c + 1) * FF_CHUNK])
        r = jnp.square(jnp.maximum(a, 0.0)).astype(BF16)
        part = _dot(r, w2_ref[c * FF_CHUNK:(c + 1) * FF_CHUNK, :])
        if c == 0:
            acc_ref[...] = part
        else:
            acc_ref[...] += part


N_MIX_PARAMS = 7
N_MIX_TOKEN_INPUTS = 6


def _mix_ffn_kernel(*refs, layout, latent_only, split):
    n_x = 2 if split else 1
    per_half = n_x + N_MIX_TOKEN_INPUTS
    n_in = TILES_PER_STEP * per_half
    normg, wout, gpost, gpre2, w1, w2, gpost2 = [r.at[0] for r in refs[n_in:n_in + N_MIX_PARAMS]]
    o_ref = refs[n_in + N_MIX_PARAMS]
    scratch = refs[n_in + N_MIX_PARAMS + 1:]
    per_tile = len(scratch) // TILES_PER_STEP
    tiles = [scratch[h * per_tile:(h + 1) * per_tile] for h in range(TILES_PER_STEP)]
    tm = TOKEN_TILE
    halves = [refs[h * per_half:(h + 1) * per_half] for h in range(TILES_PER_STEP)]
    mods = [half[n_x].at[0] for half in halves]
    for half, (ycat_ref, y_ref, _, _, _) in zip(halves, tiles):
        _mix_gather(*half[n_x + 1:], normg, ycat_ref)
        y_ref[...] = _dot(ycat_ref[...], wout[...])
    for h, (half, (_, y_ref, acc_ref, x1_ref, h2_ref)) in enumerate(zip(halves, tiles)):
        u = layout.unified_tile(pl.program_id(0) * TILES_PER_STEP + h, latent_only)
        x = layout.select_source(u, half[:n_x])
        _mix_residual(x, mods[h], y_ref, gpost, gpre2, x1_ref, h2_ref)
        _mix_mlp(h2_ref, w1, w2, acc_ref)
    for h, (_, _, acc_ref, x1_ref, _) in enumerate(tiles):
        gate2 = mods[h][0, 5:6, :]
        o_ref[h * tm:(h + 1) * tm, :] = x1_ref[...] + gate2 * _rms(acc_ref[...], gpost2[...])


def _mix_ffn(src, mod, layer, att, ysc, z, yf, yb, normg, wout, gpost, gpre2, w1, w2, gpost2, layout,
             latent_only):
    tm = TOKEN_TILE
    d = src[0].shape[-1]
    n_out_tiles = layout.bsz * (layout.lat_tiles if latent_only else layout.tiles_per_batch)
    assert n_out_tiles % TILES_PER_STEP == 0
    att_unified = att.shape[0] == layout.bsz * layout.tiles_per_batch * tm

    in_specs, operands = [], []
    for h in range(TILES_PER_STEP):
        def u_of(j, h=h):
            return layout.unified_tile(j * TILES_PER_STEP + h, latent_only)

        def tok(width, tile_of=u_of):
            return pl.BlockSpec((tm, width), lambda j: (tile_of(j), 0))

        in_specs += layout.source_specs(u_of, d, len(src) == 2)
        in_specs += [
            pl.BlockSpec((1, 1, N_MOD, d), lambda j, u_of=u_of: (layer, layout.mod_row(u_of(j)), 0, 0)),
            tok(MLA_WIDTH) if att_unified else tok(MLA_WIDTH, lambda j, h=h: j * TILES_PER_STEP + h),
            tok(SC_WIDTH), tok(SSD_WIDTH), tok(SSD_WIDTH), tok(SSD_WIDTH),
        ]
        operands += list(src) + [mod, att, ysc, z, yf, yb]
    consts = [normg, wout, gpost, gpre2, w1, w2, gpost2]
    in_specs += [_layer_spec(c, layer, single_buffer=True) for c in consts]
    return pl.pallas_call(
        functools.partial(_mix_ffn_kernel, layout=layout, latent_only=latent_only, split=len(src) == 2),
        out_shape=jax.ShapeDtypeStruct((n_out_tiles * tm, d), F32),
        grid=(n_out_tiles // TILES_PER_STEP,),
        in_specs=in_specs,
        out_specs=pl.BlockSpec((TILES_PER_STEP * tm, d), lambda j: (j, 0)),
        scratch_shapes=[pltpu.VMEM((tm, d), BF16), pltpu.VMEM((tm, d), F32), pltpu.VMEM((tm, d), F32),
                        pltpu.VMEM((tm, d), F32), pltpu.VMEM((tm, d), BF16)] * TILES_PER_STEP,
        compiler_params=pltpu.CompilerParams(
            dimension_semantics=("parallel",), vmem_limit_bytes=VMEM_LIMIT_BYTES),
        name="mix_ffn",
    )(*operands, *consts)


def _rope_tables(n_ctx, seq):
    f32 = np.float32
    half = QK_ROPE // 2
    inv_freq = (f32(ROPE_THETA) ** (-np.arange(0, half, 2, dtype=f32) / f32(half))).astype(f32)
    rows = seq // GRID_W
    row = np.repeat(np.arange(rows, dtype=f32), GRID_W)
    col = np.tile(np.arange(GRID_W, dtype=f32), rows)
    ang_r = row[:, None] * inv_freq
    ang_c = col[:, None] * inv_freq
    ang = np.concatenate([ang_r, ang_r, ang_c, ang_c], axis=-1).astype(f32)
    t = n_ctx + seq
    cos_t = np.ones((t, LANES), f32)
    sin_t = np.zeros((t, LANES), f32)
    cos_t[n_ctx:, ROPE_LANE0:ROPE_LANE0 + QK_ROPE] = np.cos(ang)
    sin_t[n_ctx:, ROPE_LANE0:ROPE_LANE0 + QK_ROPE] = np.sin(ang)
    return jnp.asarray(cos_t), jnp.asarray(sin_t)


def kernel(x, c, ctx, c_ctx, w_mod, b_mod, g_pre_mix, w_in, mla_q_norm, w_uq, mla_kv_norm, w_ukv, sc_conv_w, ssd_conv_w, ssd_conv_b, ssd_a_log, ssd_dt_bias, ssd_d, ssd_norm, w_out, g_post_mix, g_pre_ffn, w_ff1, w_ff2, g_post_ffn):
    bsz, seq, d = x.shape
    n_ctx = ctx.shape[1]
    depth = w_mod.shape[0]
    assert n_ctx == TOKEN_TILE and seq % TOKEN_TILE == 0 and seq % GRID_W == 0
    layout = _TokenLayout(bsz, n_ctx, seq)
    t = n_ctx + seq

    cvec = jnp.zeros((SUBLANES, d), F32).at[:bsz].set(c).at[layout.ctx_row].set(c_ctx)
    mod_all = _modulation(cvec, w_mod, b_mod).reshape(depth, SUBLANES, N_MOD, d)
    cos_t, sin_t = _rope_tables(n_ctx, seq)

    src = (ctx.reshape(bsz * n_ctx, d), x.reshape(bsz * seq, d))

    def rows3(a):
        return a.reshape(depth, 1, -1)

    def dt_lanes(a):
        flat = a.reshape(depth, 1, -1)
        return jnp.pad(flat, ((0, 0), (0, 0), (DT_LANE0, LANES - DT_LANE0 - flat.shape[-1])))

    alog, dtbias = dt_lanes(ssd_a_log), dt_lanes(ssd_dt_bias)
    dskip = rows3(jnp.repeat(ssd_d, SSD_HEAD_DIM, axis=-1))
    w_out_b, w_ff1_b, w_ff2_b = w_out.astype(BF16), w_ff1.astype(BF16), w_ff2.astype(BF16)
    w_in_t = jnp.swapaxes(w_in, 1, 2)

    def per_batch(a):
        return a.reshape((bsz, t) + a.shape[1:])

    for i in range(depth):
        last = i == depth - 1
        qt, k, vt, kn, qn, ysc, z, act, cs, side = _inproj(
            src, mod_all, i, rows3(g_pre_mix), w_in_t, rows3(mla_q_norm), w_uq, rows3(mla_kv_norm), w_ukv,
            cos_t, sin_t, sc_conv_w, ssd_conv_w, rows3(ssd_conv_b), alog, dtbias, layout)
        att = _attention(qt, k, vt, kn.reshape(bsz, layout.tiles_per_batch, SUBLANES, LANES), qn,
                         n_ctx, ctx_queries=not last)
        yf, yb = _ssd_scan(act, per_batch(cs), side.reshape(bsz, t // SSD_CHUNK, SIDE_ROWS, SSD_CHUNK),
                           i, dskip, n_ctx)
        out = _mix_ffn(src, mod_all, i, att.reshape(-1, MLA_WIDTH), ysc, z,
                       yf.reshape(-1, SSD_WIDTH), yb.reshape(-1, SSD_WIDTH), rows3(ssd_norm),
                       w_out_b, rows3(g_post_mix), rows3(g_pre_ffn), w_ff1_b, w_ff2_b, rows3(g_post_ffn),
                       layout, latent_only=last)
        src = (out,)
    return out.reshape(bsz, seq, d)
```

```python
import functools
import math

import jax
import jax.numpy as jnp
import numpy as np
from jax import lax
from jax.experimental import pallas as pl
from jax.experimental.pallas import tpu as pltpu

F32 = jnp.float32
BF16 = jnp.bfloat16

GRID_W = 64
EPS = 1e-6
N_MOD = 6
MLA_HEADS = 4
Q_LORA = 256
KV_LORA = 128
QK_NOPE = 64
QK_ROPE = 32
V_HEAD = 64
MLA_WIDTH = MLA_HEADS * V_HEAD
MLA_SCALE = (QK_NOPE + QK_ROPE) ** -0.5
ROPE_THETA = 10000.0
SC_WIDTH = 256
SSD_HEADS = 8
SSD_HEAD_DIM = 64
SSD_WIDTH = SSD_HEADS * SSD_HEAD_DIM
SSD_GROUPS = 2
SSD_STATE = 64
SSD_CHUNK = 128
SSD_GN = SSD_GROUPS * SSD_STATE
SSD_XBC = SSD_WIDTH + 2 * SSD_GN
HEADS_PER_GROUP = SSD_HEADS // SSD_GROUPS
IN_MLA = Q_LORA + KV_LORA + QK_ROPE
IN_SC = 3 * SC_WIDTH

LANES = 128
SUBLANES = 8
VMEM_LIMIT_BYTES = 56 * 1024 * 1024

HEAD_PAD = LANES
V_ROWS = V_HEAD + 16
ROPE_LANE0 = QK_NOPE
TOKEN_TILE = 256
TILES_PER_STEP = 2
INPROJ_TILES_PER_STEP = 4
SSD_BLOCK = 2 * SSD_CHUNK
FF_CHUNK = 1024

C_Q0 = 0
C_KV0 = C_Q0 + Q_LORA
C_KR0 = C_KV0 + KV_LORA
C_SC0 = C_KR0 + LANES
C_Z0 = C_SC0 + IN_SC
C_XBC0 = C_Z0 + SSD_WIDTH
D_IN_PAD = C_XBC0 + SSD_XBC
DT_LANE0 = QK_ROPE

Q_PRESCALE = MLA_SCALE * math.log2(math.e)

SHIFT_HEADROOM = 60.0
SHIFT_MAX_BOUND = 90.0
BOUND_SLACK = 1.0 + 2.0 ** -8


def _rms(x, g):
    return x * lax.rsqrt(jnp.mean(x * x, axis=-1, keepdims=True) + EPS) * g


def _silu(x):
    return x / (1.0 + jnp.exp(-x))


def _dot(a, b):
    return jnp.dot(a, b, preferred_element_type=F32)


class _TokenLayout:
    def __init__(self, bsz, n_ctx, seq):
        tm = TOKEN_TILE
        assert n_ctx % tm == 0 and seq % tm == 0
        self.bsz = bsz
        self.ctx_tiles = n_ctx // tm
        self.lat_tiles = seq // tm
        self.tiles_per_batch = self.ctx_tiles + self.lat_tiles
        self.n_tiles = bsz * self.tiles_per_batch
        self.ctx_row = bsz

    def unified_tile(self, o, latent_only):
        if not latent_only:
            return o
        return o + (o // self.lat_tiles + 1) * self.ctx_tiles

    def coords(self, u):
        b = u // self.tiles_per_batch
        return b, u - b * self.tiles_per_batch

    def mod_row(self, u):
        b, i = self.coords(u)
        return jnp.where(i < self.ctx_tiles, self.ctx_row, b)

    def seq_edges(self, u):
        _, i = self.coords(u)
        first = (i == 0) | (i == self.ctx_tiles)
        last = (i == self.ctx_tiles - 1) | (i == self.tiles_per_batch - 1)
        return first, last

    def _latent_tile(self, u):
        b, i = self.coords(u)
        return b * self.lat_tiles + jnp.maximum(i - self.ctx_tiles, 0)

    def _context_tile(self, u):
        b, i = self.coords(u)
        return b * self.ctx_tiles + jnp.minimum(i, self.ctx_tiles - 1)

    def source_specs(self, u_of, d, split):
        tm = TOKEN_TILE
        if not split:
            return [pl.BlockSpec((tm, d), lambda j: (u_of(j), 0))]
        return [pl.BlockSpec((tm, d), lambda j: (self._context_tile(u_of(j)), 0)),
                pl.BlockSpec((tm, d), lambda j: (self._latent_tile(u_of(j)), 0))]

    def halo_specs(self, u_of, d, split):
        rb = TOKEN_TILE // SUBLANES
        if split:
            assert self.ctx_tiles == 1
            tile_of, n_rb = (lambda j: self._latent_tile(u_of(j))), self.bsz * self.lat_tiles * rb
        else:
            tile_of, n_rb = u_of, self.n_tiles * rb
        return [pl.BlockSpec((SUBLANES, d), lambda j: (jnp.maximum(tile_of(j) * rb - 1, 0), 0)),
                pl.BlockSpec((SUBLANES, d), lambda j: (jnp.minimum((tile_of(j) + 1) * rb, n_rb - 1), 0))]

    def select_source(self, u, refs):
        if len(refs) == 1:
            return refs[0][...]
        _, i = self.coords(u)
        return jnp.where(i < self.ctx_tiles, refs[0][...], refs[1][...])


def _mod_kernel(c_ref, w_ref, b_ref, o_ref):
    s = _silu(c_ref[...]).astype(BF16)
    o_ref[0] = _dot(s, w_ref[0].astype(BF16)) + b_ref[0]


def _modulation(cvec, w_mod, b_mod):
    depth, d, nd = w_mod.shape
    rows = cvec.shape[0]
    return pl.pallas_call(
        _mod_kernel,
        out_shape=jax.ShapeDtypeStruct((depth, rows, nd), F32),
        grid=(depth, nd // d),
        in_specs=[
            pl.BlockSpec((rows, d), lambda l, j: (0, 0)),
            pl.BlockSpec((1, d, d), lambda l, j: (l, 0, j)),
            pl.BlockSpec((1, 1, d), lambda l, j: (l, 0, j)),
        ],
        out_specs=pl.BlockSpec((1, rows, d), lambda l, j: (l, 0, j)),
        compiler_params=pltpu.CompilerParams(dimension_semantics=("parallel", "parallel")),
        name="modulation",
    )(cvec, w_mod, b_mod.reshape(depth, 1, nd))


def _conv3_rows(a_ext, w_ref, tm):
    te = a_ext.shape[0]
    lo, hi = SUBLANES, SUBLANES + tm
    prev = pltpu.roll(a_ext, 1, 0)[lo:hi]
    nxt = pltpu.roll(a_ext, te - 1, 0)[lo:hi]
    return w_ref[0:1, :] * prev + w_ref[1:2, :] * a_ext[lo:hi] + w_ref[2:3, :] * nxt


N_INPROJ_PARAMS = 11
N_INPROJ_TOKEN_INPUTS = 5


def _split3(x):
    hi = x.astype(BF16)
    r = x - hi.astype(F32)
    mid = r.astype(BF16)
    lo = (r - mid.astype(F32)).astype(BF16)
    return hi, mid, lo


SIDE_ROWS = 2 * 2 * SSD_HEADS


def _ssd_decay_tables(dt_raw, alog_ref, dtbias_ref, cs_ref, side_ref, slot):
    tc = SSD_CHUNK
    tm = dt_raw.shape[0]
    nd = 2 * SSD_HEADS
    a = -jnp.exp(alog_ref[...])
    xb = dt_raw + dtbias_ref[...]
    dt = jnp.maximum(xb, 0.0) + jnp.log(1.0 + jnp.exp(-jnp.abs(xb)))
    la = dt * a
    r_i = lax.broadcasted_iota(jnp.int32, (tc, tc), 0)
    c_i = lax.broadcasted_iota(jnp.int32, (tc, tc), 1)
    tri_f = jnp.where(c_i <= r_i, 1.0, 0.0).astype(BF16)
    tri_b = jnp.where(c_i >= r_i, 1.0, 0.0).astype(BF16)
    lane = lax.broadcasted_iota(jnp.int32, (1, LANES), 1)
    fwd_lane = lane < DT_LANE0 + SSD_HEADS
    row = lax.broadcasted_iota(jnp.int32, (nd, 1), 0)
    for c in range(tm // tc):
        parts = _split3(la[c * tc:(c + 1) * tc])
        cs_f = sum(_dot(tri_f, p) for p in parts)
        cs_b = sum(_dot(tri_b, p) for p in parts)
        cs = jnp.where(fwd_lane, cs_f, cs_b)
        cs_ref[slot * tm + c * tc:slot * tm + (c + 1) * tc, :] = cs
        cs_t = cs.T[DT_LANE0:DT_LANE0 + nd]
        dt_t = dt[c * tc:(c + 1) * tc].T[DT_LANE0:DT_LANE0 + nd]
        cs_end = jnp.where(row < SSD_HEADS, cs_t[:, tc - 1:tc], cs_t[:, 0:1])
        w_t = jnp.exp(cs_end - cs_t) * dt_t
        side_ref[slot * (tm // tc) + c] = jnp.concatenate([cs_t - jnp.log(dt_t), w_t], axis=0)


def _rotate_cols(w):
    a, b, c, d = jnp.split(w, 4, axis=-1)
    return jnp.concatenate([-b, a, -d, c], axis=-1)


def _relayout_weights(win_raw, wuq_raw, wukv_raw, win_ref, wuq_ref, wukv_ref):
    d = win_raw.shape[1]
    w_kr = win_raw[Q_LORA + KV_LORA:IN_MLA, :]
    w_dt = win_raw[IN_MLA + IN_SC + SSD_WIDTH + SSD_XBC:, :]
    a, b, c, e = (w_kr[i * SUBLANES:(i + 1) * SUBLANES] for i in range(4))
    kr_block = jnp.concatenate([
        -b, a, -e, c,
        w_dt, jnp.zeros((LANES // 2 - QK_ROPE - w_dt.shape[0], d), F32),
        w_kr, jnp.zeros((LANES // 2 - QK_ROPE, d), F32)], axis=0)
    win_ref[:, C_KR0:C_SC0] = kr_block.T.astype(BF16)
    step = 2 * LANES
    for src0, dst0, n_rows in ((0, C_Q0, Q_LORA + KV_LORA), (IN_MLA, C_SC0, IN_SC + SSD_WIDTH + SSD_XBC)):
        for off in range(0, n_rows, step):
            rows = min(step, n_rows - off)
            blk = win_raw[src0 + off:src0 + off + rows, :]
            win_ref[:, dst0 + off:dst0 + off + rows] = blk.T.astype(BF16)

    dqk = QK_NOPE + QK_ROPE
    wq = wuq_raw[...]
    nq = wq.shape[0]
    plain, rot = [], []
    for h in range(MLA_HEADS):
        wh = wq[:, h * dqk:(h + 1) * dqk]
        pad = jnp.zeros((nq, HEAD_PAD - dqk), F32)
        plain += [wh, pad]
        rot += [jnp.zeros((nq, QK_NOPE), F32), _rotate_cols(wh[:, QK_NOPE:]), pad]
    wuq_ref[...] = jnp.concatenate(plain + rot, axis=1).astype(BF16)

    dkv = QK_NOPE + V_HEAD
    wkv = wukv_raw[...]
    nkv = wkv.shape[0]
    ks, vs = [], []
    for h in range(MLA_HEADS):
        wh = wkv[:, h * dkv:(h + 1) * dkv]
        ks += [wh[:, :QK_NOPE], jnp.zeros((nkv, HEAD_PAD - QK_NOPE), F32)]
        vs.append(wh[:, QK_NOPE:])
    wukv_ref[...] = jnp.concatenate(ks + vs, axis=1).astype(BF16)


def _inproj_kernel(*refs, layout, split, tiles):
    n_x = 2 if split else 1
    per_half = n_x + N_INPROJ_TOKEN_INPUTS
    n_in = tiles * per_half
    (gpre, win_raw, qg, wuq_raw, kvg, wukv_raw, scw, cw, cb, alog, dtbias) = [
        r.at[0] for r in refs[n_in:n_in + N_INPROJ_PARAMS]]
    n_out = len(refs) - n_in - N_INPROJ_PARAMS - 3
    outs = refs[n_in + N_INPROJ_PARAMS:n_in + N_INPROJ_PARAMS + n_out]
    win_ref, wuq_ref, wukv_ref = refs[n_in + N_INPROJ_PARAMS + n_out:]

    @pl.when(pl.program_id(0) == 0)
    def _():
        _relayout_weights(win_raw, wuq_raw, wukv_raw, win_ref, wuq_ref, wukv_ref)

    params = (qg, wuq_ref, kvg, wukv_ref, scw, cw, cb, alog, dtbias)
    for h in range(tiles):
        half = refs[h * per_half:(h + 1) * per_half]
        u = pl.program_id(0) * tiles + h
        x = layout.select_source(u, half[:n_x])
        xp_ref, xn_ref, mod_ref, cos_ref, sin_ref = half[n_x:]
        h_ext = _inproj_normalise(x, xp_ref, xn_ref, mod_ref.at[0], gpre, layout.seq_edges(u))
        u_ext = _dot(h_ext, win_ref[...])
        _inproj_finish(u_ext, cos_ref, sin_ref, params, outs, h)


def _inproj_normalise(x, xp_ref, xn_ref, mod_ref, gpre_ref, seq_edges):
    first_of_seq, last_of_seq = seq_edges
    shift = mod_ref[0, 0:1, :]
    scale = mod_ref[0, 1:2, :]

    def norm_mod(xx):
        return _rms(xx, gpre_ref[...]) * (1.0 + scale) + shift

    h_prev = jnp.where(first_of_seq, 0.0, norm_mod(xp_ref[...]))
    h_next = jnp.where(last_of_seq, 0.0, norm_mod(xn_ref[...]))
    return jnp.concatenate([h_prev, norm_mod(x), h_next], axis=0).astype(BF16)


def _inproj_finish(u_ext, cos_ref, sin_ref, params, outs, slot):
    qg_ref, wuq_ref, kvg_ref, wukv_ref, scw_ref, cw_ref, cb_ref, alog_ref, dtbias_ref = params
    qt_ref, k_ref, vt_ref, kn_ref, qn_ref, ysc_ref, z_ref, act_ref, cs_ref, side_ref = outs
    tm = u_ext.shape[0] - 2 * SUBLANES
    rows = slice(slot * tm, (slot + 1) * tm)
    own = slice(SUBLANES, SUBLANES + tm)

    z_ref[rows, :] = u_ext[own, C_Z0:C_XBC0].astype(z_ref.dtype)
    lane = lax.broadcasted_iota(jnp.int32, (1, LANES), 1)
    krb = u_ext[own, C_KR0:C_SC0]
    dt_raw = jnp.where((lane >= DT_LANE0) & (lane < DT_LANE0 + 2 * SSD_HEADS), krb, 0.0)
    _ssd_decay_tables(dt_raw, alog_ref, dtbias_ref, cs_ref, side_ref, slot)

    conv = _conv3_rows(u_ext[:, C_XBC0:D_IN_PAD], cw_ref, tm) + cb_ref[...]
    act_ref[rows, :] = _silu(conv).astype(act_ref.dtype)

    w = SC_WIDTH
    prod = u_ext[:, C_SC0 + w:C_SC0 + 2 * w] * u_ext[:, C_SC0 + 2 * w:C_Z0]
    ysc_ref[rows, :] = (u_ext[own, C_SC0:C_SC0 + w] * _conv3_rows(prod, scw_ref, tm)).astype(ysc_ref.dtype)

    cos = cos_ref[...]
    sin = sin_ref[...]
    nh = MLA_HEADS
    cos_h = jnp.concatenate([cos] * nh, axis=1)
    sin_h = jnp.concatenate([sin] * nh, axis=1)

    def head_norm_maxima(a_f32):
        out = []
        for h in range(nh):
            a_h = a_f32[:, h * HEAD_PAD:(h + 1) * HEAD_PAD]
            n2 = jnp.max(jnp.sum(a_h * a_h, axis=1, keepdims=True), axis=0, keepdims=True)
            out.append(jnp.broadcast_to(n2, (1, LANES)))
        return jnp.concatenate(out + [jnp.zeros((SUBLANES - nh, LANES), F32)], axis=0)

    cq = _rms(u_ext[own, C_Q0:C_KV0], qg_ref[...]).astype(BF16)
    q2 = _dot(cq, wuq_ref[...])
    qw = nh * HEAD_PAD
    q_b = ((q2[:, :qw] * cos_h + q2[:, qw:] * sin_h) * Q_PRESCALE).astype(BF16)
    q_f = q_b.astype(F32)
    qt_ref[:, rows] = q_f.T.astype(BF16)
    qn_ref[slot] = head_norm_maxima(q_f)

    ckv = _rms(u_ext[own, C_KV0:C_KR0], kvg_ref[...]).astype(BF16)
    kv = _dot(ckv, wukv_ref[...])
    rope_lane = (lane >= ROPE_LANE0) & (lane < ROPE_LANE0 + QK_ROPE)
    cos_k = jnp.where(rope_lane, cos, 0.0)
    kr = krb * cos_k + pltpu.roll(krb, LANES // 2, 1) * sin
    k = kv[:, :qw] + jnp.concatenate([kr] * nh, axis=1)
    k_b = k.astype(BF16)
    k_ref[rows, :] = k_b
    kn_ref[slot] = head_norm_maxima(k_b.astype(F32))
    v_t = kv[:, qw:].T.astype(BF16)
    for h in range(nh):
        vt_ref[h, 0:V_HEAD, rows] = v_t[h * V_HEAD:(h + 1) * V_HEAD]
        vt_ref[h, V_HEAD:V_ROWS, rows] = jnp.ones((V_ROWS - V_HEAD, tm), BF16)


def _layer_spec(a, layer, single_buffer=False):
    block = (1,) + a.shape[1:]
    index_map = lambda j: (layer,) + (0,) * (a.ndim - 1)
    if single_buffer:
        return pl.BlockSpec(block, index_map, pipeline_mode=pl.Buffered(1))
    return pl.BlockSpec(block, index_map)


def _inproj(src, mod, layer, gpre, win, qg, wuq, kvg, wukv, cos_t, sin_t, scw, cw, cb, alog, dtbias, layout):
    tm = TOKEN_TILE
    d = src[0].shape[-1]
    split = len(src) == 2
    qw = MLA_HEADS * HEAD_PAD
    nt = layout.n_tiles
    tiles = INPROJ_TILES_PER_STEP
    assert nt % tiles == 0
    rows = nt * tm

    in_specs, operands = [], []
    for h in range(tiles):
        def u_of(j, h=h):
            return j * tiles + h

        def rope(j, u_of=u_of):
            return (layout.coords(u_of(j))[1], 0)

        in_specs += layout.source_specs(u_of, d, split) + layout.halo_specs(u_of, d, split)
        in_specs += [
            pl.BlockSpec((1, 1, N_MOD, d), lambda j, u_of=u_of: (layer, layout.mod_row(u_of(j)), 0, 0)),
            pl.BlockSpec((tm, LANES), rope), pl.BlockSpec((tm, LANES), rope),
        ]
        operands += list(src) + [src[-1], src[-1], mod, cos_t, sin_t]
    consts = [gpre, win, qg, wuq, kvg, wukv, scw, cw, cb, alog, dtbias]
    in_specs += [_layer_spec(c, layer, single_buffer=c is win) for c in consts]
    step_rows = tiles * tm
    chunks_per_step = step_rows // SSD_CHUNK

    def tok(width):
        return pl.BlockSpec((step_rows, width), lambda j: (j, 0))

    norm_shape = jax.ShapeDtypeStruct((nt, SUBLANES, LANES), F32)
    norm_spec = pl.BlockSpec((tiles, SUBLANES, LANES), lambda j: (j, 0, 0))
    out_shape = (
        jax.ShapeDtypeStruct((qw, rows), BF16),
        jax.ShapeDtypeStruct((rows, qw), BF16),
        jax.ShapeDtypeStruct((MLA_HEADS, V_ROWS, rows), BF16),
        norm_shape,
        norm_shape,
        jax.ShapeDtypeStruct((rows, SC_WIDTH), BF16),
        jax.ShapeDtypeStruct((rows, SSD_WIDTH), BF16),
        jax.ShapeDtypeStruct((rows, SSD_XBC), BF16),
        jax.ShapeDtypeStruct((rows, LANES), F32),
        jax.ShapeDtypeStruct((rows // SSD_CHUNK, SIDE_ROWS, SSD_CHUNK), F32),
    )
    return pl.pallas_call(
        functools.partial(_inproj_kernel, layout=layout, split=split, tiles=tiles),
        out_shape=out_shape,
        grid=(nt // tiles,),
        in_specs=in_specs,
        out_specs=(pl.BlockSpec((qw, step_rows), lambda j: (0, j)), tok(qw),
                   pl.BlockSpec((MLA_HEADS, V_ROWS, step_rows), lambda j: (0, 0, j)),
                   norm_spec, norm_spec,
                   tok(SC_WIDTH), tok(SSD_WIDTH), tok(SSD_XBC), tok(LANES),
                   pl.BlockSpec((chunks_per_step, SIDE_ROWS, SSD_CHUNK), lambda j: (j, 0, 0))),
        scratch_shapes=[pltpu.VMEM((d, D_IN_PAD), BF16),
                        pltpu.VMEM((Q_LORA, 2 * qw), BF16),
                        pltpu.VMEM((KV_LORA, qw + MLA_WIDTH), BF16)],
        compiler_params=pltpu.CompilerParams(
            dimension_semantics=("arbitrary",), vmem_limit_bytes=VMEM_LIMIT_BYTES),
        name="inproj",
    )(*operands, *consts)


def _attn_kernel(*refs, n_keys, n_ctx, ctx_queries, q_tiles):
    qt_refs = refs[:q_tiles]
    k_ref, vt_ref, kn_ref = refs[q_tiles:q_tiles + 3]
    qn_refs = refs[q_tiles + 3:2 * q_tiles + 3]
    o_ref, s_ref, p_ref = refs[2 * q_tiles + 3:]
    i = pl.program_id(1)
    kn = jnp.max(kn_ref[0], axis=0)
    for t in range(q_tiles):
        _attend_tile(qt_refs[t], qn_refs[t], k_ref, vt_ref, kn, o_ref, t, s_ref, p_ref, i,
                     n_keys=n_keys, n_ctx=n_ctx, ctx_queries=ctx_queries)


def _attend_tile(q_t, qn_ref, k_ref, vt_ref, kn, o_ref, slot, s_ref, p_ref, i, *, n_keys, n_ctx, ctx_queries):
    tq = q_t.shape[1]

    def finish(outs):
        o_ref[0, slot * tq:(slot + 1) * tq, :] = jnp.concatenate(outs, axis=0).T.astype(o_ref.dtype)

    def head_out(ov):
        return ov[0:V_HEAD] / ov[V_HEAD:V_HEAD + 1]

    def attend_two_pass(nk):
        def scores(h):
            s_ref[h, 0:nk, :] = _dot(k_ref[0, 0:nk, h * HEAD_PAD:(h + 1) * HEAD_PAD],
                                     q_t[h * HEAD_PAD:(h + 1) * HEAD_PAD, :])

        def probs(h):
            s = s_ref[h, 0:nk, :]
            p_ref[h, 0:nk, :] = jnp.exp2(s - jnp.max(s, axis=0, keepdims=True)).astype(BF16)

        def values(h):
            return head_out(_dot(vt_ref[h, :, 0:nk], p_ref[h, 0:nk, :]))

        scores(0), scores(1)
        scores(2), scores(3), probs(0), probs(1)
        outs = [values(0), values(1)]
        probs(2), probs(3)
        outs += [values(2), values(3)]
        finish(outs)

    def attend_one_pass(nk, shift):
        outs = []
        for pair in range(MLA_HEADS // 2):
            heads = (2 * pair, 2 * pair + 1)
            s = [_dot(k_ref[0, 0:nk, h * HEAD_PAD:(h + 1) * HEAD_PAD],
                      q_t[h * HEAD_PAD:(h + 1) * HEAD_PAD, :]) for h in heads]
            p = [jnp.exp2(sh - shift[h]).astype(BF16) for h, sh in zip(heads, s)]
            outs += [head_out(_dot(vt_ref[h, :, 0:nk], ph)) for h, ph in zip(heads, p)]
        finish(outs)

    def attend(nk):
        bound = jnp.sqrt(qn_ref[0] * kn) * BOUND_SLACK
        one_pass = jnp.max(bound) <= SHIFT_MAX_BOUND

        @pl.when(one_pass)
        def _():
            attend_one_pass(nk, [bound[h:h + 1, 0:1] - SHIFT_HEADROOM for h in range(MLA_HEADS)])

        @pl.when(jnp.logical_not(one_pass))
        def _():
            attend_two_pass(nk)

    if ctx_queries:
        @pl.when(i == 0)
        def _():
            attend_two_pass(n_ctx)

        @pl.when(i > 0)
        def _():
            attend(n_keys)
    else:
        attend(n_keys)


def _attention(qt, k, vt, kn, qn, n_ctx, ctx_queries):
    bsz, t, qw = k.shape
    tq = TOKEN_TILE
    tiles = t // tq
    t0 = 0 if ctx_queries else n_ctx // tq
    nq = tiles - t0
    q_tiles = 1 if ctx_queries else 2
    assert nq % q_tiles == 0

    def tile_of(b, i, s):
        return b * tiles + i * q_tiles + s + t0

    return pl.pallas_call(
        functools.partial(_attn_kernel, n_keys=t, n_ctx=n_ctx, ctx_queries=ctx_queries, q_tiles=q_tiles),
        out_shape=jax.ShapeDtypeStruct((bsz, nq * tq, MLA_WIDTH), BF16),
        grid=(bsz, nq // q_tiles),
        in_specs=[pl.BlockSpec((qw, tq), lambda b, i, s=s: (0, tile_of(b, i, s))) for s in range(q_tiles)] + [
            pl.BlockSpec((1, t, qw), lambda b, i: (b, 0, 0)),
            pl.BlockSpec((MLA_HEADS, V_ROWS, t), lambda b, i: (0, 0, b)),
            pl.BlockSpec((1,) + kn.shape[1:], lambda b, i: (b, 0, 0, 0)),
        ] + [pl.BlockSpec((1, SUBLANES, LANES), lambda b, i, s=s: (tile_of(b, i, s), 0, 0)) for s in range(q_tiles)],
        out_specs=pl.BlockSpec((1, q_tiles * tq, MLA_WIDTH), lambda b, i: (b, i, 0)),
        scratch_shapes=[pltpu.VMEM((MLA_HEADS, t, tq), F32), pltpu.VMEM((MLA_HEADS, t, tq), BF16)],
        compiler_params=pltpu.CompilerParams(
            dimension_semantics=("parallel", "parallel"), vmem_limit_bytes=VMEM_LIMIT_BYTES),
        name="attention",
    )(*([qt] * q_tiles), k, vt, kn, *([qn] * q_tiles))


def _ssd_role(act, cs, side, direction, dskip_ref, y_ref, row0, h_ref):
    tc = SSD_CHUNK
    n = SSD_STATE
    xs = act[:, :SSD_WIDTH]
    bm = act[:, SSD_WIDTH:SSD_WIDTH + SSD_GN].astype(F32)
    cm = act[:, SSD_WIDTH + SSD_GN:]
    bm_t = bm.T

    r_i = lax.broadcasted_iota(jnp.int32, (tc, tc), 0)
    c_i = lax.broadcasted_iota(jnp.int32, (tc, tc), 1)
    tri = (c_i <= r_i) if direction == 0 else (c_i >= r_i)
    end = tc - 1 if direction == 0 else 0
    nd = 2 * SSD_HEADS
    csd_t, w_t = side[0:nd], side[nd:2 * nd]
    dec_tot = jnp.exp(cs[end:end + 1, :])
    cm_f = cm.astype(F32)
    lane = lax.broadcasted_iota(jnp.int32, (1, LANES), 1)
    low_half = lane < SSD_HEAD_DIM
    zeros_h = jnp.zeros((n, LANES), BF16)
    zeros_lhs = jnp.zeros((n, tc), BF16)

    for g in range(SSD_GROUPS):
        cb = _dot(cm[:, g * n:(g + 1) * n], bm_t[g * n:(g + 1) * n, :].astype(BF16))
        bt_g = bm_t[g * n:(g + 1) * n, :]
        for pr in range(HEADS_PER_GROUP // 2):
            slab = g * (HEADS_PER_GROUP // 2) + pr
            h_pair = h_ref[direction, g, :, pr * LANES:(pr + 1) * LANES]
            h_b = h_pair.astype(BF16)
            xs_pair = xs[:, slab * LANES:(slab + 1) * LANES]
            rhs = jnp.concatenate([xs_pair] + [h_b if gg == g else zeros_h for gg in range(SSD_GROUPS)], axis=0)
            lhs_rows = []
            cols = []
            for hh in range(2):
                r = direction * SSD_HEADS + 2 * slab + hh
                col = DT_LANE0 + r
                cols.append(col)
                cs_col = jnp.broadcast_to(cs[:, col:col + 1], (tc, tc))
                lmd = jnp.exp(jnp.where(tri, cs_col - csd_t[r:r + 1, :], -jnp.inf))
                top = jnp.concatenate([(cb * lmd).astype(BF16), (cm_f * jnp.exp(cs_col)).astype(BF16)], axis=1)
                bot = jnp.concatenate([(bt_g * w_t[r:r + 1, :]).astype(BF16), zeros_lhs], axis=1)
                lhs_rows += [top, bot]
            lhs = jnp.concatenate(lhs_rows, axis=0)
            out = _dot(lhs, rhs)
            m = tc + n
            y_pair = jnp.where(low_half, out[0:tc], out[m:m + tc])
            if direction == 0:
                y_pair = y_pair + dskip_ref[:, slab * LANES:(slab + 1) * LANES] * xs_pair.astype(F32)
            y_ref[0, row0:row0 + tc, slab * LANES:(slab + 1) * LANES] = y_pair
            h_new = jnp.where(low_half, out[tc:m], out[m + tc:2 * m])
            keep = jnp.where(low_half, dec_tot[:, cols[0]:cols[0] + 1], dec_tot[:, cols[1]:cols[1] + 1])
            h_ref[direction, g, :, pr * LANES:(pr + 1) * LANES] = h_pair * keep + h_new


def _bwd_block(i, n_blocks, ctx_blocks):
    return jnp.where(i < ctx_blocks, ctx_blocks - 1 - i, n_blocks - 1 + ctx_blocks - i)


def _ssd_kernel(af_ref, ab_ref, csf_ref, csb_ref, sidef_ref, sideb_ref, dskip_ref, yf_ref, yb_ref, h_ref):
    @pl.when(pl.program_id(1) == 0)
    def _():
        h_ref[...] = jnp.zeros_like(h_ref)

    dskip = dskip_ref.at[0]
    tc = SSD_CHUNK
    n_sub = af_ref.shape[1] // tc
    for j in range(n_sub):
        r0 = j * tc
        _ssd_role(af_ref[0, r0:r0 + tc, :], csf_ref[0, r0:r0 + tc, :], sidef_ref[0, j], 0, dskip, yf_ref, r0, h_ref)
        jb = n_sub - 1 - j
        r1 = jb * tc
        _ssd_role(ab_ref[0, r1:r1 + tc, :], csb_ref[0, r1:r1 + tc, :], sideb_ref[0, jb], 1, dskip, yb_ref, r1, h_ref)


def _ssd_scan(act, cs, side, layer, dskip, n_ctx):
    bsz, t, _ = act.shape
    tc = SSD_BLOCK
    n_blocks = t // tc
    ctx_blocks = n_ctx // tc
    assert n_ctx % tc == 0 and t % tc == 0

    def fwd(i):
        return i

    def bwd(i):
        return _bwd_block(i, n_blocks, ctx_blocks)

    def main(order, width):
        return pl.BlockSpec((1, tc, width), lambda b, i: (b, order(i), 0))

    def side_spec(order):
        return pl.BlockSpec((1, tc // SSD_CHUNK, SIDE_ROWS, SSD_CHUNK), lambda b, i: (b, order(i), 0, 0))

    def const(shape):
        return pl.BlockSpec((1,) + shape[1:], lambda b, i: (layer,) + (0,) * (len(shape) - 1))

    y_shape = jax.ShapeDtypeStruct((bsz, t, SSD_WIDTH), F32)
    return pl.pallas_call(
        _ssd_kernel,
        out_shape=(y_shape, y_shape),
        grid=(bsz, n_blocks),
        in_specs=[
            main(fwd, SSD_XBC), main(bwd, SSD_XBC), main(fwd, LANES), main(bwd, LANES),
            side_spec(fwd), side_spec(bwd),
            const(dskip.shape),
        ],
        out_specs=(main(fwd, SSD_WIDTH), main(bwd, SSD_WIDTH)),
        scratch_shapes=[pltpu.VMEM((2, SSD_GROUPS, SSD_STATE, HEADS_PER_GROUP * SSD_HEAD_DIM), F32)],
        compiler_params=pltpu.CompilerParams(
            dimension_semantics=("arbitrary", "arbitrary"), vmem_limit_bytes=VMEM_LIMIT_BYTES),
        name="ssd_scan",
    )(act, act, cs, cs, side, side, dskip)


def _mix_gather(att_ref, ysc_ref, z_ref, yf_ref, yb_ref, normg_ref, ycat_ref):
    gated = (yf_ref[...] + yb_ref[...]) * _silu(z_ref[...].astype(F32))
    gw = SSD_WIDTH // SSD_GROUPS
    w = SC_WIDTH
    ycat_ref[:, 0:MLA_WIDTH] = att_ref[...]
    ycat_ref[:, MLA_WIDTH:MLA_WIDTH + w] = ysc_ref[...]
    c0 = MLA_WIDTH + w
    for g in range(SSD_GROUPS):
        gg = gated[:, g * gw:(g + 1) * gw]
        gg = gg * lax.rsqrt(jnp.mean(gg * gg, axis=-1, keepdims=True) + EPS) * normg_ref[:, g * gw:(g + 1) * gw]
        ycat_ref[:, c0 + g * gw:c0 + (g + 1) * gw] = gg.astype(BF16)


def _mix_residual(x, mod_ref, y_ref, gpost_ref, gpre2_ref, x1_ref, h2_ref):
    gate1 = mod_ref[0, 2:3, :]
    shift2 = mod_ref[0, 3:4, :]
    scale2 = mod_ref[0, 4:5, :]
    x1 = x + gate1 * _rms(y_ref[...], gpost_ref[...])
    x1_ref[...] = x1
    h2_ref[...] = (_rms(x1, gpre2_ref[...]) * (1.0 + scale2) + shift2).astype(BF16)


def _mix_mlp(h2_ref, w1_ref, w2_ref, acc_ref):
    d_ff = w1_ref.shape[1]
    for c in range(d_ff // FF_CHUNK):
        a = _dot(h2_ref[...], w1_ref[:, c * FF_CHUNK:(c + 1) * FF_CHUNK])
        r = jnp.square(jnp.maximum(a, 0.0)).astype(BF16)
        part = _dot(r, w2_ref[c * FF_CHUNK:(c + 1) * FF_CHUNK, :])
        if c == 0:
            acc_ref[...] = part
        else:
            acc_ref[...] += part


N_MIX_PARAMS = 7
N_MIX_TOKEN_INPUTS = 6


def _mix_ffn_kernel(*refs, layout, latent_only, split):
    n_x = 2 if split else 1
    per_half = n_x + N_MIX_TOKEN_INPUTS
    n_in = TILES_PER_STEP * per_half
    normg, wout, gpost, gpre2, w1, w2, gpost2 = [r.at[0] for r in refs[n_in:n_in + N_MIX_PARAMS]]
    o_ref = refs[n_in + N_MIX_PARAMS]
    scratch = refs[n_in + N_MIX_PARAMS + 1:]
    per_tile = len(scratch) // TILES_PER_STEP
    tiles = [scratch[h * per_tile:(h + 1) * per_tile] for h in range(TILES_PER_STEP)]
    tm = TOKEN_TILE
    halves = [refs[h * per_half:(h + 1) * per_half] for h in range(TILES_PER_STEP)]
    mods = [half[n_x].at[0] for half in halves]
    for half, (ycat_ref, y_ref, _, _, _) in zip(halves, tiles):
        _mix_gather(*half[n_x + 1:], normg, ycat_ref)
        y_ref[...] = _dot(ycat_ref[...], wout[...])
    for h, (half, (_, y_ref, acc_ref, x1_ref, h2_ref)) in enumerate(zip(halves, tiles)):
        u = layout.unified_tile(pl.program_id(0) * TILES_PER_STEP + h, latent_only)
        x = layout.select_source(u, half[:n_x])
        _mix_residual(x, mods[h], y_ref, gpost, gpre2, x1_ref, h2_ref)
        _mix_mlp(h2_ref, w1, w2, acc_ref)
    for h, (_, _, acc_ref, x1_ref, _) in enumerate(tiles):
        gate2 = mods[h][0, 5:6, :]
        o_ref[h * tm:(h + 1) * tm, :] = x1_ref[...] + gate2 * _rms(acc_ref[...], gpost2[...])


def _mix_ffn(src, mod, layer, att, ysc, z, yf, yb, normg, wout, gpost, gpre2, w1, w2, gpost2, layout,
             latent_only):
    tm = TOKEN_TILE
    d = src[0].shape[-1]
    n_out_tiles = layout.bsz * (layout.lat_tiles if latent_only else layout.tiles_per_batch)
    assert n_out_tiles % TILES_PER_STEP == 0
    att_unified = att.shape[0] == layout.bsz * layout.tiles_per_batch * tm

    in_specs, operands = [], []
    for h in range(TILES_PER_STEP):
        def u_of(j, h=h):
            return layout.unified_tile(j * TILES_PER_STEP + h, latent_only)

        def tok(width, tile_of=u_of):
            return pl.BlockSpec((tm, width), lambda j: (tile_of(j), 0))

        in_specs += layout.source_specs(u_of, d, len(src) == 2)
        in_specs += [
            pl.BlockSpec((1, 1, N_MOD, d), lambda j, u_of=u_of: (layer, layout.mod_row(u_of(j)), 0, 0)),
            tok(MLA_WIDTH) if att_unified else tok(MLA_WIDTH, lambda j, h=h: j * TILES_PER_STEP + h),
            tok(SC_WIDTH), tok(SSD_WIDTH), tok(SSD_WIDTH), tok(SSD_WIDTH),
        ]
        operands += list(src) + [mod, att, ysc, z, yf, yb]
    consts = [normg, wout, gpost, gpre2, w1, w2, gpost2]
    in_specs += [_layer_spec(c, layer, single_buffer=True) for c in consts]
    return pl.pallas_call(
        functools.partial(_mix_ffn_kernel, layout=layout, latent_only=latent_only, split=len(src) == 2),
        out_shape=jax.ShapeDtypeStruct((n_out_tiles * tm, d), F32),
        grid=(n_out_tiles // TILES_PER_STEP,),
        in_specs=in_specs,
        out_specs=pl.BlockSpec((TILES_PER_STEP * tm, d), lambda j: (j, 0)),
        scratch_shapes=[pltpu.VMEM((tm, d), BF16), pltpu.VMEM((tm, d), F32), pltpu.VMEM((tm, d), F32),
                        pltpu.VMEM((tm, d), F32), pltpu.VMEM((tm, d), BF16)] * TILES_PER_STEP,
        compiler_params=pltpu.CompilerParams(
            dimension_semantics=("parallel",), vmem_limit_bytes=VMEM_LIMIT_BYTES),
        name="mix_ffn",
    )(*operands, *consts)


def _rope_tables(n_ctx, seq):
    f32 = np.float32
    half = QK_ROPE // 2
    inv_freq = (f32(ROPE_THETA) ** (-np.arange(0, half, 2, dtype=f32) / f32(half))).astype(f32)
    rows = seq // GRID_W
    row = np.repeat(np.arange(rows, dtype=f32), GRID_W)
    col = np.tile(np.arange(GRID_W, dtype=f32), rows)
    ang_r = row[:, None] * inv_freq
    ang_c = col[:, None] * inv_freq
    ang = np.concatenate([ang_r, ang_r, ang_c, ang_c], axis=-1).astype(f32)
    t = n_ctx + seq
    cos_t = np.ones((t, LANES), f32)
    sin_t = np.zeros((t, LANES), f32)
    cos_t[n_ctx:, ROPE_LANE0:ROPE_LANE0 + QK_ROPE] = np.cos(ang)
    sin_t[n_ctx:, ROPE_LANE0:ROPE_LANE0 + QK_ROPE] = np.sin(ang)
    return jnp.asarray(cos_t), jnp.asarray(sin_t)


def kernel(x, c, ctx, c_ctx, w_mod, b_mod, g_pre_mix, w_in, mla_q_norm, w_uq, mla_kv_norm, w_ukv, sc_conv_w, ssd_conv_w, ssd_conv_b, ssd_a_log, ssd_dt_bias, ssd_d, ssd_norm, w_out, g_post_mix, g_pre_ffn, w_ff1, w_ff2, g_post_ffn):
    bsz, seq, d = x.shape
    n_ctx = ctx.shape[1]
    depth = w_mod.shape[0]
    assert n_ctx == TOKEN_TILE and seq % TOKEN_TILE == 0 and seq % GRID_W == 0
    layout = _TokenLayout(bsz, n_ctx, seq)
    t = n_ctx + seq

    cvec = jnp.zeros((SUBLANES, d), F32).at[:bsz].set(c).at[layout.ctx_row].set(c_ctx)
    mod_all = _modulation(cvec, w_mod, b_mod).reshape(depth, SUBLANES, N_MOD, d)
    cos_t, sin_t = _rope_tables(n_ctx, seq)

    src = (ctx.reshape(bsz * n_ctx, d), x.reshape(bsz * seq, d))

    def rows3(a):
        return a.reshape(depth, 1, -1)

    def dt_lanes(a):
        flat = a.reshape(depth, 1, -1)
        return jnp.pad(flat, ((0, 0), (0, 0), (DT_LANE0, LANES - DT_LANE0 - flat.shape[-1])))

    alog, dtbias = dt_lanes(ssd_a_log), dt_lanes(ssd_dt_bias)
    dskip = rows3(jnp.repeat(ssd_d, SSD_HEAD_DIM, axis=-1))
    w_out_b, w_ff1_b, w_ff2_b = w_out.astype(BF16), w_ff1.astype(BF16), w_ff2.astype(BF16)
    w_in_t = jnp.swapaxes(w_in, 1, 2)

    def per_batch(a):
        return a.reshape((bsz, t) + a.shape[1:])

    for i in range(depth):
        last = i == depth - 1
        qt, k, vt, kn, qn, ysc, z, act, cs, side = _inproj(
            src, mod_all, i, rows3(g_pre_mix), w_in_t, rows3(mla_q_norm), w_uq, rows3(mla_kv_norm), w_ukv,
            cos_t, sin_t, sc_conv_w, ssd_conv_w, rows3(ssd_conv_b), alog, dtbias, layout)
        att = _attention(qt, per_batch(k), vt, kn.reshape(bsz, layout.tiles_per_batch, SUBLANES, LANES), qn,
                         n_ctx, ctx_queries=not last)
        yf, yb = _ssd_scan(per_batch(act), per_batch(cs), side.reshape(bsz, t // SSD_CHUNK, SIDE_ROWS, SSD_CHUNK),
                           i, dskip, n_ctx)
        out = _mix_ffn(src, mod_all, i, att.reshape(-1, MLA_WIDTH), ysc, z,
                       yf.reshape(-1, SSD_WIDTH), yb.reshape(-1, SSD_WIDTH), rows3(ssd_norm),
                       w_out_b, rows3(g_post_mix), rows3(g_pre_ffn), w_ff1_b, w_ff2_b, rows3(g_post_ffn),
                       layout, latent_only=last)
        src = (out,)
    return out.reshape(bsz, seq, d)
```

```python
import functools
import math

import jax
import jax.numpy as jnp
import numpy as np
from jax import lax
from jax.experimental import pallas as pl
from jax.experimental.pallas import tpu as pltpu

F32 = jnp.float32
BF16 = jnp.bfloat16

GRID_W = 64
EPS = 1e-6
N_MOD = 6
MLA_HEADS = 4
Q_LORA = 256
KV_LORA = 128
QK_NOPE = 64
QK_ROPE = 32
V_HEAD = 64
MLA_WIDTH = MLA_HEADS * V_HEAD
MLA_SCALE = (QK_NOPE + QK_ROPE) ** -0.5
ROPE_THETA = 10000.0
SC_WIDTH = 256
SSD_HEADS = 8
SSD_HEAD_DIM = 64
SSD_WIDTH = SSD_HEADS * SSD_HEAD_DIM
SSD_GROUPS = 2
SSD_STATE = 64
SSD_CHUNK = 128
SSD_GN = SSD_GROUPS * SSD_STATE
SSD_XBC = SSD_WIDTH + 2 * SSD_GN
HEADS_PER_GROUP = SSD_HEADS // SSD_GROUPS
IN_MLA = Q_LORA + KV_LORA + QK_ROPE
IN_SC = 3 * SC_WIDTH

LANES = 128
SUBLANES = 8
VMEM_LIMIT_BYTES = 56 * 1024 * 1024

HEAD_PAD = LANES
V_ROWS = V_HEAD + 16
ROPE_LANE0 = QK_NOPE
TOKEN_TILE = 256
TILES_PER_STEP = 2
INPROJ_TILES_PER_STEP = 4
SSD_BLOCK = 2 * SSD_CHUNK
FF_CHUNK = 1024

C_Q0 = 0
C_KV0 = C_Q0 + Q_LORA
C_KR0 = C_KV0 + KV_LORA
C_SC0 = C_KR0 + LANES
C_Z0 = C_SC0 + IN_SC
C_XBC0 = C_Z0 + SSD_WIDTH
D_IN_PAD = C_XBC0 + SSD_XBC
DT_LANE0 = QK_ROPE

LOG2E = math.log2(math.e)
Q_PRESCALE = MLA_SCALE * LOG2E

SHIFT_HEADROOM = 60.0
SHIFT_MAX_BOUND = 90.0
BOUND_SLACK = 1.0 + 2.0 ** -8


def _rms(x, g):
    return x * lax.rsqrt(jnp.mean(x * x, axis=-1, keepdims=True) + EPS) * g


def _silu(x):
    return x / (1.0 + jnp.exp(-x))


def _dot(a, b):
    return jnp.dot(a, b, preferred_element_type=F32)


class _TokenLayout:
    def __init__(self, bsz, n_ctx, seq):
        tm = TOKEN_TILE
        assert n_ctx % tm == 0 and seq % tm == 0
        self.bsz = bsz
        self.ctx_tiles = n_ctx // tm
        self.lat_tiles = seq // tm
        self.tiles_per_batch = self.ctx_tiles + self.lat_tiles
        self.n_tiles = bsz * self.tiles_per_batch
        self.ctx_row = bsz

    def unified_tile(self, o, latent_only):
        if not latent_only:
            return o
        return o + (o // self.lat_tiles + 1) * self.ctx_tiles

    def coords(self, u):
        b = u // self.tiles_per_batch
        return b, u - b * self.tiles_per_batch

    def mod_row(self, u):
        b, i = self.coords(u)
        return jnp.where(i < self.ctx_tiles, self.ctx_row, b)

    def seq_edges(self, u):
        _, i = self.coords(u)
        first = (i == 0) | (i == self.ctx_tiles)
        last = (i == self.ctx_tiles - 1) | (i == self.tiles_per_batch - 1)
        return first, last

    def _latent_tile(self, u):
        b, i = self.coords(u)
        return b * self.lat_tiles + jnp.maximum(i - self.ctx_tiles, 0)

    def _context_tile(self, u):
        b, i = self.coords(u)
        return b * self.ctx_tiles + jnp.minimum(i, self.ctx_tiles - 1)

    def source_specs(self, u_of, d, split):
        tm = TOKEN_TILE
        if not split:
            return [pl.BlockSpec((tm, d), lambda j: (u_of(j), 0))]
        return [pl.BlockSpec((tm, d), lambda j: (self._context_tile(u_of(j)), 0)),
                pl.BlockSpec((tm, d), lambda j: (self._latent_tile(u_of(j)), 0))]

    def halo_specs(self, u_of, d, split):
        rb = TOKEN_TILE // SUBLANES
        if split:
            assert self.ctx_tiles == 1
            tile_of, n_rb = (lambda j: self._latent_tile(u_of(j))), self.bsz * self.lat_tiles * rb
        else:
            tile_of, n_rb = u_of, self.n_tiles * rb
        return [pl.BlockSpec((SUBLANES, d), lambda j: (jnp.maximum(tile_of(j) * rb - 1, 0), 0)),
                pl.BlockSpec((SUBLANES, d), lambda j: (jnp.minimum((tile_of(j) + 1) * rb, n_rb - 1), 0))]

    def select_source(self, u, refs):
        if len(refs) == 1:
            return refs[0][...]
        _, i = self.coords(u)
        return jnp.where(i < self.ctx_tiles, refs[0][...], refs[1][...])


def _mod_kernel(c_ref, w_ref, b_ref, o_ref):
    s = _silu(c_ref[...]).astype(BF16)
    o_ref[0] = _dot(s, w_ref[0].astype(BF16)) + b_ref[0]


def _modulation(cvec, w_mod, b_mod):
    depth, d, nd = w_mod.shape
    rows = cvec.shape[0]
    return pl.pallas_call(
        _mod_kernel,
        out_shape=jax.ShapeDtypeStruct((depth, rows, nd), F32),
        grid=(depth, nd // d),
        in_specs=[
            pl.BlockSpec((rows, d), lambda l, j: (0, 0)),
            pl.BlockSpec((1, d, d), lambda l, j: (l, 0, j)),
            pl.BlockSpec((1, 1, d), lambda l, j: (l, 0, j)),
        ],
        out_specs=pl.BlockSpec((1, rows, d), lambda l, j: (l, 0, j)),
        compiler_params=pltpu.CompilerParams(dimension_semantics=("parallel", "parallel")),
        name="modulation",
    )(cvec, w_mod, b_mod.reshape(depth, 1, nd))


def _conv3_rows(a_ext, w_ref, tm):
    te = a_ext.shape[0]
    lo, hi = SUBLANES, SUBLANES + tm
    prev = pltpu.roll(a_ext, 1, 0)[lo:hi]
    nxt = pltpu.roll(a_ext, te - 1, 0)[lo:hi]
    return w_ref[0:1, :] * prev + w_ref[1:2, :] * a_ext[lo:hi] + w_ref[2:3, :] * nxt


N_INPROJ_PARAMS = 11
N_INPROJ_TOKEN_INPUTS = 5


def _split3(x):
    hi = x.astype(BF16)
    r = x - hi.astype(F32)
    mid = r.astype(BF16)
    lo = (r - mid.astype(F32)).astype(BF16)
    return hi, mid, lo


SIDE_ROWS = 2 * 2 * SSD_HEADS


def _ssd_decay_tables(dt_raw, alog_ref, dtbias_ref, cs_ref, side_ref, slot):
    tc = SSD_CHUNK
    tm = dt_raw.shape[0]
    nd = 2 * SSD_HEADS
    a = -jnp.exp(alog_ref[...])
    xb = dt_raw + dtbias_ref[...]
    dt = jnp.maximum(xb, 0.0) + jnp.log(1.0 + jnp.exp(-jnp.abs(xb)))
    la = dt * a * LOG2E
    r_i = lax.broadcasted_iota(jnp.int32, (tc, tc), 0)
    c_i = lax.broadcasted_iota(jnp.int32, (tc, tc), 1)
    tri_f = jnp.where(c_i <= r_i, 1.0, 0.0).astype(BF16)
    tri_b = jnp.where(c_i >= r_i, 1.0, 0.0).astype(BF16)
    lane = lax.broadcasted_iota(jnp.int32, (1, LANES), 1)
    fwd_lane = lane < DT_LANE0 + SSD_HEADS
    row = lax.broadcasted_iota(jnp.int32, (nd, 1), 0)
    for c in range(tm // tc):
        parts = _split3(la[c * tc:(c + 1) * tc])
        cs_f = sum(_dot(tri_f, p) for p in parts)
        cs_b = sum(_dot(tri_b, p) for p in parts)
        cs = jnp.where(fwd_lane, cs_f, cs_b)
        cs_ref[slot * tm + c * tc:slot * tm + (c + 1) * tc, :] = cs
        cs_t = cs.T[DT_LANE0:DT_LANE0 + nd]
        dt_t = dt[c * tc:(c + 1) * tc].T[DT_LANE0:DT_LANE0 + nd]
        cs_end = jnp.where(row < SSD_HEADS, cs_t[:, tc - 1:tc], cs_t[:, 0:1])
        w_t = jnp.exp2(cs_end - cs_t) * dt_t
        side_ref[slot * (tm // tc) + c] = jnp.concatenate([cs_t - jnp.log(dt_t) * LOG2E, w_t], axis=0)


def _rotate_cols(w):
    a, b, c, d = jnp.split(w, 4, axis=-1)
    return jnp.concatenate([-b, a, -d, c], axis=-1)


def _relayout_weights(win_raw, wuq_raw, wukv_raw, win_ref, wuq_ref, wukv_ref):
    d = win_raw.shape[1]
    w_kr = win_raw[Q_LORA + KV_LORA:IN_MLA, :]
    w_dt = win_raw[IN_MLA + IN_SC + SSD_WIDTH + SSD_XBC:, :]
    a, b, c, e = (w_kr[i * SUBLANES:(i + 1) * SUBLANES] for i in range(4))
    kr_block = jnp.concatenate([
        -b, a, -e, c,
        w_dt, jnp.zeros((LANES // 2 - QK_ROPE - w_dt.shape[0], d), F32),
        w_kr, jnp.zeros((LANES // 2 - QK_ROPE, d), F32)], axis=0)
    win_ref[:, C_KR0:C_SC0] = kr_block.T.astype(BF16)
    step = 2 * LANES
    for src0, dst0, n_rows in ((0, C_Q0, Q_LORA + KV_LORA), (IN_MLA, C_SC0, IN_SC + SSD_WIDTH + SSD_XBC)):
        for off in range(0, n_rows, step):
            rows = min(step, n_rows - off)
            blk = win_raw[src0 + off:src0 + off + rows, :]
            win_ref[:, dst0 + off:dst0 + off + rows] = blk.T.astype(BF16)

    dqk = QK_NOPE + QK_ROPE
    wq = wuq_raw[...]
    nq = wq.shape[0]
    plain, rot = [], []
    for h in range(MLA_HEADS):
        wh = wq[:, h * dqk:(h + 1) * dqk]
        pad = jnp.zeros((nq, HEAD_PAD - dqk), F32)
        plain += [wh, pad]
        rot += [jnp.zeros((nq, QK_NOPE), F32), _rotate_cols(wh[:, QK_NOPE:]), pad]
    wuq_ref[...] = jnp.concatenate(plain + rot, axis=1).astype(BF16)

    dkv = QK_NOPE + V_HEAD
    wkv = wukv_raw[...]
    nkv = wkv.shape[0]
    ks, vs = [], []
    for h in range(MLA_HEADS):
        wh = wkv[:, h * dkv:(h + 1) * dkv]
        ks += [wh[:, :QK_NOPE], jnp.zeros((nkv, HEAD_PAD - QK_NOPE), F32)]
        vs.append(wh[:, QK_NOPE:])
    wukv_ref[...] = jnp.concatenate(ks + vs, axis=1).astype(BF16)


def _inproj_kernel(*refs, layout, split, tiles):
    n_x = 2 if split else 1
    per_half = n_x + N_INPROJ_TOKEN_INPUTS
    n_in = tiles * per_half
    (gpre, win_raw, qg, wuq_raw, kvg, wukv_raw, scw, cw, cb, alog, dtbias) = [
        r.at[0] for r in refs[n_in:n_in + N_INPROJ_PARAMS]]
    n_out = len(refs) - n_in - N_INPROJ_PARAMS - 3
    outs = refs[n_in + N_INPROJ_PARAMS:n_in + N_INPROJ_PARAMS + n_out]
    win_ref, wuq_ref, wukv_ref = refs[n_in + N_INPROJ_PARAMS + n_out:]

    @pl.when(pl.program_id(0) == 0)
    def _():
        _relayout_weights(win_raw, wuq_raw, wukv_raw, win_ref, wuq_ref, wukv_ref)

    params = (qg, wuq_ref, kvg, wukv_ref, scw, cw, cb, alog, dtbias)
    for h in range(tiles):
        half = refs[h * per_half:(h + 1) * per_half]
        u = pl.program_id(0) * tiles + h
        x = layout.select_source(u, half[:n_x])
        xp_ref, xn_ref, mod_ref, cos_ref, sin_ref = half[n_x:]
        h_ext = _inproj_normalise(x, xp_ref, xn_ref, mod_ref.at[0], gpre, layout.seq_edges(u))
        u_ext = _dot(h_ext, win_ref[...])
        _inproj_finish(u_ext, cos_ref, sin_ref, params, outs, h)


def _inproj_normalise(x, xp_ref, xn_ref, mod_ref, gpre_ref, seq_edges):
    first_of_seq, last_of_seq = seq_edges
    shift = mod_ref[0, 0:1, :]
    scale = mod_ref[0, 1:2, :]

    def norm_mod(xx):
        return _rms(xx, gpre_ref[...]) * (1.0 + scale) + shift

    h_prev = jnp.where(first_of_seq, 0.0, norm_mod(xp_ref[...]))
    h_next = jnp.where(last_of_seq, 0.0, norm_mod(xn_ref[...]))
    return jnp.concatenate([h_prev, norm_mod(x), h_next], axis=0).astype(BF16)


def _inproj_finish(u_ext, cos_ref, sin_ref, params, outs, slot):
    qg_ref, wuq_ref, kvg_ref, wukv_ref, scw_ref, cw_ref, cb_ref, alog_ref, dtbias_ref = params
    qt_ref, k_ref, vt_ref, kn_ref, qn_ref, ysc_ref, z_ref, act_ref, cs_ref, side_ref = outs
    tm = u_ext.shape[0] - 2 * SUBLANES
    rows = slice(slot * tm, (slot + 1) * tm)
    own = slice(SUBLANES, SUBLANES + tm)

    z_ref[rows, :] = u_ext[own, C_Z0:C_XBC0].astype(z_ref.dtype)
    lane = lax.broadcasted_iota(jnp.int32, (1, LANES), 1)
    krb = u_ext[own, C_KR0:C_SC0]
    dt_raw = jnp.where((lane >= DT_LANE0) & (lane < DT_LANE0 + 2 * SSD_HEADS), krb, 0.0)
    _ssd_decay_tables(dt_raw, alog_ref, dtbias_ref, cs_ref, side_ref, slot)

    conv = _conv3_rows(u_ext[:, C_XBC0:D_IN_PAD], cw_ref, tm) + cb_ref[...]
    act_ref[rows, :] = _silu(conv).astype(act_ref.dtype)

    w = SC_WIDTH
    prod = u_ext[:, C_SC0 + w:C_SC0 + 2 * w] * u_ext[:, C_SC0 + 2 * w:C_Z0]
    ysc_ref[rows, :] = (u_ext[own, C_SC0:C_SC0 + w] * _conv3_rows(prod, scw_ref, tm)).astype(ysc_ref.dtype)

    cos = cos_ref[...]
    sin = sin_ref[...]
    nh = MLA_HEADS
    cos_h = jnp.concatenate([cos] * nh, axis=1)
    sin_h = jnp.concatenate([sin] * nh, axis=1)

    def head_norm_maxima(a_f32):
        out = []
        for h in range(nh):
            a_h = a_f32[:, h * HEAD_PAD:(h + 1) * HEAD_PAD]
            n2 = jnp.max(jnp.sum(a_h * a_h, axis=1, keepdims=True), axis=0, keepdims=True)
            out.append(jnp.broadcast_to(n2, (1, LANES)))
        return jnp.concatenate(out + [jnp.zeros((SUBLANES - nh, LANES), F32)], axis=0)

    cq = _rms(u_ext[own, C_Q0:C_KV0], qg_ref[...]).astype(BF16)
    q2 = _dot(cq, wuq_ref[...])
    qw = nh * HEAD_PAD
    q_b = ((q2[:, :qw] * cos_h + q2[:, qw:] * sin_h) * Q_PRESCALE).astype(BF16)
    q_f = q_b.astype(F32)
    qt_ref[:, rows] = q_f.T.astype(BF16)
    qn_ref[slot] = head_norm_maxima(q_f)

    ckv = _rms(u_ext[own, C_KV0:C_KR0], kvg_ref[...]).astype(BF16)
    kv = _dot(ckv, wukv_ref[...])
    rope_lane = (lane >= ROPE_LANE0) & (lane < ROPE_LANE0 + QK_ROPE)
    cos_k = jnp.where(rope_lane, cos, 0.0)
    kr = krb * cos_k + pltpu.roll(krb, LANES // 2, 1) * sin
    k = kv[:, :qw] + jnp.concatenate([kr] * nh, axis=1)
    k_b = k.astype(BF16)
    k_ref[rows, :] = k_b
    kn_ref[slot] = head_norm_maxima(k_b.astype(F32))
    v_t = kv[:, qw:].T.astype(BF16)
    for h in range(nh):
        vt_ref[h, 0:V_HEAD, rows] = v_t[h * V_HEAD:(h + 1) * V_HEAD]
        vt_ref[h, V_HEAD:V_ROWS, rows] = jnp.ones((V_ROWS - V_HEAD, tm), BF16)


def _layer_spec(a, layer, single_buffer=False):
    block = (1,) + a.shape[1:]
    index_map = lambda j: (layer,) + (0,) * (a.ndim - 1)
    if single_buffer:
        return pl.BlockSpec(block, index_map, pipeline_mode=pl.Buffered(1))
    return pl.BlockSpec(block, index_map)


def _inproj(src, mod, layer, gpre, win, qg, wuq, kvg, wukv, cos_t, sin_t, scw, cw, cb, alog, dtbias, layout):
    tm = TOKEN_TILE
    d = src[0].shape[-1]
    split = len(src) == 2
    qw = MLA_HEADS * HEAD_PAD
    nt = layout.n_tiles
    tiles = INPROJ_TILES_PER_STEP
    assert nt % tiles == 0
    rows = nt * tm

    in_specs, operands = [], []
    for h in range(tiles):
        def u_of(j, h=h):
            return j * tiles + h

        def rope(j, u_of=u_of):
            return (layout.coords(u_of(j))[1], 0)

        in_specs += layout.source_specs(u_of, d, split) + layout.halo_specs(u_of, d, split)
        in_specs += [
            pl.BlockSpec((1, 1, N_MOD, d), lambda j, u_of=u_of: (layer, layout.mod_row(u_of(j)), 0, 0)),
            pl.BlockSpec((tm, LANES), rope), pl.BlockSpec((tm, LANES), rope),
        ]
        operands += list(src) + [src[-1], src[-1], mod, cos_t, sin_t]
    consts = [gpre, win, qg, wuq, kvg, wukv, scw, cw, cb, alog, dtbias]
    in_specs += [_layer_spec(c, layer, single_buffer=c is win) for c in consts]
    step_rows = tiles * tm
    chunks_per_step = step_rows // SSD_CHUNK

    def tok(width):
        return pl.BlockSpec((step_rows, width), lambda j: (j, 0))

    norm_shape = jax.ShapeDtypeStruct((nt, SUBLANES, LANES), F32)
    norm_spec = pl.BlockSpec((tiles, SUBLANES, LANES), lambda j: (j, 0, 0))
    out_shape = (
        jax.ShapeDtypeStruct((qw, rows), BF16),
        jax.ShapeDtypeStruct((rows, qw), BF16),
        jax.ShapeDtypeStruct((MLA_HEADS, V_ROWS, rows), BF16),
        norm_shape,
        norm_shape,
        jax.ShapeDtypeStruct((rows, SC_WIDTH), BF16),
        jax.ShapeDtypeStruct((rows, SSD_WIDTH), BF16),
        jax.ShapeDtypeStruct((rows, SSD_XBC), BF16),
        jax.ShapeDtypeStruct((rows, LANES), F32),
        jax.ShapeDtypeStruct((rows // SSD_CHUNK, SIDE_ROWS, SSD_CHUNK), F32),
    )
    return pl.pallas_call(
        functools.partial(_inproj_kernel, layout=layout, split=split, tiles=tiles),
        out_shape=out_shape,
        grid=(nt // tiles,),
        in_specs=in_specs,
        out_specs=(pl.BlockSpec((qw, step_rows), lambda j: (0, j)), tok(qw),
                   pl.BlockSpec((MLA_HEADS, V_ROWS, step_rows), lambda j: (0, 0, j)),
                   norm_spec, norm_spec,
                   tok(SC_WIDTH), tok(SSD_WIDTH), tok(SSD_XBC), tok(LANES),
                   pl.BlockSpec((chunks_per_step, SIDE_ROWS, SSD_CHUNK), lambda j: (j, 0, 0))),
        scratch_shapes=[pltpu.VMEM((d, D_IN_PAD), BF16),
                        pltpu.VMEM((Q_LORA, 2 * qw), BF16),
                        pltpu.VMEM((KV_LORA, qw + MLA_WIDTH), BF16)],
        compiler_params=pltpu.CompilerParams(
            dimension_semantics=("arbitrary",), vmem_limit_bytes=VMEM_LIMIT_BYTES),
        name="inproj",
    )(*operands, *consts)


def _attn_kernel(*refs, n_keys, n_ctx, ctx_queries, q_tiles):
    qt_refs = refs[:q_tiles]
    k_ref, vt_ref, kn_ref = refs[q_tiles:q_tiles + 3]
    qn_refs = refs[q_tiles + 3:2 * q_tiles + 3]
    o_ref, s_ref, p_ref = refs[2 * q_tiles + 3:]
    i = pl.program_id(1)
    kn = jnp.max(kn_ref[0], axis=0)
    for t in range(q_tiles):
        _attend_tile(qt_refs[t], qn_refs[t], k_ref, vt_ref, kn, o_ref, t, s_ref, p_ref, i,
                     n_keys=n_keys, n_ctx=n_ctx, ctx_queries=ctx_queries)


def _attend_tile(q_t, qn_ref, k_ref, vt_ref, kn, o_ref, slot, s_ref, p_ref, i, *, n_keys, n_ctx, ctx_queries):
    tq = q_t.shape[1]

    def finish(outs):
        o_ref[0, slot * tq:(slot + 1) * tq, :] = jnp.concatenate(outs, axis=0).T.astype(o_ref.dtype)

    def head_out(ov):
        return ov[0:V_HEAD] / ov[V_HEAD:V_HEAD + 1]

    def attend_two_pass(nk):
        def scores(h):
            s_ref[h, 0:nk, :] = _dot(k_ref[0, 0:nk, h * HEAD_PAD:(h + 1) * HEAD_PAD],
                                     q_t[h * HEAD_PAD:(h + 1) * HEAD_PAD, :])

        def probs(h):
            s = s_ref[h, 0:nk, :]
            p_ref[h, 0:nk, :] = jnp.exp2(s - jnp.max(s, axis=0, keepdims=True)).astype(BF16)

        def values(h):
            return head_out(_dot(vt_ref[h, :, 0:nk], p_ref[h, 0:nk, :]))

        scores(0), scores(1)
        scores(2), scores(3), probs(0), probs(1)
        outs = [values(0), values(1)]
        probs(2), probs(3)
        outs += [values(2), values(3)]
        finish(outs)

    def attend_one_pass(nk, shift):
        outs = []
        for pair in range(MLA_HEADS // 2):
            heads = (2 * pair, 2 * pair + 1)
            s = [_dot(k_ref[0, 0:nk, h * HEAD_PAD:(h + 1) * HEAD_PAD],
                      q_t[h * HEAD_PAD:(h + 1) * HEAD_PAD, :]) for h in heads]
            p = [jnp.exp2(sh - shift[h]).astype(BF16) for h, sh in zip(heads, s)]
            outs += [head_out(_dot(vt_ref[h, :, 0:nk], ph)) for h, ph in zip(heads, p)]
        finish(outs)

    def attend(nk):
        bound = jnp.sqrt(qn_ref[0] * kn) * BOUND_SLACK
        one_pass = jnp.max(bound) <= SHIFT_MAX_BOUND

        @pl.when(one_pass)
        def _():
            attend_one_pass(nk, [bound[h:h + 1, 0:1] - SHIFT_HEADROOM for h in range(MLA_HEADS)])

        @pl.when(jnp.logical_not(one_pass))
        def _():
            attend_two_pass(nk)

    if ctx_queries:
        @pl.when(i == 0)
        def _():
            attend_two_pass(n_ctx)

        @pl.when(i > 0)
        def _():
            attend(n_keys)
    else:
        attend(n_keys)


def _attention(qt, k, vt, kn, qn, n_ctx, ctx_queries):
    bsz, t, qw = k.shape
    tq = TOKEN_TILE
    tiles = t // tq
    t0 = 0 if ctx_queries else n_ctx // tq
    nq = tiles - t0
    q_tiles = 1 if ctx_queries else 2
    assert nq % q_tiles == 0

    def tile_of(b, i, s):
        return b * tiles + i * q_tiles + s + t0

    return pl.pallas_call(
        functools.partial(_attn_kernel, n_keys=t, n_ctx=n_ctx, ctx_queries=ctx_queries, q_tiles=q_tiles),
        out_shape=jax.ShapeDtypeStruct((bsz, nq * tq, MLA_WIDTH), BF16),
        grid=(bsz, nq // q_tiles),
        in_specs=[pl.BlockSpec((qw, tq), lambda b, i, s=s: (0, tile_of(b, i, s))) for s in range(q_tiles)] + [
            pl.BlockSpec((1, t, qw), lambda b, i: (b, 0, 0)),
            pl.BlockSpec((MLA_HEADS, V_ROWS, t), lambda b, i: (0, 0, b)),
            pl.BlockSpec((1,) + kn.shape[1:], lambda b, i: (b, 0, 0, 0)),
        ] + [pl.BlockSpec((1, SUBLANES, LANES), lambda b, i, s=s: (tile_of(b, i, s), 0, 0)) for s in range(q_tiles)],
        out_specs=pl.BlockSpec((1, q_tiles * tq, MLA_WIDTH), lambda b, i: (b, i, 0)),
        scratch_shapes=[pltpu.VMEM((MLA_HEADS, t, tq), F32), pltpu.VMEM((MLA_HEADS, t, tq), BF16)],
        compiler_params=pltpu.CompilerParams(
            dimension_semantics=("parallel", "parallel"), vmem_limit_bytes=VMEM_LIMIT_BYTES),
        name="attention",
    )(*([qt] * q_tiles), k, vt, kn, *([qn] * q_tiles))


def _ssd_role(act, cs, side, direction, dskip_ref, y_ref, row0, h_ref):
    tc = SSD_CHUNK
    n = SSD_STATE
    xs = act[:, :SSD_WIDTH]
    bm = act[:, SSD_WIDTH:SSD_WIDTH + SSD_GN].astype(F32)
    cm = act[:, SSD_WIDTH + SSD_GN:]
    bm_t = bm.T

    r_i = lax.broadcasted_iota(jnp.int32, (tc, tc), 0)
    c_i = lax.broadcasted_iota(jnp.int32, (tc, tc), 1)
    tri = (c_i <= r_i) if direction == 0 else (c_i >= r_i)
    end = tc - 1 if direction == 0 else 0
    nd = 2 * SSD_HEADS
    csd_t, w_t = side[0:nd], side[nd:2 * nd]
    dec_tot = jnp.exp2(cs[end:end + 1, :])
    cm_f = cm.astype(F32)
    lane = lax.broadcasted_iota(jnp.int32, (1, LANES), 1)
    low_half = lane < SSD_HEAD_DIM
    zeros_h = jnp.zeros((n, LANES), BF16)
    zeros_lhs = jnp.zeros((n, tc), BF16)

    for g in range(SSD_GROUPS):
        cb = _dot(cm[:, g * n:(g + 1) * n], bm_t[g * n:(g + 1) * n, :].astype(BF16))
        bt_g = bm_t[g * n:(g + 1) * n, :]
        for pr in range(HEADS_PER_GROUP // 2):
            slab = g * (HEADS_PER_GROUP // 2) + pr
            h_pair = h_ref[direction, g, :, pr * LANES:(pr + 1) * LANES]
            h_b = h_pair.astype(BF16)
            xs_pair = xs[:, slab * LANES:(slab + 1) * LANES]
            rhs = jnp.concatenate([xs_pair] + [h_b if gg == g else zeros_h for gg in range(SSD_GROUPS)], axis=0)
            lhs_rows = []
            cols = []
            for hh in range(2):
                r = direction * SSD_HEADS + 2 * slab + hh
                col = DT_LANE0 + r
                cols.append(col)
                cs_col = jnp.broadcast_to(cs[:, col:col + 1], (tc, tc))
                lmd = jnp.exp2(jnp.where(tri, cs_col - csd_t[r:r + 1, :], -jnp.inf))
                top = jnp.concatenate([(cb * lmd).astype(BF16), (cm_f * jnp.exp2(cs_col)).astype(BF16)], axis=1)
                bot = jnp.concatenate([(bt_g * w_t[r:r + 1, :]).astype(BF16), zeros_lhs], axis=1)
                lhs_rows += [top, bot]
            lhs = jnp.concatenate(lhs_rows, axis=0)
            out = _dot(lhs, rhs)
            m = tc + n
            y_pair = jnp.where(low_half, out[0:tc], out[m:m + tc])
            if direction == 0:
                y_pair = y_pair + dskip_ref[:, slab * LANES:(slab + 1) * LANES] * xs_pair.astype(F32)
            y_ref[0, row0:row0 + tc, slab * LANES:(slab + 1) * LANES] = y_pair
            h_new = jnp.where(low_half, out[tc:m], out[m + tc:2 * m])
            keep = jnp.where(low_half, dec_tot[:, cols[0]:cols[0] + 1], dec_tot[:, cols[1]:cols[1] + 1])
            h_ref[direction, g, :, pr * LANES:(pr + 1) * LANES] = h_pair * keep + h_new


def _bwd_block(i, n_blocks, ctx_blocks):
    return jnp.where(i < ctx_blocks, ctx_blocks - 1 - i, n_blocks - 1 + ctx_blocks - i)


def _ssd_kernel(af_ref, ab_ref, csf_ref, csb_ref, sidef_ref, sideb_ref, dskip_ref, yf_ref, yb_ref, h_ref):
    @pl.when(pl.program_id(1) == 0)
    def _():
        h_ref[...] = jnp.zeros_like(h_ref)

    dskip = dskip_ref.at[0]
    tc = SSD_CHUNK
    n_sub = af_ref.shape[1] // tc
    for j in range(n_sub):
        r0 = j * tc
        _ssd_role(af_ref[0, r0:r0 + tc, :], csf_ref[0, r0:r0 + tc, :], sidef_ref[0, j], 0, dskip, yf_ref, r0, h_ref)
        jb = n_sub - 1 - j
        r1 = jb * tc
        _ssd_role(ab_ref[0, r1:r1 + tc, :], csb_ref[0, r1:r1 + tc, :], sideb_ref[0, jb], 1, dskip, yb_ref, r1, h_ref)


def _ssd_scan(act, cs, side, layer, dskip, n_ctx):
    bsz, t, _ = act.shape
    tc = SSD_BLOCK
    n_blocks = t // tc
    ctx_blocks = n_ctx // tc
    assert n_ctx % tc == 0 and t % tc == 0

    def fwd(i):
        return i

    def bwd(i):
        return _bwd_block(i, n_blocks, ctx_blocks)

    def main(order, width):
        return pl.BlockSpec((1, tc, width), lambda b, i: (b, order(i), 0))

    def side_spec(order):
        return pl.BlockSpec((1, tc // SSD_CHUNK, SIDE_ROWS, SSD_CHUNK), lambda b, i: (b, order(i), 0, 0))

    def const(shape):
        return pl.BlockSpec((1,) + shape[1:], lambda b, i: (layer,) + (0,) * (len(shape) - 1))

    y_shape = jax.ShapeDtypeStruct((bsz, t, SSD_WIDTH), F32)
    return pl.pallas_call(
        _ssd_kernel,
        out_shape=(y_shape, y_shape),
        grid=(bsz, n_blocks),
        in_specs=[
            main(fwd, SSD_XBC), main(bwd, SSD_XBC), main(fwd, LANES), main(bwd, LANES),
            side_spec(fwd), side_spec(bwd),
            const(dskip.shape),
        ],
        out_specs=(main(fwd, SSD_WIDTH), main(bwd, SSD_WIDTH)),
        scratch_shapes=[pltpu.VMEM((2, SSD_GROUPS, SSD_STATE, HEADS_PER_GROUP * SSD_HEAD_DIM), F32)],
        compiler_params=pltpu.CompilerParams(
            dimension_semantics=("arbitrary", "arbitrary"), vmem_limit_bytes=VMEM_LIMIT_BYTES),
        name="ssd_scan",
    )(act, act, cs, cs, side, side, dskip)


def _mix_gather(att_ref, ysc_ref, z_ref, yf_ref, yb_ref, normg_ref, ycat_ref):
    gated = (yf_ref[...] + yb_ref[...]) * _silu(z_ref[...].astype(F32))
    gw = SSD_WIDTH // SSD_GROUPS
    w = SC_WIDTH
    ycat_ref[:, 0:MLA_WIDTH] = att_ref[...]
    ycat_ref[:, MLA_WIDTH:MLA_WIDTH + w] = ysc_ref[...]
    c0 = MLA_WIDTH + w
    for g in range(SSD_GROUPS):
        gg = gated[:, g * gw:(g + 1) * gw]
        gg = gg * lax.rsqrt(jnp.mean(gg * gg, axis=-1, keepdims=True) + EPS) * normg_ref[:, g * gw:(g + 1) * gw]
        ycat_ref[:, c0 + g * gw:c0 + (g + 1) * gw] = gg.astype(BF16)


def _mix_residual(x, mod_ref, y_ref, gpost_ref, gpre2_ref, x1_ref, h2_ref):
    gate1 = mod_ref[0, 2:3, :]
    shift2 = mod_ref[0, 3:4, :]
    scale2 = mod_ref[0, 4:5, :]
    x1 = x + gate1 * _rms(y_ref[...], gpost_ref[...])
    x1_ref[...] = x1
    h2_ref[...] = (_rms(x1, gpre2_ref[...]) * (1.0 + scale2) + shift2).astype(BF16)


def _mix_mlp(h2_ref, w1_ref, w2_ref, acc_ref):
    d_ff = w1_ref.shape[1]
    for c in range(d_ff // FF_CHUNK):
        a = _dot(h2_ref[...], w1_ref[:, c * FF_CHUNK:(c + 1) * FF_CHUNK])
        r = jnp.square(jnp.maximum(a, 0.0)).astype(BF16)
        part = _dot(r, w2_ref[c * FF_CHUNK:(c + 1) * FF_CHUNK, :])
        if c == 0:
            acc_ref[...] = part
        else:
            acc_ref[...] += part


N_MIX_PARAMS = 7
N_MIX_TOKEN_INPUTS = 6


def _mix_ffn_kernel(*refs, layout, latent_only, split):
    n_x = 2 if split else 1
    per_half = n_x + N_MIX_TOKEN_INPUTS
    n_in = TILES_PER_STEP * per_half
    normg, wout, gpost, gpre2, w1, w2, gpost2 = [r.at[0] for r in refs[n_in:n_in + N_MIX_PARAMS]]
    o_ref = refs[n_in + N_MIX_PARAMS]
    scratch = refs[n_in + N_MIX_PARAMS + 1:]
    per_tile = len(scratch) // TILES_PER_STEP
    tiles = [scratch[h * per_tile:(h + 1) * per_tile] for h in range(TILES_PER_STEP)]
    tm = TOKEN_TILE
    halves = [refs[h * per_half:(h + 1) * per_half] for h in range(TILES_PER_STEP)]
    mods = [half[n_x].at[0] for half in halves]
    for half, (ycat_ref, y_ref, _, _, _) in zip(halves, tiles):
        _mix_gather(*half[n_x + 1:], normg, ycat_ref)
        y_ref[...] = _dot(ycat_ref[...], wout[...])
    for h, (half, (_, y_ref, acc_ref, x1_ref, h2_ref)) in enumerate(zip(halves, tiles)):
        u = layout.unified_tile(pl.program_id(0) * TILES_PER_STEP + h, latent_only)
        x = layout.select_source(u, half[:n_x])
        _mix_residual(x, mods[h], y_ref, gpost, gpre2, x1_ref, h2_ref)
        _mix_mlp(h2_ref, w1, w2, acc_ref)
    for h, (_, _, acc_ref, x1_ref, _) in enumerate(tiles):
        gate2 = mods[h][0, 5:6, :]
        o_ref[h * tm:(h + 1) * tm, :] = x1_ref[...] + gate2 * _rms(acc_ref[...], gpost2[...])


def _mix_ffn(src, mod, layer, att, ysc, z, yf, yb, normg, wout, gpost, gpre2, w1, w2, gpost2, layout,
             latent_only):
    tm = TOKEN_TILE
    d = src[0].shape[-1]
    n_out_tiles = layout.bsz * (layout.lat_tiles if latent_only else layout.tiles_per_batch)
    assert n_out_tiles % TILES_PER_STEP == 0
    att_unified = att.shape[0] == layout.bsz * layout.tiles_per_batch * tm

    in_specs, operands = [], []
    for h in range(TILES_PER_STEP):
        def u_of(j, h=h):
            return layout.unified_tile(j * TILES_PER_STEP + h, latent_only)

        def tok(width, tile_of=u_of):
            return pl.BlockSpec((tm, width), lambda j: (tile_of(j), 0))

        in_specs += layout.source_specs(u_of, d, len(src) == 2)
        in_specs += [
            pl.BlockSpec((1, 1, N_MOD, d), lambda j, u_of=u_of: (layer, layout.mod_row(u_of(j)), 0, 0)),
            tok(MLA_WIDTH) if att_unified else tok(MLA_WIDTH, lambda j, h=h: j * TILES_PER_STEP + h),
            tok(SC_WIDTH), tok(SSD_WIDTH), tok(SSD_WIDTH), tok(SSD_WIDTH),
        ]
        operands += list(src) + [mod, att, ysc, z, yf, yb]
    consts = [normg, wout, gpost, gpre2, w1, w2, gpost2]
    in_specs += [_layer_spec(c, layer, single_buffer=True) for c in consts]
    return pl.pallas_call(
        functools.partial(_mix_ffn_kernel, layout=layout, latent_only=latent_only, split=len(src) == 2),
        out_shape=jax.ShapeDtypeStruct((n_out_tiles * tm, d), F32),
        grid=(n_out_tiles // TILES_PER_STEP,),
        in_specs=in_specs,
        out_specs=pl.BlockSpec((TILES_PER_STEP * tm, d), lambda j: (j, 0)),
        scratch_shapes=[pltpu.VMEM((tm, d), BF16), pltpu.VMEM((tm, d), F32), pltpu.VMEM((tm, d), F32),
                        pltpu.VMEM((tm, d), F32), pltpu.VMEM((tm, d), BF16)] * TILES_PER_STEP,
        compiler_params=pltpu.CompilerParams(
            dimension_semantics=("parallel",), vmem_limit_bytes=VMEM_LIMIT_BYTES),
        name="mix_ffn",
    )(*operands, *consts)


def _rope_tables(n_ctx, seq):
    f32 = np.float32
    half = QK_ROPE // 2
    inv_freq = (f32(ROPE_THETA) ** (-np.arange(0, half, 2, dtype=f32) / f32(half))).astype(f32)
    rows = seq // GRID_W
    row = np.repeat(np.arange(rows, dtype=f32), GRID_W)
    col = np.tile(np.arange(GRID_W, dtype=f32), rows)
    ang_r = row[:, None] * inv_freq
    ang_c = col[:, None] * inv_freq
    ang = np.concatenate([ang_r, ang_r, ang_c, ang_c], axis=-1).astype(f32)
    t = n_ctx + seq
    cos_t = np.ones((t, LANES), f32)
    sin_t = np.zeros((t, LANES), f32)
    cos_t[n_ctx:, ROPE_LANE0:ROPE_LANE0 + QK_ROPE] = np.cos(ang)
    sin_t[n_ctx:, ROPE_LANE0:ROPE_LANE0 + QK_ROPE] = np.sin(ang)
    return jnp.asarray(cos_t), jnp.asarray(sin_t)


def kernel(x, c, ctx, c_ctx, w_mod, b_mod, g_pre_mix, w_in, mla_q_norm, w_uq, mla_kv_norm, w_ukv, sc_conv_w, ssd_conv_w, ssd_conv_b, ssd_a_log, ssd_dt_bias, ssd_d, ssd_norm, w_out, g_post_mix, g_pre_ffn, w_ff1, w_ff2, g_post_ffn):
    bsz, seq, d = x.shape
    n_ctx = ctx.shape[1]
    depth = w_mod.shape[0]
    assert n_ctx == TOKEN_TILE and seq % TOKEN_TILE == 0 and seq % GRID_W == 0
    layout = _TokenLayout(bsz, n_ctx, seq)
    t = n_ctx + seq

    cvec = jnp.zeros((SUBLANES, d), F32).at[:bsz].set(c).at[layout.ctx_row].set(c_ctx)
    mod_all = _modulation(cvec, w_mod, b_mod).reshape(depth, SUBLANES, N_MOD, d)
    cos_t, sin_t = _rope_tables(n_ctx, seq)

    src = (ctx.reshape(bsz * n_ctx, d), x.reshape(bsz * seq, d))

    def rows3(a):
        return a.reshape(depth, 1, -1)

    def dt_lanes(a):
        flat = a.reshape(depth, 1, -1)
        return jnp.pad(flat, ((0, 0), (0, 0), (DT_LANE0, LANES - DT_LANE0 - flat.shape[-1])))

    alog, dtbias = dt_lanes(ssd_a_log), dt_lanes(ssd_dt_bias)
    dskip = rows3(jnp.repeat(ssd_d, SSD_HEAD_DIM, axis=-1))
    w_out_b, w_ff1_b, w_ff2_b = w_out.astype(BF16), w_ff1.astype(BF16), w_ff2.astype(BF16)
    w_in_t = jnp.swapaxes(w_in, 1, 2)

    def per_batch(a):
        return a.reshape((bsz, t) + a.shape[1:])

    for i in range(depth):
        last = i == depth - 1
        qt, k, vt, kn, qn, ysc, z, act, cs, side = _inproj(
            src, mod_all, i, rows3(g_pre_mix), w_in_t, rows3(mla_q_norm), w_uq, rows3(mla_kv_norm), w_ukv,
            cos_t, sin_t, sc_conv_w, ssd_conv_w, rows3(ssd_conv_b), alog, dtbias, layout)
        att = _attention(qt, per_batch(k), vt, kn.reshape(bsz, layout.tiles_per_batch, SUBLANES, LANES), qn,
                         n_ctx, ctx_queries=not last)
        yf, yb = _ssd_scan(per_batch(act), per_batch(cs), side.reshape(bsz, t // SSD_CHUNK, SIDE_ROWS, SSD_CHUNK),
                           i, dskip, n_ctx)
        out = _mix_ffn(src, mod_all, i, att.reshape(-1, MLA_WIDTH), ysc, z,
                       yf.reshape(-1, SSD_WIDTH), yb.reshape(-1, SSD_WIDTH), rows3(ssd_norm),
                       w_out_b, rows3(g_post_mix), rows3(g_pre_ffn), w_ff1_b, w_ff2_b, rows3(g_post_ffn),
                       layout, latent_only=last)
        src = (out,)
    return out.reshape(bsz, seq, d)
```

```python
import functools
import math

import jax
import jax.numpy as jnp
import numpy as np
from jax import lax
from jax.experimental import pallas as pl
from jax.experimental.pallas import tpu as pltpu

F32 = jnp.float32
BF16 = jnp.bfloat16

GRID_W = 64
EPS = 1e-6
N_MOD = 6
MLA_HEADS = 4
Q_LORA = 256
KV_LORA = 128
QK_NOPE = 64
QK_ROPE = 32
V_HEAD = 64
MLA_WIDTH = MLA_HEADS * V_HEAD
MLA_SCALE = (QK_NOPE + QK_ROPE) ** -0.5
ROPE_THETA = 10000.0
SC_WIDTH = 256
SSD_HEADS = 8
SSD_HEAD_DIM = 64
SSD_WIDTH = SSD_HEADS * SSD_HEAD_DIM
SSD_GROUPS = 2
SSD_STATE = 64
SSD_CHUNK = 128
SSD_GN = SSD_GROUPS * SSD_STATE
SSD_XBC = SSD_WIDTH + 2 * SSD_GN
HEADS_PER_GROUP = SSD_HEADS // SSD_GROUPS
IN_MLA = Q_LORA + KV_LORA + QK_ROPE
IN_SC = 3 * SC_WIDTH

LANES = 128
SUBLANES = 8
VMEM_LIMIT_BYTES = 56 * 1024 * 1024

HEAD_PAD = LANES
V_ROWS = V_HEAD + 16
ROPE_LANE0 = QK_NOPE
TOKEN_TILE = 256
TILES_PER_STEP = 2
INPROJ_TILES_PER_STEP = 4
SSD_BLOCK = 2 * SSD_CHUNK
FF_CHUNK = 1024

C_Q0 = 0
C_KV0 = C_Q0 + Q_LORA
C_KR0 = C_KV0 + KV_LORA
C_SC0 = C_KR0 + LANES
C_Z0 = C_SC0 + IN_SC
C_XBC0 = C_Z0 + SSD_WIDTH
D_IN_PAD = C_XBC0 + SSD_XBC
DT_LANE0 = QK_ROPE

LOG2E = math.log2(math.e)
Q_PRESCALE = MLA_SCALE * LOG2E

SHIFT_HEADROOM = 60.0
SHIFT_MAX_BOUND = 90.0
BOUND_SLACK = 1.0 + 2.0 ** -8


def _rms(x, g):
    return x * lax.rsqrt(jnp.mean(x * x, axis=-1, keepdims=True) + EPS) * g


def _silu(x):
    return x / (1.0 + jnp.exp(-x))


def _dot(a, b):
    return jnp.dot(a, b, preferred_element_type=F32)


class _TokenLayout:
    def __init__(self, bsz, n_ctx, seq):
        tm = TOKEN_TILE
        assert n_ctx % tm == 0 and seq % tm == 0
        self.bsz = bsz
        self.ctx_tiles = n_ctx // tm
        self.lat_tiles = seq // tm
        self.tiles_per_batch = self.ctx_tiles + self.lat_tiles
        self.n_tiles = bsz * self.tiles_per_batch
        self.ctx_row = bsz

    def unified_tile(self, o, latent_only):
        if not latent_only:
            return o
        return o + (o // self.lat_tiles + 1) * self.ctx_tiles

    def coords(self, u):
        b = u // self.tiles_per_batch
        return b, u - b * self.tiles_per_batch

    def mod_row(self, u):
        b, i = self.coords(u)
        return jnp.where(i < self.ctx_tiles, self.ctx_row, b)

    def seq_edges(self, u):
        _, i = self.coords(u)
        first = (i == 0) | (i == self.ctx_tiles)
        last = (i == self.ctx_tiles - 1) | (i == self.tiles_per_batch - 1)
        return first, last

    def _latent_tile(self, u):
        b, i = self.coords(u)
        return b * self.lat_tiles + jnp.maximum(i - self.ctx_tiles, 0)

    def _context_tile(self, u):
        b, i = self.coords(u)
        return b * self.ctx_tiles + jnp.minimum(i, self.ctx_tiles - 1)

    def source_specs(self, u_of, d, split):
        tm = TOKEN_TILE
        if not split:
            return [pl.BlockSpec((tm, d), lambda j: (u_of(j), 0))]
        return [pl.BlockSpec((tm, d), lambda j: (self._context_tile(u_of(j)), 0)),
                pl.BlockSpec((tm, d), lambda j: (self._latent_tile(u_of(j)), 0))]

    def halo_specs(self, u_of, d, split):
        rb = TOKEN_TILE // SUBLANES
        if split:
            assert self.ctx_tiles == 1
            tile_of, n_rb = (lambda j: self._latent_tile(u_of(j))), self.bsz * self.lat_tiles * rb
        else:
            tile_of, n_rb = u_of, self.n_tiles * rb
        return [pl.BlockSpec((SUBLANES, d), lambda j: (jnp.maximum(tile_of(j) * rb - 1, 0), 0)),
                pl.BlockSpec((SUBLANES, d), lambda j: (jnp.minimum((tile_of(j) + 1) * rb, n_rb - 1), 0))]

    def select_source(self, u, refs):
        if len(refs) == 1:
            return refs[0][...]
        _, i = self.coords(u)
        return jnp.where(i < self.ctx_tiles, refs[0][...], refs[1][...])


def _mod_kernel(c_ref, w_ref, b_ref, o_ref):
    s = _silu(c_ref[...]).astype(BF16)
    o_ref[0] = _dot(s, w_ref[0].astype(BF16)) + b_ref[0]


def _modulation(cvec, w_mod, b_mod):
    depth, d, nd = w_mod.shape
    rows = cvec.shape[0]
    return pl.pallas_call(
        _mod_kernel,
        out_shape=jax.ShapeDtypeStruct((depth, rows, nd), F32),
        grid=(depth, nd // d),
        in_specs=[
            pl.BlockSpec((rows, d), lambda l, j: (0, 0)),
            pl.BlockSpec((1, d, d), lambda l, j: (l, 0, j)),
            pl.BlockSpec((1, 1, d), lambda l, j: (l, 0, j)),
        ],
        out_specs=pl.BlockSpec((1, rows, d), lambda l, j: (l, 0, j)),
        compiler_params=pltpu.CompilerParams(dimension_semantics=("parallel", "parallel")),
        name="modulation",
    )(cvec, w_mod, b_mod.reshape(depth, 1, nd))


def _conv3_rows(a_ext, w_ref, tm):
    te = a_ext.shape[0]
    lo, hi = SUBLANES, SUBLANES + tm
    prev = pltpu.roll(a_ext, 1, 0)[lo:hi]
    nxt = pltpu.roll(a_ext, te - 1, 0)[lo:hi]
    return w_ref[0:1, :] * prev + w_ref[1:2, :] * a_ext[lo:hi] + w_ref[2:3, :] * nxt


N_INPROJ_PARAMS = 11
N_INPROJ_TOKEN_INPUTS = 5


def _split3(x):
    hi = x.astype(BF16)
    r = x - hi.astype(F32)
    mid = r.astype(BF16)
    lo = (r - mid.astype(F32)).astype(BF16)
    return hi, mid, lo


SIDE_ROWS = 2 * 2 * SSD_HEADS


def _ssd_decay_tables(dt_raw, alog_ref, dtbias_ref, cs_ref, side_ref, slot):
    tc = SSD_CHUNK
    tm = dt_raw.shape[0]
    nd = 2 * SSD_HEADS
    a = -jnp.exp(alog_ref[...])
    xb = dt_raw + dtbias_ref[...]
    dt = jnp.maximum(xb, 0.0) + jnp.log(1.0 + jnp.exp(-jnp.abs(xb)))
    la = dt * a * LOG2E
    r_i = lax.broadcasted_iota(jnp.int32, (tc, tc), 0)
    c_i = lax.broadcasted_iota(jnp.int32, (tc, tc), 1)
    tri_f = jnp.where(c_i <= r_i, 1.0, 0.0).astype(BF16)
    tri_b = jnp.where(c_i >= r_i, 1.0, 0.0).astype(BF16)
    lane = lax.broadcasted_iota(jnp.int32, (1, LANES), 1)
    fwd_lane = lane < DT_LANE0 + SSD_HEADS
    row = lax.broadcasted_iota(jnp.int32, (nd, 1), 0)
    for c in range(tm // tc):
        parts = _split3(la[c * tc:(c + 1) * tc])
        cs_f = sum(_dot(tri_f, p) for p in parts)
        cs_b = sum(_dot(tri_b, p) for p in parts)
        cs = jnp.where(fwd_lane, cs_f, cs_b)
        cs_ref[slot * tm + c * tc:slot * tm + (c + 1) * tc, :] = cs
        cs_t = cs.T[DT_LANE0:DT_LANE0 + nd]
        dt_t = dt[c * tc:(c + 1) * tc].T[DT_LANE0:DT_LANE0 + nd]
        cs_end = jnp.where(row < SSD_HEADS, cs_t[:, tc - 1:tc], cs_t[:, 0:1])
        w_t = jnp.exp2(cs_end - cs_t) * dt_t
        side_ref[slot * (tm // tc) + c] = jnp.concatenate([cs_t - jnp.log(dt_t) * LOG2E, w_t], axis=0)


def _rotate_cols(w):
    a, b, c, d = jnp.split(w, 4, axis=-1)
    return jnp.concatenate([-b, a, -d, c], axis=-1)


def _relayout_weights(win_raw, wuq_raw, wukv_raw, win_ref, wuq_ref, wukv_ref):
    d = win_raw.shape[1]
    w_kr = win_raw[Q_LORA + KV_LORA:IN_MLA, :]
    w_dt = win_raw[IN_MLA + IN_SC + SSD_WIDTH + SSD_XBC:, :]
    a, b, c, e = (w_kr[i * SUBLANES:(i + 1) * SUBLANES] for i in range(4))
    kr_block = jnp.concatenate([
        -b, a, -e, c,
        w_dt, jnp.zeros((LANES // 2 - QK_ROPE - w_dt.shape[0], d), F32),
        w_kr, jnp.zeros((LANES // 2 - QK_ROPE, d), F32)], axis=0)
    win_ref[:, C_KR0:C_SC0] = kr_block.T.astype(BF16)
    step = 2 * LANES
    for src0, dst0, n_rows in ((0, C_Q0, Q_LORA + KV_LORA), (IN_MLA, C_SC0, IN_SC + SSD_WIDTH + SSD_XBC)):
        for off in range(0, n_rows, step):
            rows = min(step, n_rows - off)
            blk = win_raw[src0 + off:src0 + off + rows, :]
            win_ref[:, dst0 + off:dst0 + off + rows] = blk.T.astype(BF16)

    dqk = QK_NOPE + QK_ROPE
    wq = wuq_raw[...]
    nq = wq.shape[0]
    plain, rot = [], []
    for h in range(MLA_HEADS):
        wh = wq[:, h * dqk:(h + 1) * dqk]
        pad = jnp.zeros((nq, HEAD_PAD - dqk), F32)
        plain += [wh, pad]
        rot += [jnp.zeros((nq, QK_NOPE), F32), _rotate_cols(wh[:, QK_NOPE:]), pad]
    wuq_ref[...] = jnp.concatenate(plain + rot, axis=1).astype(BF16)

    dkv = QK_NOPE + V_HEAD
    wkv = wukv_raw[...]
    nkv = wkv.shape[0]
    ks, vs = [], []
    for h in range(MLA_HEADS):
        wh = wkv[:, h * dkv:(h + 1) * dkv]
        ks += [wh[:, :QK_NOPE], jnp.zeros((nkv, HEAD_PAD - QK_NOPE), F32)]
        vs.append(wh[:, QK_NOPE:])
    wukv_ref[...] = jnp.concatenate(ks + vs, axis=1).astype(BF16)


def _inproj_kernel(*refs, layout, split, tiles):
    n_x = 2 if split else 1
    per_half = n_x + N_INPROJ_TOKEN_INPUTS
    n_in = tiles * per_half
    (gpre, win_raw, qg, wuq_raw, kvg, wukv_raw, scw, cw, cb, alog, dtbias) = [
        r.at[0] for r in refs[n_in:n_in + N_INPROJ_PARAMS]]
    n_out = len(refs) - n_in - N_INPROJ_PARAMS - 3
    outs = refs[n_in + N_INPROJ_PARAMS:n_in + N_INPROJ_PARAMS + n_out]
    win_ref, wuq_ref, wukv_ref = refs[n_in + N_INPROJ_PARAMS + n_out:]

    @pl.when(pl.program_id(0) == 0)
    def _():
        _relayout_weights(win_raw, wuq_raw, wukv_raw, win_ref, wuq_ref, wukv_ref)

    params = (qg, wuq_ref, kvg, wukv_ref, scw, cw, cb, alog, dtbias)
    for h in range(tiles):
        half = refs[h * per_half:(h + 1) * per_half]
        u = pl.program_id(0) * tiles + h
        x = layout.select_source(u, half[:n_x])
        xp_ref, xn_ref, mod_ref, cos_ref, sin_ref = half[n_x:]
        h_ext = _inproj_normalise(x, xp_ref, xn_ref, mod_ref.at[0], gpre, layout.seq_edges(u))
        u_ext = _dot(h_ext, win_ref[...])
        _inproj_finish(u_ext, cos_ref, sin_ref, params, outs, h)


def _inproj_normalise(x, xp_ref, xn_ref, mod_ref, gpre_ref, seq_edges):
    first_of_seq, last_of_seq = seq_edges
    shift = mod_ref[0, 0:1, :]
    scale = mod_ref[0, 1:2, :]

    def norm_mod(xx):
        return _rms(xx, gpre_ref[...]) * (1.0 + scale) + shift

    h_prev = jnp.where(first_of_seq, 0.0, norm_mod(xp_ref[...]))
    h_next = jnp.where(last_of_seq, 0.0, norm_mod(xn_ref[...]))
    return jnp.concatenate([h_prev, norm_mod(x), h_next], axis=0).astype(BF16)


def _inproj_finish(u_ext, cos_ref, sin_ref, params, outs, slot):
    qg_ref, wuq_ref, kvg_ref, wukv_ref, scw_ref, cw_ref, cb_ref, alog_ref, dtbias_ref = params
    qt_ref, k_ref, vt_ref, kn_ref, qn_ref, ysc_ref, z_ref, act_ref, cs_ref, side_ref = outs
    tm = u_ext.shape[0] - 2 * SUBLANES
    rows = slice(slot * tm, (slot + 1) * tm)
    own = slice(SUBLANES, SUBLANES + tm)

    z_ref[rows, :] = u_ext[own, C_Z0:C_XBC0].astype(z_ref.dtype)
    lane = lax.broadcasted_iota(jnp.int32, (1, LANES), 1)
    krb = u_ext[own, C_KR0:C_SC0]
    dt_raw = jnp.where((lane >= DT_LANE0) & (lane < DT_LANE0 + 2 * SSD_HEADS), krb, 0.0)
    _ssd_decay_tables(dt_raw, alog_ref, dtbias_ref, cs_ref, side_ref, slot)

    conv = _conv3_rows(u_ext[:, C_XBC0:D_IN_PAD], cw_ref, tm) + cb_ref[...]
    act_ref[rows, :] = _silu(conv).astype(act_ref.dtype)

    w = SC_WIDTH
    prod = u_ext[:, C_SC0 + w:C_SC0 + 2 * w] * u_ext[:, C_SC0 + 2 * w:C_Z0]
    ysc_ref[rows, :] = (u_ext[own, C_SC0:C_SC0 + w] * _conv3_rows(prod, scw_ref, tm)).astype(ysc_ref.dtype)

    cos = cos_ref[...]
    sin = sin_ref[...]
    nh = MLA_HEADS
    cos_h = jnp.concatenate([cos] * nh, axis=1)
    sin_h = jnp.concatenate([sin] * nh, axis=1)

    def head_norm_maxima(a_f32):
        out = []
        for h in range(nh):
            a_h = a_f32[:, h * HEAD_PAD:(h + 1) * HEAD_PAD]
            n2 = jnp.max(jnp.sum(a_h * a_h, axis=1, keepdims=True), axis=0, keepdims=True)
            out.append(jnp.broadcast_to(n2, (1, LANES)))
        return jnp.concatenate(out + [jnp.zeros((SUBLANES - nh, LANES), F32)], axis=0)

    cq = _rms(u_ext[own, C_Q0:C_KV0], qg_ref[...]).astype(BF16)
    q2 = _dot(cq, wuq_ref[...])
    qw = nh * HEAD_PAD
    q_b = ((q2[:, :qw] * cos_h + q2[:, qw:] * sin_h) * Q_PRESCALE).astype(BF16)
    q_f = q_b.astype(F32)
    qt_ref[:, rows] = q_f.T.astype(BF16)
    qn_ref[slot] = head_norm_maxima(q_f)

    ckv = _rms(u_ext[own, C_KV0:C_KR0], kvg_ref[...]).astype(BF16)
    kv = _dot(ckv, wukv_ref[...])
    rope_lane = (lane >= ROPE_LANE0) & (lane < ROPE_LANE0 + QK_ROPE)
    cos_k = jnp.where(rope_lane, cos, 0.0)
    kr = krb * cos_k + pltpu.roll(krb, LANES // 2, 1) * sin
    k = kv[:, :qw] + jnp.concatenate([kr] * nh, axis=1)
    k_b = k.astype(BF16)
    k_ref[rows, :] = k_b
    kn_ref[slot] = head_norm_maxima(k_b.astype(F32))
    v_t = kv[:, qw:].T.astype(BF16)
    for h in range(nh):
        vt_ref[h, 0:V_HEAD, rows] = v_t[h * V_HEAD:(h + 1) * V_HEAD]
        vt_ref[h, V_HEAD:V_ROWS, rows] = jnp.ones((V_ROWS - V_HEAD, tm), BF16)


def _layer_spec(a, layer, single_buffer=False):
    block = (1,) + a.shape[1:]
    index_map = lambda j: (layer,) + (0,) * (a.ndim - 1)
    if single_buffer:
        return pl.BlockSpec(block, index_map, pipeline_mode=pl.Buffered(1))
    return pl.BlockSpec(block, index_map)


def _inproj(src, mod, layer, gpre, win, qg, wuq, kvg, wukv, cos_t, sin_t, scw, cw, cb, alog, dtbias, layout):
    tm = TOKEN_TILE
    d = src[0].shape[-1]
    split = len(src) == 2
    qw = MLA_HEADS * HEAD_PAD
    nt = layout.n_tiles
    tiles = INPROJ_TILES_PER_STEP
    assert nt % tiles == 0
    rows = nt * tm

    in_specs, operands = [], []
    for h in range(tiles):
        def u_of(j, h=h):
            return j * tiles + h

        def rope(j, u_of=u_of):
            return (layout.coords(u_of(j))[1], 0)

        in_specs += layout.source_specs(u_of, d, split) + layout.halo_specs(u_of, d, split)
        in_specs += [
            pl.BlockSpec((1, 1, N_MOD, d), lambda j, u_of=u_of: (layer, layout.mod_row(u_of(j)), 0, 0)),
            pl.BlockSpec((tm, LANES), rope), pl.BlockSpec((tm, LANES), rope),
        ]
        operands += list(src) + [src[-1], src[-1], mod, cos_t, sin_t]
    consts = [gpre, win, qg, wuq, kvg, wukv, scw, cw, cb, alog, dtbias]
    in_specs += [_layer_spec(c, layer, single_buffer=c is win) for c in consts]
    step_rows = tiles * tm
    chunks_per_step = step_rows // SSD_CHUNK

    def tok(width):
        return pl.BlockSpec((step_rows, width), lambda j: (j, 0))

    norm_shape = jax.ShapeDtypeStruct((nt, SUBLANES, LANES), F32)
    norm_spec = pl.BlockSpec((tiles, SUBLANES, LANES), lambda j: (j, 0, 0))
    out_shape = (
        jax.ShapeDtypeStruct((qw, rows), BF16),
        jax.ShapeDtypeStruct((rows, qw), BF16),
        jax.ShapeDtypeStruct((MLA_HEADS, V_ROWS, rows), BF16),
        norm_shape,
        norm_shape,
        jax.ShapeDtypeStruct((rows, SC_WIDTH), BF16),
        jax.ShapeDtypeStruct((rows, SSD_WIDTH), BF16),
        jax.ShapeDtypeStruct((rows, SSD_XBC), BF16),
        jax.ShapeDtypeStruct((rows, LANES), F32),
        jax.ShapeDtypeStruct((rows // SSD_CHUNK, SIDE_ROWS, SSD_CHUNK), F32),
    )
    return pl.pallas_call(
        functools.partial(_inproj_kernel, layout=layout, split=split, tiles=tiles),
        out_shape=out_shape,
        grid=(nt // tiles,),
        in_specs=in_specs,
        out_specs=(pl.BlockSpec((qw, step_rows), lambda j: (0, j)), tok(qw),
                   pl.BlockSpec((MLA_HEADS, V_ROWS, step_rows), lambda j: (0, 0, j)),
                   norm_spec, norm_spec,
                   tok(SC_WIDTH), tok(SSD_WIDTH), tok(SSD_XBC), tok(LANES),
                   pl.BlockSpec((chunks_per_step, SIDE_ROWS, SSD_CHUNK), lambda j: (j, 0, 0))),
        scratch_shapes=[pltpu.VMEM((d, D_IN_PAD), BF16),
                        pltpu.VMEM((Q_LORA, 2 * qw), BF16),
                        pltpu.VMEM((KV_LORA, qw + MLA_WIDTH), BF16)],
        compiler_params=pltpu.CompilerParams(
            dimension_semantics=("arbitrary",), vmem_limit_bytes=VMEM_LIMIT_BYTES),
        name="inproj",
    )(*operands, *consts)


def _attn_kernel(*refs, n_keys, n_ctx, ctx_queries, q_tiles):
    qt_refs = refs[:q_tiles]
    k_ref, vt_ref, kn_ref = refs[q_tiles:q_tiles + 3]
    qn_refs = refs[q_tiles + 3:2 * q_tiles + 3]
    o_ref, s_ref, p_ref = refs[2 * q_tiles + 3:]
    i = pl.program_id(1)
    kn = jnp.max(kn_ref[0], axis=0)
    for t in range(q_tiles):
        _attend_tile(qt_refs[t], qn_refs[t], k_ref, vt_ref, kn, o_ref, t, s_ref, p_ref, i,
                     n_keys=n_keys, n_ctx=n_ctx, ctx_queries=ctx_queries)


def _attend_tile(q_t, qn_ref, k_ref, vt_ref, kn, o_ref, slot, s_ref, p_ref, i, *, n_keys, n_ctx, ctx_queries):
    tq = q_t.shape[1]

    def finish(outs):
        o_ref[0, slot * tq:(slot + 1) * tq, :] = jnp.concatenate(outs, axis=0).T.astype(o_ref.dtype)

    def head_out(ov):
        return ov[0:V_HEAD] / ov[V_HEAD:V_HEAD + 1]

    def attend_two_pass(nk):
        def scores(h):
            s_ref[h, 0:nk, :] = _dot(k_ref[0, 0:nk, h * HEAD_PAD:(h + 1) * HEAD_PAD],
                                     q_t[h * HEAD_PAD:(h + 1) * HEAD_PAD, :])

        def probs(h):
            s = s_ref[h, 0:nk, :]
            p_ref[h, 0:nk, :] = jnp.exp2(s - jnp.max(s, axis=0, keepdims=True)).astype(BF16)

        def values(h):
            return head_out(_dot(vt_ref[h, :, 0:nk], p_ref[h, 0:nk, :]))

        scores(0), scores(1)
        scores(2), scores(3), probs(0), probs(1)
        outs = [values(0), values(1)]
        probs(2), probs(3)
        outs += [values(2), values(3)]
        finish(outs)

    def attend_one_pass(nk, shift):
        outs = []
        for pair in range(MLA_HEADS // 2):
            heads = (2 * pair, 2 * pair + 1)
            s = [_dot(k_ref[0, 0:nk, h * HEAD_PAD:(h + 1) * HEAD_PAD],
                      q_t[h * HEAD_PAD:(h + 1) * HEAD_PAD, :]) for h in heads]
            p = [jnp.exp2(sh - shift[h]).astype(BF16) for h, sh in zip(heads, s)]
            outs += [head_out(_dot(vt_ref[h, :, 0:nk], ph)) for h, ph in zip(heads, p)]
        finish(outs)

    def attend(nk):
        bound = jnp.sqrt(qn_ref[0] * kn) * BOUND_SLACK
        one_pass = jnp.max(bound) <= SHIFT_MAX_BOUND

        @pl.when(one_pass)
        def _():
            attend_one_pass(nk, [bound[h:h + 1, 0:1] - SHIFT_HEADROOM for h in range(MLA_HEADS)])

        @pl.when(jnp.logical_not(one_pass))
        def _():
            attend_two_pass(nk)

    if ctx_queries:
        @pl.when(i == 0)
        def _():
            attend_two_pass(n_ctx)

        @pl.when(i > 0)
        def _():
            attend(n_keys)
    else:
        attend(n_keys)


def _attention(qt, k, vt, kn, qn, n_ctx, ctx_queries):
    bsz, t, qw = k.shape
    tq = TOKEN_TILE
    tiles = t // tq
    t0 = 0 if ctx_queries else n_ctx // tq
    nq = tiles - t0
    q_tiles = 1 if ctx_queries else 2
    assert nq % q_tiles == 0

    def tile_of(b, i, s):
        return b * tiles + i * q_tiles + s + t0

    return pl.pallas_call(
        functools.partial(_attn_kernel, n_keys=t, n_ctx=n_ctx, ctx_queries=ctx_queries, q_tiles=q_tiles),
        out_shape=jax.ShapeDtypeStruct((bsz, nq * tq, MLA_WIDTH), BF16),
        grid=(bsz, nq // q_tiles),
        in_specs=[pl.BlockSpec((qw, tq), lambda b, i, s=s: (0, tile_of(b, i, s))) for s in range(q_tiles)] + [
            pl.BlockSpec((1, t, qw), lambda b, i: (b, 0, 0)),
            pl.BlockSpec((MLA_HEADS, V_ROWS, t), lambda b, i: (0, 0, b)),
            pl.BlockSpec((1,) + kn.shape[1:], lambda b, i: (b, 0, 0, 0)),
        ] + [pl.BlockSpec((1, SUBLANES, LANES), lambda b, i, s=s: (tile_of(b, i, s), 0, 0)) for s in range(q_tiles)],
        out_specs=pl.BlockSpec((1, q_tiles * tq, MLA_WIDTH), lambda b, i: (b, i, 0)),
        scratch_shapes=[pltpu.VMEM((MLA_HEADS, t, tq), F32), pltpu.VMEM((MLA_HEADS, t, tq), BF16)],
        compiler_params=pltpu.CompilerParams(
            dimension_semantics=("parallel", "parallel"), vmem_limit_bytes=VMEM_LIMIT_BYTES),
        name="attention",
    )(*([qt] * q_tiles), k, vt, kn, *([qn] * q_tiles))


PAIRS_PER_CHUNK = SSD_HEADS // 2


def _ssd_build_lhs(act, cs, side, direction, lhs_ref, base):
    tc = SSD_CHUNK
    n = SSD_STATE
    bm = act[:, SSD_WIDTH:SSD_WIDTH + SSD_GN].astype(F32)
    cm = act[:, SSD_WIDTH + SSD_GN:]
    bm_t = bm.T

    r_i = lax.broadcasted_iota(jnp.int32, (tc, tc), 0)
    c_i = lax.broadcasted_iota(jnp.int32, (tc, tc), 1)
    tri = (c_i <= r_i) if direction == 0 else (c_i >= r_i)
    nd = 2 * SSD_HEADS
    csd_t, w_t = side[0:nd], side[nd:2 * nd]
    cm_f = cm.astype(F32)
    zeros_lhs = jnp.zeros((n, tc), BF16)
    m = tc + n
    for g in range(SSD_GROUPS):
        cb = _dot(cm[:, g * n:(g + 1) * n], bm_t[g * n:(g + 1) * n, :].astype(BF16))
        bt_g = bm_t[g * n:(g + 1) * n, :]
        for pr in range(HEADS_PER_GROUP // 2):
            slab = g * (HEADS_PER_GROUP // 2) + pr
            for hh in range(2):
                r = direction * SSD_HEADS + 2 * slab + hh
                col = DT_LANE0 + r
                cs_col = jnp.broadcast_to(cs[:, col:col + 1], (tc, tc))
                lmd = jnp.exp2(jnp.where(tri, cs_col - csd_t[r:r + 1, :], -jnp.inf))
                lhs_ref[base + slab, hh * m:hh * m + tc, 0:tc] = (cb * lmd).astype(BF16)
                lhs_ref[base + slab, hh * m:hh * m + tc, tc:2 * tc] = (cm_f * jnp.exp2(cs_col)).astype(BF16)
                lhs_ref[base + slab, hh * m + tc:(hh + 1) * m, 0:tc] = (bt_g * w_t[r:r + 1, :]).astype(BF16)
                lhs_ref[base + slab, hh * m + tc:(hh + 1) * m, tc:2 * tc] = zeros_lhs


def _ssd_apply(act, cs, direction, dskip_ref, y_ref, row0, h_ref, lhs_ref, base):
    tc = SSD_CHUNK
    n = SSD_STATE
    m = tc + n
    end = tc - 1 if direction == 0 else 0
    dec_tot = jnp.exp2(cs[end:end + 1, :])
    lane = lax.broadcasted_iota(jnp.int32, (1, LANES), 1)
    low_half = lane < SSD_HEAD_DIM
    zeros_h = jnp.zeros((n, LANES), BF16)
    for g in range(SSD_GROUPS):
        for pr in range(HEADS_PER_GROUP // 2):
            slab = g * (HEADS_PER_GROUP // 2) + pr
            h_pair = h_ref[direction, g, :, pr * LANES:(pr + 1) * LANES]
            h_b = h_pair.astype(BF16)
            xs_pair = act[:, slab * LANES:(slab + 1) * LANES]
            rhs = jnp.concatenate([xs_pair] + [h_b if gg == g else zeros_h for gg in range(SSD_GROUPS)], axis=0)
            out = _dot(lhs_ref[base + slab], rhs)
            y_pair = jnp.where(low_half, out[0:tc], out[m:m + tc])
            if direction == 0:
                y_pair = y_pair + dskip_ref[:, slab * LANES:(slab + 1) * LANES] * xs_pair.astype(F32)
            y_ref[0, row0:row0 + tc, slab * LANES:(slab + 1) * LANES] = y_pair
            h_new = jnp.where(low_half, out[tc:m], out[m + tc:2 * m])
            cols = [DT_LANE0 + direction * SSD_HEADS + 2 * slab + hh for hh in range(2)]
            keep = jnp.where(low_half, dec_tot[:, cols[0]:cols[0] + 1], dec_tot[:, cols[1]:cols[1] + 1])
            h_ref[direction, g, :, pr * LANES:(pr + 1) * LANES] = h_pair * keep + h_new


def _bwd_block(i, n_blocks, ctx_blocks):
    return jnp.where(i < ctx_blocks, ctx_blocks - 1 - i, n_blocks - 1 + ctx_blocks - i)


def _ssd_kernel(af_ref, ab_ref, csf_ref, csb_ref, sidef_ref, sideb_ref, dskip_ref, yf_ref, yb_ref,
                h_ref, lhs_ref):
    @pl.when(pl.program_id(1) == 0)
    def _():
        h_ref[...] = jnp.zeros_like(h_ref)

    dskip = dskip_ref.at[0]
    tc = SSD_CHUNK
    n_sub = af_ref.shape[1] // tc
    roles = []
    for j in range(n_sub):
        jb = n_sub - 1 - j
        roles.append((af_ref, csf_ref, sidef_ref, j, 0, yf_ref))
        roles.append((ab_ref, csb_ref, sideb_ref, jb, 1, yb_ref))
    for idx, (a_ref, c_ref, s_ref, j, direction, _) in enumerate(roles):
        r0 = j * tc
        _ssd_build_lhs(a_ref[0, r0:r0 + tc, :], c_ref[0, r0:r0 + tc, :], s_ref[0, j], direction,
                       lhs_ref, idx * PAIRS_PER_CHUNK)
    for idx, (a_ref, c_ref, _, j, direction, y_ref) in enumerate(roles):
        r0 = j * tc
        _ssd_apply(a_ref[0, r0:r0 + tc, :], c_ref[0, r0:r0 + tc, :], direction, dskip, y_ref, r0, h_ref,
                   lhs_ref, idx * PAIRS_PER_CHUNK)


def _ssd_scan(act, cs, side, layer, dskip, n_ctx):
    bsz, t, _ = act.shape
    tc = SSD_BLOCK
    n_blocks = t // tc
    ctx_blocks = n_ctx // tc
    assert n_ctx % tc == 0 and t % tc == 0

    def fwd(i):
        return i

    def bwd(i):
        return _bwd_block(i, n_blocks, ctx_blocks)

    def main(order, width):
        return pl.BlockSpec((1, tc, width), lambda b, i: (b, order(i), 0))

    def side_spec(order):
        return pl.BlockSpec((1, tc // SSD_CHUNK, SIDE_ROWS, SSD_CHUNK), lambda b, i: (b, order(i), 0, 0))

    def const(shape):
        return pl.BlockSpec((1,) + shape[1:], lambda b, i: (layer,) + (0,) * (len(shape) - 1))

    y_shape = jax.ShapeDtypeStruct((bsz, t, SSD_WIDTH), F32)
    return pl.pallas_call(
        _ssd_kernel,
        out_shape=(y_shape, y_shape),
        grid=(bsz, n_blocks),
        in_specs=[
            main(fwd, SSD_XBC), main(bwd, SSD_XBC), main(fwd, LANES), main(bwd, LANES),
            side_spec(fwd), side_spec(bwd),
            const(dskip.shape),
        ],
        out_specs=(main(fwd, SSD_WIDTH), main(bwd, SSD_WIDTH)),
        scratch_shapes=[pltpu.VMEM((2, SSD_GROUPS, SSD_STATE, HEADS_PER_GROUP * SSD_HEAD_DIM), F32),
                        pltpu.VMEM((2 * (tc // SSD_CHUNK) * PAIRS_PER_CHUNK, 2 * (SSD_CHUNK + SSD_STATE),
                                    2 * SSD_CHUNK), BF16)],
        compiler_params=pltpu.CompilerParams(
            dimension_semantics=("arbitrary", "arbitrary"), vmem_limit_bytes=VMEM_LIMIT_BYTES),
        name="ssd_scan",
    )(act, act, cs, cs, side, side, dskip)


def _mix_gather(att_ref, ysc_ref, z_ref, yf_ref, yb_ref, normg_ref, ycat_ref):
    gated = (yf_ref[...] + yb_ref[...]) * _silu(z_ref[...].astype(F32))
    gw = SSD_WIDTH // SSD_GROUPS
    w = SC_WIDTH
    ycat_ref[:, 0:MLA_WIDTH] = att_ref[...]
    ycat_ref[:, MLA_WIDTH:MLA_WIDTH + w] = ysc_ref[...]
    c0 = MLA_WIDTH + w
    for g in range(SSD_GROUPS):
        gg = gated[:, g * gw:(g + 1) * gw]
        gg = gg * lax.rsqrt(jnp.mean(gg * gg, axis=-1, keepdims=True) + EPS) * normg_ref[:, g * gw:(g + 1) * gw]
        ycat_ref[:, c0 + g * gw:c0 + (g + 1) * gw] = gg.astype(BF16)


def _mix_residual(x, mod_ref, y_ref, gpost_ref, gpre2_ref, x1_ref, h2_ref):
    gate1 = mod_ref[0, 2:3, :]
    shift2 = mod_ref[0, 3:4, :]
    scale2 = mod_ref[0, 4:5, :]
    x1 = x + gate1 * _rms(y_ref[...], gpost_ref[...])
    x1_ref[...] = x1
    h2_ref[...] = (_rms(x1, gpre2_ref[...]) * (1.0 + scale2) + shift2).astype(BF16)


def _mix_mlp(h2_ref, w1_ref, w2_ref, acc_ref):
    d_ff = w1_ref.shape[1]
    for c in range(d_ff // FF_CHUNK):
        a = _dot(h2_ref[...], w1_ref[:, c * FF_CHUNK:(c + 1) * FF_CHUNK])
        r = jnp.square(jnp.maximum(a, 0.0)).astype(BF16)
        part = _dot(r, w2_ref[c * FF_CHUNK:(c + 1) * FF_CHUNK, :])
        if c == 0:
            acc_ref[...] = part
        else:
            acc_ref[...] += part


N_MIX_PARAMS = 7
N_MIX_TOKEN_INPUTS = 6


def _mix_ffn_kernel(*refs, layout, latent_only, split):
    n_x = 2 if split else 1
    per_half = n_x + N_MIX_TOKEN_INPUTS
    n_in = TILES_PER_STEP * per_half
    normg, wout, gpost, gpre2, w1, w2, gpost2 = [r.at[0] for r in refs[n_in:n_in + N_MIX_PARAMS]]
    o_ref = refs[n_in + N_MIX_PARAMS]
    scratch = refs[n_in + N_MIX_PARAMS + 1:]
    per_tile = len(scratch) // TILES_PER_STEP
    tiles = [scratch[h * per_tile:(h + 1) * per_tile] for h in range(TILES_PER_STEP)]
    tm = TOKEN_TILE
    halves = [refs[h * per_half:(h + 1) * per_half] for h in range(TILES_PER_STEP)]
    mods = [half[n_x].at[0] for half in halves]
    for half, (ycat_ref, y_ref, _, _, _) in zip(halves, tiles):
        _mix_gather(*half[n_x + 1:], normg, ycat_ref)
        y_ref[...] = _dot(ycat_ref[...], wout[...])
    for h, (half, (_, y_ref, acc_ref, x1_ref, h2_ref)) in enumerate(zip(halves, tiles)):
        u = layout.unified_tile(pl.program_id(0) * TILES_PER_STEP + h, latent_only)
        x = layout.select_source(u, half[:n_x])
        _mix_residual(x, mods[h], y_ref, gpost, gpre2, x1_ref, h2_ref)
        _mix_mlp(h2_ref, w1, w2, acc_ref)
    for h, (_, _, acc_ref, x1_ref, _) in enumerate(tiles):
        gate2 = mods[h][0, 5:6, :]
        o_ref[h * tm:(h + 1) * tm, :] = x1_ref[...] + gate2 * _rms(acc_ref[...], gpost2[...])


def _mix_ffn(src, mod, layer, att, ysc, z, yf, yb, normg, wout, gpost, gpre2, w1, w2, gpost2, layout,
             latent_only):
    tm = TOKEN_TILE
    d = src[0].shape[-1]
    n_out_tiles = layout.bsz * (layout.lat_tiles if latent_only else layout.tiles_per_batch)
    assert n_out_tiles % TILES_PER_STEP == 0
    att_unified = att.shape[0] == layout.bsz * layout.tiles_per_batch * tm

    in_specs, operands = [], []
    for h in range(TILES_PER_STEP):
        def u_of(j, h=h):
            return layout.unified_tile(j * TILES_PER_STEP + h, latent_only)

        def tok(width, tile_of=u_of):
            return pl.BlockSpec((tm, width), lambda j: (tile_of(j), 0))

        in_specs += layout.source_specs(u_of, d, len(src) == 2)
        in_specs += [
            pl.BlockSpec((1, 1, N_MOD, d), lambda j, u_of=u_of: (layer, layout.mod_row(u_of(j)), 0, 0)),
            tok(MLA_WIDTH) if att_unified else tok(MLA_WIDTH, lambda j, h=h: j * TILES_PER_STEP + h),
            tok(SC_WIDTH), tok(SSD_WIDTH), tok(SSD_WIDTH), tok(SSD_WIDTH),
        ]
        operands += list(src) + [mod, att, ysc, z, yf, yb]
    consts = [normg, wout, gpost, gpre2, w1, w2, gpost2]
    in_specs += [_layer_spec(c, layer, single_buffer=True) for c in consts]
    return pl.pallas_call(
        functools.partial(_mix_ffn_kernel, layout=layout, latent_only=latent_only, split=len(src) == 2),
        out_shape=jax.ShapeDtypeStruct((n_out_tiles * tm, d), F32),
        grid=(n_out_tiles // TILES_PER_STEP,),
        in_specs=in_specs,
        out_specs=pl.BlockSpec((TILES_PER_STEP * tm, d), lambda j: (j, 0)),
        scratch_shapes=[pltpu.VMEM((tm, d), BF16), pltpu.VMEM((tm, d), F32), pltpu.VMEM((tm, d), F32),
                        pltpu.VMEM((tm, d), F32), pltpu.VMEM((tm, d), BF16)] * TILES_PER_STEP,
        compiler_params=pltpu.CompilerParams(
            dimension_semantics=("parallel",), vmem_limit_bytes=VMEM_LIMIT_BYTES),
        name="mix_ffn",
    )(*operands, *consts)


def _rope_tables(n_ctx, seq):
    f32 = np.float32
    half = QK_ROPE // 2
    inv_freq = (f32(ROPE_THETA) ** (-np.arange(0, half, 2, dtype=f32) / f32(half))).astype(f32)
    rows = seq // GRID_W
    row = np.repeat(np.arange(rows, dtype=f32), GRID_W)
    col = np.tile(np.arange(GRID_W, dtype=f32), rows)
    ang_r = row[:, None] * inv_freq
    ang_c = col[:, None] * inv_freq
    ang = np.concatenate([ang_r, ang_r, ang_c, ang_c], axis=-1).astype(f32)
    t = n_ctx + seq
    cos_t = np.ones((t, LANES), f32)
    sin_t = np.zeros((t, LANES), f32)
    cos_t[n_ctx:, ROPE_LANE0:ROPE_LANE0 + QK_ROPE] = np.cos(ang)
    sin_t[n_ctx:, ROPE_LANE0:ROPE_LANE0 + QK_ROPE] = np.sin(ang)
    return jnp.asarray(cos_t), jnp.asarray(sin_t)


def kernel(x, c, ctx, c_ctx, w_mod, b_mod, g_pre_mix, w_in, mla_q_norm, w_uq, mla_kv_norm, w_ukv, sc_conv_w, ssd_conv_w, ssd_conv_b, ssd_a_log, ssd_dt_bias, ssd_d, ssd_norm, w_out, g_post_mix, g_pre_ffn, w_ff1, w_ff2, g_post_ffn):
    bsz, seq, d = x.shape
    n_ctx = ctx.shape[1]
    depth = w_mod.shape[0]
    assert n_ctx == TOKEN_TILE and seq % TOKEN_TILE == 0 and seq % GRID_W == 0
    layout = _TokenLayout(bsz, n_ctx, seq)
    t = n_ctx + seq

    cvec = jnp.zeros((SUBLANES, d), F32).at[:bsz].set(c).at[layout.ctx_row].set(c_ctx)
    mod_all = _modulation(cvec, w_mod, b_mod).reshape(depth, SUBLANES, N_MOD, d)
    cos_t, sin_t = _rope_tables(n_ctx, seq)

    src = (ctx.reshape(bsz * n_ctx, d), x.reshape(bsz * seq, d))

    def rows3(a):
        return a.reshape(depth, 1, -1)

    def dt_lanes(a):
        flat = a.reshape(depth, 1, -1)
        return jnp.pad(flat, ((0, 0), (0, 0), (DT_LANE0, LANES - DT_LANE0 - flat.shape[-1])))

    alog, dtbias = dt_lanes(ssd_a_log), dt_lanes(ssd_dt_bias)
    dskip = rows3(jnp.repeat(ssd_d, SSD_HEAD_DIM, axis=-1))
    w_out_b, w_ff1_b, w_ff2_b = w_out.astype(BF16), w_ff1.astype(BF16), w_ff2.astype(BF16)
    w_in_t = jnp.swapaxes(w_in, 1, 2)

    def per_batch(a):
        return a.reshape((bsz, t) + a.shape[1:])

    for i in range(depth):
        last = i == depth - 1
        qt, k, vt, kn, qn, ysc, z, act, cs, side = _inproj(
            src, mod_all, i, rows3(g_pre_mix), w_in_t, rows3(mla_q_norm), w_uq, rows3(mla_kv_norm), w_ukv,
            cos_t, sin_t, sc_conv_w, ssd_conv_w, rows3(ssd_conv_b), alog, dtbias, layout)
        att = _attention(qt, per_batch(k), vt, kn.reshape(bsz, layout.tiles_per_batch, SUBLANES, LANES), qn,
                         n_ctx, ctx_queries=not last)
        yf, yb = _ssd_scan(per_batch(act), per_batch(cs), side.reshape(bsz, t // SSD_CHUNK, SIDE_ROWS, SSD_CHUNK),
                           i, dskip, n_ctx)
        out = _mix_ffn(src, mod_all, i, att.reshape(-1, MLA_WIDTH), ysc, z,
                       yf.reshape(-1, SSD_WIDTH), yb.reshape(-1, SSD_WIDTH), rows3(ssd_norm),
                       w_out_b, rows3(g_post_mix), rows3(g_pre_ffn), w_ff1_b, w_ff2_b, rows3(g_post_ffn),
                       layout, latent_only=last)
        src = (out,)
    return out.reshape(bsz, seq, d)
```

```python
import functools
import math

import jax
import jax.numpy as jnp
import numpy as np
from jax import lax
from jax.experimental import pallas as pl
from jax.experimental.pallas import tpu as pltpu

F32 = jnp.float32
BF16 = jnp.bfloat16

GRID_W = 64
EPS = 1e-6
N_MOD = 6
MLA_HEADS = 4
Q_LORA = 256
KV_LORA = 128
QK_NOPE = 64
QK_ROPE = 32
V_HEAD = 64
MLA_WIDTH = MLA_HEADS * V_HEAD
MLA_SCALE = (QK_NOPE + QK_ROPE) ** -0.5
ROPE_THETA = 10000.0
SC_WIDTH = 256
SSD_HEADS = 8
SSD_HEAD_DIM = 64
SSD_WIDTH = SSD_HEADS * SSD_HEAD_DIM
SSD_GROUPS = 2
SSD_STATE = 64
SSD_CHUNK = 128
SSD_GN = SSD_GROUPS * SSD_STATE
SSD_XBC = SSD_WIDTH + 2 * SSD_GN
HEADS_PER_GROUP = SSD_HEADS // SSD_GROUPS
IN_MLA = Q_LORA + KV_LORA + QK_ROPE
IN_SC = 3 * SC_WIDTH

LANES = 128
SUBLANES = 8
VMEM_LIMIT_BYTES = 56 * 1024 * 1024

HEAD_PAD = LANES
V_ROWS = V_HEAD + 16
ROPE_LANE0 = QK_NOPE
TOKEN_TILE = 256
TILES_PER_STEP = 2
INPROJ_TILES_PER_STEP = 4
SSD_BLOCK = 2 * SSD_CHUNK
FF_CHUNK = 1024

C_Q0 = 0
C_KV0 = C_Q0 + Q_LORA
C_KR0 = C_KV0 + KV_LORA
C_SC0 = C_KR0 + LANES
C_Z0 = C_SC0 + IN_SC
C_XBC0 = C_Z0 + SSD_WIDTH
D_IN_PAD = C_XBC0 + SSD_XBC
DT_LANE0 = QK_ROPE

LOG2E = math.log2(math.e)
Q_PRESCALE = MLA_SCALE * LOG2E

SHIFT_HEADROOM = 60.0
SHIFT_MAX_BOUND = 90.0
BOUND_SLACK = 1.0 + 2.0 ** -8


def _rms(x, g):
    return x * lax.rsqrt(jnp.mean(x * x, axis=-1, keepdims=True) + EPS) * g


def _silu(x):
    return x / (1.0 + jnp.exp(-x))


def _dot(a, b):
    return jnp.dot(a, b, preferred_element_type=F32)


class _TokenLayout:
    def __init__(self, bsz, n_ctx, seq):
        tm = TOKEN_TILE
        assert n_ctx % tm == 0 and seq % tm == 0
        self.bsz = bsz
        self.ctx_tiles = n_ctx // tm
        self.lat_tiles = seq // tm
        self.tiles_per_batch = self.ctx_tiles + self.lat_tiles
        self.n_tiles = bsz * self.tiles_per_batch
        self.ctx_row = bsz

    def unified_tile(self, o, latent_only):
        if not latent_only:
            return o
        return o + (o // self.lat_tiles + 1) * self.ctx_tiles

    def coords(self, u):
        b = u // self.tiles_per_batch
        return b, u - b * self.tiles_per_batch

    def mod_row(self, u):
        b, i = self.coords(u)
        return jnp.where(i < self.ctx_tiles, self.ctx_row, b)

    def seq_edges(self, u):
        _, i = self.coords(u)
        first = (i == 0) | (i == self.ctx_tiles)
        last = (i == self.ctx_tiles - 1) | (i == self.tiles_per_batch - 1)
        return first, last

    def _latent_tile(self, u):
        b, i = self.coords(u)
        return b * self.lat_tiles + jnp.maximum(i - self.ctx_tiles, 0)

    def _context_tile(self, u):
        b, i = self.coords(u)
        return b * self.ctx_tiles + jnp.minimum(i, self.ctx_tiles - 1)

    def source_specs(self, u_of, d, split):
        tm = TOKEN_TILE
        if not split:
            return [pl.BlockSpec((tm, d), lambda j: (u_of(j), 0))]
        return [pl.BlockSpec((tm, d), lambda j: (self._context_tile(u_of(j)), 0)),
                pl.BlockSpec((tm, d), lambda j: (self._latent_tile(u_of(j)), 0))]

    def halo_specs(self, u_of, d, split):
        rb = TOKEN_TILE // SUBLANES
        if split:
            assert self.ctx_tiles == 1
            tile_of, n_rb = (lambda j: self._latent_tile(u_of(j))), self.bsz * self.lat_tiles * rb
        else:
            tile_of, n_rb = u_of, self.n_tiles * rb
        return [pl.BlockSpec((SUBLANES, d), lambda j: (jnp.maximum(tile_of(j) * rb - 1, 0), 0)),
                pl.BlockSpec((SUBLANES, d), lambda j: (jnp.minimum((tile_of(j) + 1) * rb, n_rb - 1), 0))]

    def select_source(self, u, refs):
        if len(refs) == 1:
            return refs[0][...]
        _, i = self.coords(u)
        return jnp.where(i < self.ctx_tiles, refs[0][...], refs[1][...])


def _mod_kernel(c_ref, w_ref, b_ref, o_ref):
    s = _silu(c_ref[...]).astype(BF16)
    o_ref[0] = _dot(s, w_ref[0].astype(BF16)) + b_ref[0]


def _modulation(cvec, w_mod, b_mod):
    depth, d, nd = w_mod.shape
    rows = cvec.shape[0]
    return pl.pallas_call(
        _mod_kernel,
        out_shape=jax.ShapeDtypeStruct((depth, rows, nd), F32),
        grid=(depth, nd // d),
        in_specs=[
            pl.BlockSpec((rows, d), lambda l, j: (0, 0)),
            pl.BlockSpec((1, d, d), lambda l, j: (l, 0, j)),
            pl.BlockSpec((1, 1, d), lambda l, j: (l, 0, j)),
        ],
        out_specs=pl.BlockSpec((1, rows, d), lambda l, j: (l, 0, j)),
        compiler_params=pltpu.CompilerParams(dimension_semantics=("parallel", "parallel")),
        name="modulation",
    )(cvec, w_mod, b_mod.reshape(depth, 1, nd))


def _conv3_rows(a_ext, w_ref, tm):
    te = a_ext.shape[0]
    lo, hi = SUBLANES, SUBLANES + tm
    prev = pltpu.roll(a_ext, 1, 0)[lo:hi]
    nxt = pltpu.roll(a_ext, te - 1, 0)[lo:hi]
    return w_ref[0:1, :] * prev + w_ref[1:2, :] * a_ext[lo:hi] + w_ref[2:3, :] * nxt


N_INPROJ_PARAMS = 11
N_INPROJ_TOKEN_INPUTS = 5


def _split3(x):
    hi = x.astype(BF16)
    r = x - hi.astype(F32)
    mid = r.astype(BF16)
    lo = (r - mid.astype(F32)).astype(BF16)
    return hi, mid, lo


SIDE_ROWS = 2 * 2 * SSD_HEADS


def _ssd_decay_tables(dt_raw, alog_ref, dtbias_ref, cs_ref, side_ref, slot):
    tc = SSD_CHUNK
    tm = dt_raw.shape[0]
    nd = 2 * SSD_HEADS
    a = -jnp.exp(alog_ref[...])
    xb = dt_raw + dtbias_ref[...]
    dt = jnp.maximum(xb, 0.0) + jnp.log(1.0 + jnp.exp(-jnp.abs(xb)))
    la = dt * a * LOG2E
    r_i = lax.broadcasted_iota(jnp.int32, (tc, tc), 0)
    c_i = lax.broadcasted_iota(jnp.int32, (tc, tc), 1)
    tri_f = jnp.where(c_i <= r_i, 1.0, 0.0).astype(BF16)
    tri_b = jnp.where(c_i >= r_i, 1.0, 0.0).astype(BF16)
    lane = lax.broadcasted_iota(jnp.int32, (1, LANES), 1)
    fwd_lane = lane < DT_LANE0 + SSD_HEADS
    row = lax.broadcasted_iota(jnp.int32, (nd, 1), 0)
    for c in range(tm // tc):
        parts = _split3(la[c * tc:(c + 1) * tc])
        cs_f = sum(_dot(tri_f, p) for p in parts)
        cs_b = sum(_dot(tri_b, p) for p in parts)
        cs = jnp.where(fwd_lane, cs_f, cs_b)
        cs_ref[slot * tm + c * tc:slot * tm + (c + 1) * tc, :] = cs
        cs_t = cs.T[DT_LANE0:DT_LANE0 + nd]
        dt_t = dt[c * tc:(c + 1) * tc].T[DT_LANE0:DT_LANE0 + nd]
        cs_end = jnp.where(row < SSD_HEADS, cs_t[:, tc - 1:tc], cs_t[:, 0:1])
        w_t = jnp.exp2(cs_end - cs_t) * dt_t
        side_ref[slot * (tm // tc) + c] = jnp.concatenate([cs_t - jnp.log(dt_t) * LOG2E, w_t], axis=0)


def _rotate_cols(w):
    a, b, c, d = jnp.split(w, 4, axis=-1)
    return jnp.concatenate([-b, a, -d, c], axis=-1)


def _relayout_weights(win_raw, wuq_raw, wukv_raw, win_ref, wuq_ref, wukv_ref):
    d = win_raw.shape[1]
    w_kr = win_raw[Q_LORA + KV_LORA:IN_MLA, :]
    w_dt = win_raw[IN_MLA + IN_SC + SSD_WIDTH + SSD_XBC:, :]
    a, b, c, e = (w_kr[i * SUBLANES:(i + 1) * SUBLANES] for i in range(4))
    kr_block = jnp.concatenate([
        -b, a, -e, c,
        w_dt, jnp.zeros((LANES // 2 - QK_ROPE - w_dt.shape[0], d), F32),
        w_kr, jnp.zeros((LANES // 2 - QK_ROPE, d), F32)], axis=0)
    win_ref[:, C_KR0:C_SC0] = kr_block.T.astype(BF16)
    step = 2 * LANES
    for src0, dst0, n_rows in ((0, C_Q0, Q_LORA + KV_LORA), (IN_MLA, C_SC0, IN_SC + SSD_WIDTH + SSD_XBC)):
        for off in range(0, n_rows, step):
            rows = min(step, n_rows - off)
            blk = win_raw[src0 + off:src0 + off + rows, :]
            win_ref[:, dst0 + off:dst0 + off + rows] = blk.T.astype(BF16)

    dqk = QK_NOPE + QK_ROPE
    wq = wuq_raw[...]
    nq = wq.shape[0]
    plain, rot = [], []
    for h in range(MLA_HEADS):
        wh = wq[:, h * dqk:(h + 1) * dqk]
        pad = jnp.zeros((nq, HEAD_PAD - dqk), F32)
        plain += [wh, pad]
        rot += [jnp.zeros((nq, QK_NOPE), F32), _rotate_cols(wh[:, QK_NOPE:]), pad]
    wuq_ref[...] = jnp.concatenate(plain + rot, axis=1).astype(BF16)

    dkv = QK_NOPE + V_HEAD
    wkv = wukv_raw[...]
    nkv = wkv.shape[0]
    ks, vs = [], []
    for h in range(MLA_HEADS):
        wh = wkv[:, h * dkv:(h + 1) * dkv]
        ks += [wh[:, :QK_NOPE], jnp.zeros((nkv, HEAD_PAD - QK_NOPE), F32)]
        vs.append(wh[:, QK_NOPE:])
    wukv_ref[...] = jnp.concatenate(ks + vs, axis=1).astype(BF16)


def _inproj_kernel(*refs, layout, split, tiles):
    n_x = 2 if split else 1
    per_half = n_x + N_INPROJ_TOKEN_INPUTS
    n_in = tiles * per_half
    (gpre, win_raw, qg, wuq_raw, kvg, wukv_raw, scw, cw, cb, alog, dtbias) = [
        r.at[0] for r in refs[n_in:n_in + N_INPROJ_PARAMS]]
    n_out = len(refs) - n_in - N_INPROJ_PARAMS - 3
    outs = refs[n_in + N_INPROJ_PARAMS:n_in + N_INPROJ_PARAMS + n_out]
    win_ref, wuq_ref, wukv_ref = refs[n_in + N_INPROJ_PARAMS + n_out:]

    @pl.when(pl.program_id(0) == 0)
    def _():
        _relayout_weights(win_raw, wuq_raw, wukv_raw, win_ref, wuq_ref, wukv_ref)

    params = (qg, wuq_ref, kvg, wukv_ref, scw, cw, cb, alog, dtbias)
    for h in range(tiles):
        half = refs[h * per_half:(h + 1) * per_half]
        u = pl.program_id(0) * tiles + h
        x = layout.select_source(u, half[:n_x])
        xp_ref, xn_ref, mod_ref, cos_ref, sin_ref = half[n_x:]
        h_ext = _inproj_normalise(x, xp_ref, xn_ref, mod_ref.at[0], gpre, layout.seq_edges(u))
        u_ext = _dot(h_ext, win_ref[...])
        _inproj_finish(u_ext, cos_ref, sin_ref, params, outs, h)


def _inproj_normalise(x, xp_ref, xn_ref, mod_ref, gpre_ref, seq_edges):
    first_of_seq, last_of_seq = seq_edges
    shift = mod_ref[0, 0:1, :]
    scale = mod_ref[0, 1:2, :]

    def norm_mod(xx):
        return _rms(xx, gpre_ref[...]) * (1.0 + scale) + shift

    h_prev = jnp.where(first_of_seq, 0.0, norm_mod(xp_ref[...]))
    h_next = jnp.where(last_of_seq, 0.0, norm_mod(xn_ref[...]))
    return jnp.concatenate([h_prev, norm_mod(x), h_next], axis=0).astype(BF16)


def _inproj_finish(u_ext, cos_ref, sin_ref, params, outs, slot):
    qg_ref, wuq_ref, kvg_ref, wukv_ref, scw_ref, cw_ref, cb_ref, alog_ref, dtbias_ref = params
    qt_ref, k_ref, vt_ref, kn_ref, qn_ref, ysc_ref, z_ref, act_ref, cs_ref, side_ref = outs
    tm = u_ext.shape[0] - 2 * SUBLANES
    rows = slice(slot * tm, (slot + 1) * tm)
    own = slice(SUBLANES, SUBLANES + tm)

    z_ref[rows, :] = u_ext[own, C_Z0:C_XBC0].astype(z_ref.dtype)
    lane = lax.broadcasted_iota(jnp.int32, (1, LANES), 1)
    krb = u_ext[own, C_KR0:C_SC0]
    dt_raw = jnp.where((lane >= DT_LANE0) & (lane < DT_LANE0 + 2 * SSD_HEADS), krb, 0.0)
    _ssd_decay_tables(dt_raw, alog_ref, dtbias_ref, cs_ref, side_ref, slot)

    conv = _conv3_rows(u_ext[:, C_XBC0:D_IN_PAD], cw_ref, tm) + cb_ref[...]
    act_ref[rows, :] = _silu(conv).astype(act_ref.dtype)

    w = SC_WIDTH
    prod = u_ext[:, C_SC0 + w:C_SC0 + 2 * w] * u_ext[:, C_SC0 + 2 * w:C_Z0]
    ysc_ref[rows, :] = (u_ext[own, C_SC0:C_SC0 + w] * _conv3_rows(prod, scw_ref, tm)).astype(ysc_ref.dtype)

    cos = cos_ref[...]
    sin = sin_ref[...]
    nh = MLA_HEADS
    cos_h = jnp.concatenate([cos] * nh, axis=1)
    sin_h = jnp.concatenate([sin] * nh, axis=1)

    def head_norm_maxima(a_f32):
        out = []
        for h in range(nh):
            a_h = a_f32[:, h * HEAD_PAD:(h + 1) * HEAD_PAD]
            n2 = jnp.max(jnp.sum(a_h * a_h, axis=1, keepdims=True), axis=0, keepdims=True)
            out.append(jnp.broadcast_to(n2, (1, LANES)))
        return jnp.concatenate(out + [jnp.zeros((SUBLANES - nh, LANES), F32)], axis=0)

    cq = _rms(u_ext[own, C_Q0:C_KV0], qg_ref[...]).astype(BF16)
    q2 = _dot(cq, wuq_ref[...])
    qw = nh * HEAD_PAD
    q_b = ((q2[:, :qw] * cos_h + q2[:, qw:] * sin_h) * Q_PRESCALE).astype(BF16)
    q_f = q_b.astype(F32)
    qt_ref[:, rows] = q_f.T.astype(BF16)
    qn_ref[slot] = head_norm_maxima(q_f)

    ckv = _rms(u_ext[own, C_KV0:C_KR0], kvg_ref[...]).astype(BF16)
    kv = _dot(ckv, wukv_ref[...])
    rope_lane = (lane >= ROPE_LANE0) & (lane < ROPE_LANE0 + QK_ROPE)
    cos_k = jnp.where(rope_lane, cos, 0.0)
    kr = krb * cos_k + pltpu.roll(krb, LANES // 2, 1) * sin
    k = kv[:, :qw] + jnp.concatenate([kr] * nh, axis=1)
    k_b = k.astype(BF16)
    k_ref[rows, :] = k_b
    kn_ref[slot] = head_norm_maxima(k_b.astype(F32))
    v_t = kv[:, qw:].T.astype(BF16)
    for h in range(nh):
        vt_ref[h, 0:V_HEAD, rows] = v_t[h * V_HEAD:(h + 1) * V_HEAD]
        vt_ref[h, V_HEAD:V_ROWS, rows] = jnp.ones((V_ROWS - V_HEAD, tm), BF16)


def _layer_spec(a, layer, single_buffer=False):
    block = (1,) + a.shape[1:]
    index_map = lambda j: (layer,) + (0,) * (a.ndim - 1)
    if single_buffer:
        return pl.BlockSpec(block, index_map, pipeline_mode=pl.Buffered(1))
    return pl.BlockSpec(block, index_map)


def _inproj(src, mod, layer, gpre, win, qg, wuq, kvg, wukv, cos_t, sin_t, scw, cw, cb, alog, dtbias, layout):
    tm = TOKEN_TILE
    d = src[0].shape[-1]
    split = len(src) == 2
    qw = MLA_HEADS * HEAD_PAD
    nt = layout.n_tiles
    tiles = INPROJ_TILES_PER_STEP
    assert nt % tiles == 0
    rows = nt * tm

    in_specs, operands = [], []
    for h in range(tiles):
        def u_of(j, h=h):
            return j * tiles + h

        def rope(j, u_of=u_of):
            return (layout.coords(u_of(j))[1], 0)

        in_specs += layout.source_specs(u_of, d, split) + layout.halo_specs(u_of, d, split)
        in_specs += [
            pl.BlockSpec((1, 1, N_MOD, d), lambda j, u_of=u_of: (layer, layout.mod_row(u_of(j)), 0, 0)),
            pl.BlockSpec((tm, LANES), rope), pl.BlockSpec((tm, LANES), rope),
        ]
        operands += list(src) + [src[-1], src[-1], mod, cos_t, sin_t]
    consts = [gpre, win, qg, wuq, kvg, wukv, scw, cw, cb, alog, dtbias]
    in_specs += [_layer_spec(c, layer, single_buffer=c is win) for c in consts]
    step_rows = tiles * tm
    chunks_per_step = step_rows // SSD_CHUNK

    def tok(width):
        return pl.BlockSpec((step_rows, width), lambda j: (j, 0))

    norm_shape = jax.ShapeDtypeStruct((nt, SUBLANES, LANES), F32)
    norm_spec = pl.BlockSpec((tiles, SUBLANES, LANES), lambda j: (j, 0, 0))
    out_shape = (
        jax.ShapeDtypeStruct((qw, rows), BF16),
        jax.ShapeDtypeStruct((rows, qw), BF16),
        jax.ShapeDtypeStruct((MLA_HEADS, V_ROWS, rows), BF16),
        norm_shape,
        norm_shape,
        jax.ShapeDtypeStruct((rows, SC_WIDTH), BF16),
        jax.ShapeDtypeStruct((rows, SSD_WIDTH), BF16),
        jax.ShapeDtypeStruct((rows, SSD_XBC), BF16),
        jax.ShapeDtypeStruct((rows, LANES), F32),
        jax.ShapeDtypeStruct((rows // SSD_CHUNK, SIDE_ROWS, SSD_CHUNK), F32),
    )
    return pl.pallas_call(
        functools.partial(_inproj_kernel, layout=layout, split=split, tiles=tiles),
        out_shape=out_shape,
        grid=(nt // tiles,),
        in_specs=in_specs,
        out_specs=(pl.BlockSpec((qw, step_rows), lambda j: (0, j)), tok(qw),
                   pl.BlockSpec((MLA_HEADS, V_ROWS, step_rows), lambda j: (0, 0, j)),
                   norm_spec, norm_spec,
                   tok(SC_WIDTH), tok(SSD_WIDTH), tok(SSD_XBC), tok(LANES),
                   pl.BlockSpec((chunks_per_step, SIDE_ROWS, SSD_CHUNK), lambda j: (j, 0, 0))),
        scratch_shapes=[pltpu.VMEM((d, D_IN_PAD), BF16),
                        pltpu.VMEM((Q_LORA, 2 * qw), BF16),
                        pltpu.VMEM((KV_LORA, qw + MLA_WIDTH), BF16)],
        compiler_params=pltpu.CompilerParams(
            dimension_semantics=("arbitrary",), vmem_limit_bytes=VMEM_LIMIT_BYTES),
        name="inproj",
    )(*operands, *consts)


def _attn_kernel(*refs, n_keys, n_ctx, ctx_queries, q_tiles):
    qt_refs = refs[:q_tiles]
    k_ref, vt_ref, kn_ref = refs[q_tiles:q_tiles + 3]
    qn_refs = refs[q_tiles + 3:2 * q_tiles + 3]
    o_ref, s_ref, p_ref = refs[2 * q_tiles + 3:]
    i = pl.program_id(1)
    kn = jnp.max(kn_ref[0], axis=0)
    for t in range(q_tiles):
        _attend_tile(qt_refs[t], qn_refs[t], k_ref, vt_ref, kn, o_ref, t, s_ref, p_ref, i,
                     n_keys=n_keys, n_ctx=n_ctx, ctx_queries=ctx_queries)


def _attend_tile(q_t, qn_ref, k_ref, vt_ref, kn, o_ref, slot, s_ref, p_ref, i, *, n_keys, n_ctx, ctx_queries):
    tq = q_t.shape[1]

    def finish(outs):
        o_ref[0, slot * tq:(slot + 1) * tq, :] = jnp.concatenate(outs, axis=0).T.astype(o_ref.dtype)

    def head_out(ov):
        return ov[0:V_HEAD] / ov[V_HEAD:V_HEAD + 1]

    def attend_two_pass(nk):
        def scores(h):
            s_ref[h, 0:nk, :] = _dot(k_ref[0, 0:nk, h * HEAD_PAD:(h + 1) * HEAD_PAD],
                                     q_t[h * HEAD_PAD:(h + 1) * HEAD_PAD, :])

        def probs(h):
            s = s_ref[h, 0:nk, :]
            p_ref[h, 0:nk, :] = jnp.exp2(s - jnp.max(s, axis=0, keepdims=True)).astype(BF16)

        def values(h):
            return head_out(_dot(vt_ref[h, :, 0:nk], p_ref[h, 0:nk, :]))

        scores(0), scores(1)
        scores(2), scores(3), probs(0), probs(1)
        outs = [values(0), values(1)]
        probs(2), probs(3)
        outs += [values(2), values(3)]
        finish(outs)

    def attend_one_pass(nk, shift):
        outs = []
        for pair in range(MLA_HEADS // 2):
            heads = (2 * pair, 2 * pair + 1)
            s = [_dot(k_ref[0, 0:nk, h * HEAD_PAD:(h + 1) * HEAD_PAD],
                      q_t[h * HEAD_PAD:(h + 1) * HEAD_PAD, :]) for h in heads]
            for h, sh in zip(heads, s):
                p_ref[h, 0:nk, :] = jnp.exp2(sh - shift[h]).astype(BF16)
            outs += [head_out(_dot(vt_ref[h, :, 0:nk], p_ref[h, 0:nk, :])) for h in heads]
        finish(outs)

    def attend(nk):
        bound = jnp.sqrt(qn_ref[0] * kn) * BOUND_SLACK
        one_pass = jnp.max(bound) <= SHIFT_MAX_BOUND

        @pl.when(one_pass)
        def _():
            attend_one_pass(nk, [bound[h:h + 1, 0:1] - SHIFT_HEADROOM for h in range(MLA_HEADS)])

        @pl.when(jnp.logical_not(one_pass))
        def _():
            attend_two_pass(nk)

    if ctx_queries:
        @pl.when(i == 0)
        def _():
            attend_two_pass(n_ctx)

        @pl.when(i > 0)
        def _():
            attend(n_keys)
    else:
        attend(n_keys)


def _attention(qt, k, vt, kn, qn, n_ctx, ctx_queries):
    bsz, t, qw = k.shape
    tq = TOKEN_TILE
    tiles = t // tq
    t0 = 0 if ctx_queries else n_ctx // tq
    nq = tiles - t0
    q_tiles = 1 if ctx_queries else 2
    assert nq % q_tiles == 0

    def tile_of(b, i, s):
        return b * tiles + i * q_tiles + s + t0

    return pl.pallas_call(
        functools.partial(_attn_kernel, n_keys=t, n_ctx=n_ctx, ctx_queries=ctx_queries, q_tiles=q_tiles),
        out_shape=jax.ShapeDtypeStruct((bsz, nq * tq, MLA_WIDTH), BF16),
        grid=(bsz, nq // q_tiles),
        in_specs=[pl.BlockSpec((qw, tq), lambda b, i, s=s: (0, tile_of(b, i, s))) for s in range(q_tiles)] + [
            pl.BlockSpec((1, t, qw), lambda b, i: (b, 0, 0)),
            pl.BlockSpec((MLA_HEADS, V_ROWS, t), lambda b, i: (0, 0, b)),
            pl.BlockSpec((1,) + kn.shape[1:], lambda b, i: (b, 0, 0, 0)),
        ] + [pl.BlockSpec((1, SUBLANES, LANES), lambda b, i, s=s: (tile_of(b, i, s), 0, 0)) for s in range(q_tiles)],
        out_specs=pl.BlockSpec((1, q_tiles * tq, MLA_WIDTH), lambda b, i: (b, i, 0)),
        scratch_shapes=[pltpu.VMEM((MLA_HEADS, t, tq), F32), pltpu.VMEM((MLA_HEADS, t, tq), BF16)],
        compiler_params=pltpu.CompilerParams(
            dimension_semantics=("parallel", "parallel"), vmem_limit_bytes=VMEM_LIMIT_BYTES),
        name="attention",
    )(*([qt] * q_tiles), k, vt, kn, *([qn] * q_tiles))


PAIRS_PER_CHUNK = SSD_HEADS // 2


def _ssd_build_lhs(act, cs, side, direction, lhs_ref, base):
    tc = SSD_CHUNK
    n = SSD_STATE
    bm = act[:, SSD_WIDTH:SSD_WIDTH + SSD_GN].astype(F32)
    cm = act[:, SSD_WIDTH + SSD_GN:]
    bm_t = bm.T

    r_i = lax.broadcasted_iota(jnp.int32, (tc, tc), 0)
    c_i = lax.broadcasted_iota(jnp.int32, (tc, tc), 1)
    tri = (c_i <= r_i) if direction == 0 else (c_i >= r_i)
    nd = 2 * SSD_HEADS
    csd_t, w_t = side[0:nd], side[nd:2 * nd]
    cm_f = cm.astype(F32)
    zeros_lhs = jnp.zeros((n, tc), BF16)
    m = tc + n
    for g in range(SSD_GROUPS):
        cb = _dot(cm[:, g * n:(g + 1) * n], bm_t[g * n:(g + 1) * n, :].astype(BF16))
        bt_g = bm_t[g * n:(g + 1) * n, :]
        for pr in range(HEADS_PER_GROUP // 2):
            slab = g * (HEADS_PER_GROUP // 2) + pr
            for hh in range(2):
                r = direction * SSD_HEADS + 2 * slab + hh
                col = DT_LANE0 + r
                cs_col = jnp.broadcast_to(cs[:, col:col + 1], (tc, tc))
                lmd = jnp.exp2(jnp.where(tri, cs_col - csd_t[r:r + 1, :], -jnp.inf))
                lhs_ref[base + slab, hh * m:hh * m + tc, 0:tc] = (cb * lmd).astype(BF16)
                lhs_ref[base + slab, hh * m:hh * m + tc, tc:2 * tc] = (cm_f * jnp.exp2(cs_col)).astype(BF16)
                lhs_ref[base + slab, hh * m + tc:(hh + 1) * m, 0:tc] = (bt_g * w_t[r:r + 1, :]).astype(BF16)
                lhs_ref[base + slab, hh * m + tc:(hh + 1) * m, tc:2 * tc] = zeros_lhs


def _ssd_apply(act, cs, direction, dskip_ref, y_ref, row0, h_ref, lhs_ref, base):
    tc = SSD_CHUNK
    n = SSD_STATE
    m = tc + n
    end = tc - 1 if direction == 0 else 0
    dec_tot = jnp.exp2(cs[end:end + 1, :])
    lane = lax.broadcasted_iota(jnp.int32, (1, LANES), 1)
    low_half = lane < SSD_HEAD_DIM
    zeros_h = jnp.zeros((n, LANES), BF16)
    for g in range(SSD_GROUPS):
        for pr in range(HEADS_PER_GROUP // 2):
            slab = g * (HEADS_PER_GROUP // 2) + pr
            h_pair = h_ref[direction, g, :, pr * LANES:(pr + 1) * LANES]
            h_b = h_pair.astype(BF16)
            xs_pair = act[:, slab * LANES:(slab + 1) * LANES]
            rhs = jnp.concatenate([xs_pair] + [h_b if gg == g else zeros_h for gg in range(SSD_GROUPS)], axis=0)
            out = _dot(lhs_ref[base + slab], rhs)
            y_pair = jnp.where(low_half, out[0:tc], out[m:m + tc])
            if direction == 0:
                y_pair = y_pair + dskip_ref[:, slab * LANES:(slab + 1) * LANES] * xs_pair.astype(F32)
            y_ref[0, row0:row0 + tc, slab * LANES:(slab + 1) * LANES] = y_pair
            h_new = jnp.where(low_half, out[tc:m], out[m + tc:2 * m])
            cols = [DT_LANE0 + direction * SSD_HEADS + 2 * slab + hh for hh in range(2)]
            keep = jnp.where(low_half, dec_tot[:, cols[0]:cols[0] + 1], dec_tot[:, cols[1]:cols[1] + 1])
            h_ref[direction, g, :, pr * LANES:(pr + 1) * LANES] = h_pair * keep + h_new


def _bwd_block(i, n_blocks, ctx_blocks):
    return jnp.where(i < ctx_blocks, ctx_blocks - 1 - i, n_blocks - 1 + ctx_blocks - i)


def _ssd_kernel(af_ref, ab_ref, csf_ref, csb_ref, sidef_ref, sideb_ref, dskip_ref, yf_ref, yb_ref,
                h_ref, lhs_ref):
    @pl.when(pl.program_id(1) == 0)
    def _():
        h_ref[...] = jnp.zeros_like(h_ref)

    dskip = dskip_ref.at[0]
    tc = SSD_CHUNK
    n_sub = af_ref.shape[1] // tc
    roles = []
    for j in range(n_sub):
        jb = n_sub - 1 - j
        roles.append((af_ref, csf_ref, sidef_ref, j, 0, yf_ref))
        roles.append((ab_ref, csb_ref, sideb_ref, jb, 1, yb_ref))
    for idx, (a_ref, c_ref, s_ref, j, direction, _) in enumerate(roles):
        r0 = j * tc
        _ssd_build_lhs(a_ref[0, r0:r0 + tc, :], c_ref[0, r0:r0 + tc, :], s_ref[0, j], direction,
                       lhs_ref, idx * PAIRS_PER_CHUNK)
    for idx, (a_ref, c_ref, _, j, direction, y_ref) in enumerate(roles):
        r0 = j * tc
        _ssd_apply(a_ref[0, r0:r0 + tc, :], c_ref[0, r0:r0 + tc, :], direction, dskip, y_ref, r0, h_ref,
                   lhs_ref, idx * PAIRS_PER_CHUNK)


def _ssd_scan(act, cs, side, layer, dskip, n_ctx):
    bsz, t, _ = act.shape
    tc = SSD_BLOCK
    n_blocks = t // tc
    ctx_blocks = n_ctx // tc
    assert n_ctx % tc == 0 and t % tc == 0

    def fwd(i):
        return i

    def bwd(i):
        return _bwd_block(i, n_blocks, ctx_blocks)

    def main(order, width):
        return pl.BlockSpec((1, tc, width), lambda b, i: (b, order(i), 0))

    def side_spec(order):
        return pl.BlockSpec((1, tc // SSD_CHUNK, SIDE_ROWS, SSD_CHUNK), lambda b, i: (b, order(i), 0, 0))

    def const(shape):
        return pl.BlockSpec((1,) + shape[1:], lambda b, i: (layer,) + (0,) * (len(shape) - 1))

    y_shape = jax.ShapeDtypeStruct((bsz, t, SSD_WIDTH), F32)
    return pl.pallas_call(
        _ssd_kernel,
        out_shape=(y_shape, y_shape),
        grid=(bsz, n_blocks),
        in_specs=[
            main(fwd, SSD_XBC), main(bwd, SSD_XBC), main(fwd, LANES), main(bwd, LANES),
            side_spec(fwd), side_spec(bwd),
            const(dskip.shape),
        ],
        out_specs=(main(fwd, SSD_WIDTH), main(bwd, SSD_WIDTH)),
        scratch_shapes=[pltpu.VMEM((2, SSD_GROUPS, SSD_STATE, HEADS_PER_GROUP * SSD_HEAD_DIM), F32),
                        pltpu.VMEM((2 * (tc // SSD_CHUNK) * PAIRS_PER_CHUNK, 2 * (SSD_CHUNK + SSD_STATE),
                                    2 * SSD_CHUNK), BF16)],
        compiler_params=pltpu.CompilerParams(
            dimension_semantics=("arbitrary", "arbitrary"), vmem_limit_bytes=VMEM_LIMIT_BYTES),
        name="ssd_scan",
    )(act, act, cs, cs, side, side, dskip)


def _mix_gather(att_ref, ysc_ref, z_ref, yf_ref, yb_ref, normg_ref, ycat_ref):
    gated = (yf_ref[...] + yb_ref[...]) * _silu(z_ref[...].astype(F32))
    gw = SSD_WIDTH // SSD_GROUPS
    w = SC_WIDTH
    ycat_ref[:, 0:MLA_WIDTH] = att_ref[...]
    ycat_ref[:, MLA_WIDTH:MLA_WIDTH + w] = ysc_ref[...]
    c0 = MLA_WIDTH + w
    for g in range(SSD_GROUPS):
        gg = gated[:, g * gw:(g + 1) * gw]
        gg = gg * lax.rsqrt(jnp.mean(gg * gg, axis=-1, keepdims=True) + EPS) * normg_ref[:, g * gw:(g + 1) * gw]
        ycat_ref[:, c0 + g * gw:c0 + (g + 1) * gw] = gg.astype(BF16)


def _mix_residual(x, mod_ref, y_ref, gpost_ref, gpre2_ref, x1_ref, h2_ref):
    gate1 = mod_ref[0, 2:3, :]
    shift2 = mod_ref[0, 3:4, :]
    scale2 = mod_ref[0, 4:5, :]
    x1 = x + gate1 * _rms(y_ref[...], gpost_ref[...])
    x1_ref[...] = x1
    h2_ref[...] = (_rms(x1, gpre2_ref[...]) * (1.0 + scale2) + shift2).astype(BF16)


def _mix_mlp(h2_ref, w1_ref, w2_ref, acc_ref):
    d_ff = w1_ref.shape[1]
    for c in range(d_ff // FF_CHUNK):
        a = _dot(h2_ref[...], w1_ref[:, c * FF_CHUNK:(c + 1) * FF_CHUNK])
        r = jnp.square(jnp.maximum(a, 0.0)).astype(BF16)
        part = _dot(r, w2_ref[c * FF_CHUNK:(c + 1) * FF_CHUNK, :])
        if c == 0:
            acc_ref[...] = part
        else:
            acc_ref[...] += part


N_MIX_PARAMS = 7
N_MIX_TOKEN_INPUTS = 6


def _mix_ffn_kernel(*refs, layout, latent_only, split):
    n_x = 2 if split else 1
    per_half = n_x + N_MIX_TOKEN_INPUTS
    n_in = TILES_PER_STEP * per_half
    normg, wout, gpost, gpre2, w1, w2, gpost2 = [r.at[0] for r in refs[n_in:n_in + N_MIX_PARAMS]]
    o_ref = refs[n_in + N_MIX_PARAMS]
    scratch = refs[n_in + N_MIX_PARAMS + 1:]
    per_tile = len(scratch) // TILES_PER_STEP
    tiles = [scratch[h * per_tile:(h + 1) * per_tile] for h in range(TILES_PER_STEP)]
    tm = TOKEN_TILE
    halves = [refs[h * per_half:(h + 1) * per_half] for h in range(TILES_PER_STEP)]
    mods = [half[n_x].at[0] for half in halves]
    for half, (ycat_ref, y_ref, _, _, _) in zip(halves, tiles):
        _mix_gather(*half[n_x + 1:], normg, ycat_ref)
        y_ref[...] = _dot(ycat_ref[...], wout[...])
    for h, (half, (_, y_ref, acc_ref, x1_ref, h2_ref)) in enumerate(zip(halves, tiles)):
        u = layout.unified_tile(pl.program_id(0) * TILES_PER_STEP + h, latent_only)
        x = layout.select_source(u, half[:n_x])
        _mix_residual(x, mods[h], y_ref, gpost, gpre2, x1_ref, h2_ref)
        _mix_mlp(h2_ref, w1, w2, acc_ref)
    for h, (_, _, acc_ref, x1_ref, _) in enumerate(tiles):
        gate2 = mods[h][0, 5:6, :]
        o_ref[h * tm:(h + 1) * tm, :] = x1_ref[...] + gate2 * _rms(acc_ref[...], gpost2[...])


def _mix_ffn(src, mod, layer, att, ysc, z, yf, yb, normg, wout, gpost, gpre2, w1, w2, gpost2, layout,
             latent_only):
    tm = TOKEN_TILE
    d = src[0].shape[-1]
    n_out_tiles = layout.bsz * (layout.lat_tiles if latent_only else layout.tiles_per_batch)
    assert n_out_tiles % TILES_PER_STEP == 0
    att_unified = att.shape[0] == layout.bsz * layout.tiles_per_batch * tm

    in_specs, operands = [], []
    for h in range(TILES_PER_STEP):
        def u_of(j, h=h):
            return layout.unified_tile(j * TILES_PER_STEP + h, latent_only)

        def tok(width, tile_of=u_of):
            return pl.BlockSpec((tm, width), lambda j: (tile_of(j), 0))

        in_specs += layout.source_specs(u_of, d, len(src) == 2)
        in_specs += [
            pl.BlockSpec((1, 1, N_MOD, d), lambda j, u_of=u_of: (layer, layout.mod_row(u_of(j)), 0, 0)),
            tok(MLA_WIDTH) if att_unified else tok(MLA_WIDTH, lambda j, h=h: j * TILES_PER_STEP + h),
            tok(SC_WIDTH), tok(SSD_WIDTH), tok(SSD_WIDTH), tok(SSD_WIDTH),
        ]
        operands += list(src) + [mod, att, ysc, z, yf, yb]
    consts = [normg, wout, gpost, gpre2, w1, w2, gpost2]
    in_specs += [_layer_spec(c, layer, single_buffer=True) for c in consts]
    return pl.pallas_call(
        functools.partial(_mix_ffn_kernel, layout=layout, latent_only=latent_only, split=len(src) == 2),
        out_shape=jax.ShapeDtypeStruct((n_out_tiles * tm, d), F32),
        grid=(n_out_tiles // TILES_PER_STEP,),
        in_specs=in_specs,
        out_specs=pl.BlockSpec((TILES_PER_STEP * tm, d), lambda j: (j, 0)),
        scratch_shapes=[pltpu.VMEM((tm, d), BF16), pltpu.VMEM((tm, d), F32), pltpu.VMEM((tm, d), F32),
                        pltpu.VMEM((tm, d), F32), pltpu.VMEM((tm, d), BF16)] * TILES_PER_STEP,
        compiler_params=pltpu.CompilerParams(
            dimension_semantics=("parallel",), vmem_limit_bytes=VMEM_LIMIT_BYTES),
        name="mix_ffn",
    )(*operands, *consts)


def _rope_tables(n_ctx, seq):
    f32 = np.float32
    half = QK_ROPE // 2
    inv_freq = (f32(ROPE_THETA) ** (-np.arange(0, half, 2, dtype=f32) / f32(half))).astype(f32)
    rows = seq // GRID_W
    row = np.repeat(np.arange(rows, dtype=f32), GRID_W)
    col = np.tile(np.arange(GRID_W, dtype=f32), rows)
    ang_r = row[:, None] * inv_freq
    ang_c = col[:, None] * inv_freq
    ang = np.concatenate([ang_r, ang_r, ang_c, ang_c], axis=-1).astype(f32)
    t = n_ctx + seq
    cos_t = np.ones((t, LANES), f32)
    sin_t = np.zeros((t, LANES), f32)
    cos_t[n_ctx:, ROPE_LANE0:ROPE_LANE0 + QK_ROPE] = np.cos(ang)
    sin_t[n_ctx:, ROPE_LANE0:ROPE_LANE0 + QK_ROPE] = np.sin(ang)
    return jnp.asarray(cos_t), jnp.asarray(sin_t)


def kernel(x, c, ctx, c_ctx, w_mod, b_mod, g_pre_mix, w_in, mla_q_norm, w_uq, mla_kv_norm, w_ukv, sc_conv_w, ssd_conv_w, ssd_conv_b, ssd_a_log, ssd_dt_bias, ssd_d, ssd_norm, w_out, g_post_mix, g_pre_ffn, w_ff1, w_ff2, g_post_ffn):
    bsz, seq, d = x.shape
    n_ctx = ctx.shape[1]
    depth = w_mod.shape[0]
    assert n_ctx == TOKEN_TILE and seq % TOKEN_TILE == 0 and seq % GRID_W == 0
    layout = _TokenLayout(bsz, n_ctx, seq)
    t = n_ctx + seq

    cvec = jnp.zeros((SUBLANES, d), F32).at[:bsz].set(c).at[layout.ctx_row].set(c_ctx)
    mod_all = _modulation(cvec, w_mod, b_mod).reshape(depth, SUBLANES, N_MOD, d)
    cos_t, sin_t = _rope_tables(n_ctx, seq)

    src = (ctx.reshape(bsz * n_ctx, d), x.reshape(bsz * seq, d))

    def rows3(a):
        return a.reshape(depth, 1, -1)

    def dt_lanes(a):
        flat = a.reshape(depth, 1, -1)
        return jnp.pad(flat, ((0, 0), (0, 0), (DT_LANE0, LANES - DT_LANE0 - flat.shape[-1])))

    alog, dtbias = dt_lanes(ssd_a_log), dt_lanes(ssd_dt_bias)
    dskip = rows3(jnp.repeat(ssd_d, SSD_HEAD_DIM, axis=-1))
    w_out_b, w_ff1_b, w_ff2_b = w_out.astype(BF16), w_ff1.astype(BF16), w_ff2.astype(BF16)
    w_in_t = jnp.swapaxes(w_in, 1, 2)

    def per_batch(a):
        return a.reshape((bsz, t) + a.shape[1:])

    for i in range(depth):
        last = i == depth - 1
        qt, k, vt, kn, qn, ysc, z, act, cs, side = _inproj(
            src, mod_all, i, rows3(g_pre_mix), w_in_t, rows3(mla_q_norm), w_uq, rows3(mla_kv_norm), w_ukv,
            cos_t, sin_t, sc_conv_w, ssd_conv_w, rows3(ssd_conv_b), alog, dtbias, layout)
        att = _attention(qt, per_batch(k), vt, kn.reshape(bsz, layout.tiles_per_batch, SUBLANES, LANES), qn,
                         n_ctx, ctx_queries=not last)
        yf, yb = _ssd_scan(per_batch(act), per_batch(cs), side.reshape(bsz, t // SSD_CHUNK, SIDE_ROWS, SSD_CHUNK),
                           i, dskip, n_ctx)
        out = _mix_ffn(src, mod_all, i, att.reshape(-1, MLA_WIDTH), ysc, z,
                       yf.reshape(-1, SSD_WIDTH), yb.reshape(-1, SSD_WIDTH), rows3(ssd_norm),
                       w_out_b, rows3(g_post_mix), rows3(g_pre_ffn), w_ff1_b, w_ff2_b, rows3(g_post_ffn),
                       layout, latent_only=last)
        src = (out,)
    return out.reshape(bsz, seq, d)
```

```python
import functools
import math

import jax
import jax.numpy as jnp
import numpy as np
from jax import lax
from jax.experimental import pallas as pl
from jax.experimental.pallas import tpu as pltpu

F32 = jnp.float32
BF16 = jnp.bfloat16

GRID_W = 64
EPS = 1e-6
N_MOD = 6
MLA_HEADS = 4
Q_LORA = 256
KV_LORA = 128
QK_NOPE = 64
QK_ROPE = 32
V_HEAD = 64
MLA_WIDTH = MLA_HEADS * V_HEAD
MLA_SCALE = (QK_NOPE + QK_ROPE) ** -0.5
ROPE_THETA = 10000.0
SC_WIDTH = 256
SSD_HEADS = 8
SSD_HEAD_DIM = 64
SSD_WIDTH = SSD_HEADS * SSD_HEAD_DIM
SSD_GROUPS = 2
SSD_STATE = 64
SSD_CHUNK = 128
SSD_GN = SSD_GROUPS * SSD_STATE
SSD_XBC = SSD_WIDTH + 2 * SSD_GN
HEADS_PER_GROUP = SSD_HEADS // SSD_GROUPS
IN_MLA = Q_LORA + KV_LORA + QK_ROPE
IN_SC = 3 * SC_WIDTH

LANES = 128
SUBLANES = 8
VMEM_LIMIT_BYTES = 56 * 1024 * 1024

HEAD_PAD = LANES
V_ROWS = V_HEAD + 16
ROPE_LANE0 = QK_NOPE
TOKEN_TILE = 256
TILES_PER_STEP = 2
INPROJ_TILES_PER_STEP = 4
SSD_BLOCK = 2 * SSD_CHUNK
SSD_BATCH_PER_STEP = 2
FF_CHUNK = 1024

C_Q0 = 0
C_KV0 = C_Q0 + Q_LORA
C_KR0 = C_KV0 + KV_LORA
C_SC0 = C_KR0 + LANES
C_Z0 = C_SC0 + IN_SC
C_XBC0 = C_Z0 + SSD_WIDTH
D_IN_PAD = C_XBC0 + SSD_XBC
DT_LANE0 = QK_ROPE

LOG2E = math.log2(math.e)
Q_PRESCALE = MLA_SCALE * LOG2E

SHIFT_HEADROOM = 60.0
SHIFT_MAX_BOUND = 90.0
BOUND_SLACK = 1.0 + 2.0 ** -8


def _rms(x, g):
    return x * lax.rsqrt(jnp.mean(x * x, axis=-1, keepdims=True) + EPS) * g


def _silu(x):
    return x / (1.0 + jnp.exp(-x))


def _dot(a, b):
    return jnp.dot(a, b, preferred_element_type=F32)


class _TokenLayout:
    def __init__(self, bsz, n_ctx, seq):
        tm = TOKEN_TILE
        assert n_ctx % tm == 0 and seq % tm == 0
        self.bsz = bsz
        self.ctx_tiles = n_ctx // tm
        self.lat_tiles = seq // tm
        self.tiles_per_batch = self.ctx_tiles + self.lat_tiles
        self.n_tiles = bsz * self.tiles_per_batch
        self.ctx_row = bsz

    def unified_tile(self, o, latent_only):
        if not latent_only:
            return o
        return o + (o // self.lat_tiles + 1) * self.ctx_tiles

    def coords(self, u):
        b = u // self.tiles_per_batch
        return b, u - b * self.tiles_per_batch

    def mod_row(self, u):
        b, i = self.coords(u)
        return jnp.where(i < self.ctx_tiles, self.ctx_row, b)

    def seq_edges(self, u):
        _, i = self.coords(u)
        first = (i == 0) | (i == self.ctx_tiles)
        last = (i == self.ctx_tiles - 1) | (i == self.tiles_per_batch - 1)
        return first, last

    def _latent_tile(self, u):
        b, i = self.coords(u)
        return b * self.lat_tiles + jnp.maximum(i - self.ctx_tiles, 0)

    def _context_tile(self, u):
        b, i = self.coords(u)
        return b * self.ctx_tiles + jnp.minimum(i, self.ctx_tiles - 1)

    def source_specs(self, u_of, d, split):
        tm = TOKEN_TILE
        if not split:
            return [pl.BlockSpec((tm, d), lambda j: (u_of(j), 0))]
        return [pl.BlockSpec((tm, d), lambda j: (self._context_tile(u_of(j)), 0)),
                pl.BlockSpec((tm, d), lambda j: (self._latent_tile(u_of(j)), 0))]

    def halo_specs(self, u_of, d, split):
        rb = TOKEN_TILE // SUBLANES
        if split:
            assert self.ctx_tiles == 1
            tile_of, n_rb = (lambda j: self._latent_tile(u_of(j))), self.bsz * self.lat_tiles * rb
        else:
            tile_of, n_rb = u_of, self.n_tiles * rb
        return [pl.BlockSpec((SUBLANES, d), lambda j: (jnp.maximum(tile_of(j) * rb - 1, 0), 0)),
                pl.BlockSpec((SUBLANES, d), lambda j: (jnp.minimum((tile_of(j) + 1) * rb, n_rb - 1), 0))]

    def select_source(self, u, refs):
        if len(refs) == 1:
            return refs[0][...]
        _, i = self.coords(u)
        return jnp.where(i < self.ctx_tiles, refs[0][...], refs[1][...])


def _mod_kernel(c_ref, w_ref, b_ref, o_ref):
    s = _silu(c_ref[...]).astype(BF16)
    o_ref[0] = _dot(s, w_ref[0].astype(BF16)) + b_ref[0]


def _modulation(cvec, w_mod, b_mod):
    depth, d, nd = w_mod.shape
    rows = cvec.shape[0]
    return pl.pallas_call(
        _mod_kernel,
        out_shape=jax.ShapeDtypeStruct((depth, rows, nd), F32),
        grid=(depth, nd // d),
        in_specs=[
            pl.BlockSpec((rows, d), lambda l, j: (0, 0)),
            pl.BlockSpec((1, d, d), lambda l, j: (l, 0, j)),
            pl.BlockSpec((1, 1, d), lambda l, j: (l, 0, j)),
        ],
        out_specs=pl.BlockSpec((1, rows, d), lambda l, j: (l, 0, j)),
        compiler_params=pltpu.CompilerParams(dimension_semantics=("parallel", "parallel")),
        name="modulation",
    )(cvec, w_mod, b_mod.reshape(depth, 1, nd))


def _conv3_rows(a_ext, w_ref, tm):
    te = a_ext.shape[0]
    lo, hi = SUBLANES, SUBLANES + tm
    prev = pltpu.roll(a_ext, 1, 0)[lo:hi]
    nxt = pltpu.roll(a_ext, te - 1, 0)[lo:hi]
    return w_ref[0:1, :] * prev + w_ref[1:2, :] * a_ext[lo:hi] + w_ref[2:3, :] * nxt


N_INPROJ_PARAMS = 11
N_INPROJ_TOKEN_INPUTS = 5


def _split3(x):
    hi = x.astype(BF16)
    r = x - hi.astype(F32)
    mid = r.astype(BF16)
    lo = (r - mid.astype(F32)).astype(BF16)
    return hi, mid, lo


SIDE_ROWS = 2 * 2 * SSD_HEADS


def _ssd_decay_tables(dt_raw, alog_ref, dtbias_ref, cs_ref, side_ref, slot):
    tc = SSD_CHUNK
    tm = dt_raw.shape[0]
    nd = 2 * SSD_HEADS
    a = -jnp.exp(alog_ref[...])
    xb = dt_raw + dtbias_ref[...]
    dt = jnp.maximum(xb, 0.0) + jnp.log(1.0 + jnp.exp(-jnp.abs(xb)))
    la = dt * a * LOG2E
    r_i = lax.broadcasted_iota(jnp.int32, (tc, tc), 0)
    c_i = lax.broadcasted_iota(jnp.int32, (tc, tc), 1)
    tri_f = jnp.where(c_i <= r_i, 1.0, 0.0).astype(BF16)
    tri_b = jnp.where(c_i >= r_i, 1.0, 0.0).astype(BF16)
    lane = lax.broadcasted_iota(jnp.int32, (1, LANES), 1)
    fwd_lane = lane < DT_LANE0 + SSD_HEADS
    row = lax.broadcasted_iota(jnp.int32, (nd, 1), 0)
    for c in range(tm // tc):
        parts = _split3(la[c * tc:(c + 1) * tc])
        cs_f = sum(_dot(tri_f, p) for p in parts)
        cs_b = sum(_dot(tri_b, p) for p in parts)
        cs = jnp.where(fwd_lane, cs_f, cs_b)
        cs_ref[slot * tm + c * tc:slot * tm + (c + 1) * tc, :] = cs
        cs_t = cs.T[DT_LANE0:DT_LANE0 + nd]
        dt_t = dt[c * tc:(c + 1) * tc].T[DT_LANE0:DT_LANE0 + nd]
        cs_end = jnp.where(row < SSD_HEADS, cs_t[:, tc - 1:tc], cs_t[:, 0:1])
        w_t = jnp.exp2(cs_end - cs_t) * dt_t
        side_ref[slot * (tm // tc) + c] = jnp.concatenate([cs_t - jnp.log(dt_t) * LOG2E, w_t], axis=0)


def _rotate_cols(w):
    a, b, c, d = jnp.split(w, 4, axis=-1)
    return jnp.concatenate([-b, a, -d, c], axis=-1)


def _relayout_weights(win_raw, wuq_raw, wukv_raw, win_ref, wuq_ref, wukv_ref):
    d = win_raw.shape[1]
    w_kr = win_raw[Q_LORA + KV_LORA:IN_MLA, :]
    w_dt = win_raw[IN_MLA + IN_SC + SSD_WIDTH + SSD_XBC:, :]
    a, b, c, e = (w_kr[i * SUBLANES:(i + 1) * SUBLANES] for i in range(4))
    kr_block = jnp.concatenate([
        -b, a, -e, c,
        w_dt, jnp.zeros((LANES // 2 - QK_ROPE - w_dt.shape[0], d), F32),
        w_kr, jnp.zeros((LANES // 2 - QK_ROPE, d), F32)], axis=0)
    win_ref[:, C_KR0:C_SC0] = kr_block.T.astype(BF16)
    step = 2 * LANES
    for src0, dst0, n_rows in ((0, C_Q0, Q_LORA + KV_LORA), (IN_MLA, C_SC0, IN_SC + SSD_WIDTH + SSD_XBC)):
        for off in range(0, n_rows, step):
            rows = min(step, n_rows - off)
            blk = win_raw[src0 + off:src0 + off + rows, :]
            win_ref[:, dst0 + off:dst0 + off + rows] = blk.T.astype(BF16)

    dqk = QK_NOPE + QK_ROPE
    wq = wuq_raw[...]
    nq = wq.shape[0]
    plain, rot = [], []
    for h in range(MLA_HEADS):
        wh = wq[:, h * dqk:(h + 1) * dqk]
        pad = jnp.zeros((nq, HEAD_PAD - dqk), F32)
        plain += [wh, pad]
        rot += [jnp.zeros((nq, QK_NOPE), F32), _rotate_cols(wh[:, QK_NOPE:]), pad]
    wuq_ref[...] = jnp.concatenate(plain + rot, axis=1).astype(BF16)

    dkv = QK_NOPE + V_HEAD
    wkv = wukv_raw[...]
    nkv = wkv.shape[0]
    ks, vs = [], []
    for h in range(MLA_HEADS):
        wh = wkv[:, h * dkv:(h + 1) * dkv]
        ks += [wh[:, :QK_NOPE], jnp.zeros((nkv, HEAD_PAD - QK_NOPE), F32)]
        vs.append(wh[:, QK_NOPE:])
    wukv_ref[...] = jnp.concatenate(ks + vs, axis=1).astype(BF16)


def _inproj_kernel(*refs, layout, split, tiles):
    n_x = 2 if split else 1
    per_half = n_x + N_INPROJ_TOKEN_INPUTS
    n_in = tiles * per_half
    (gpre, win_raw, qg, wuq_raw, kvg, wukv_raw, scw, cw, cb, alog, dtbias) = [
        r.at[0] for r in refs[n_in:n_in + N_INPROJ_PARAMS]]
    n_out = len(refs) - n_in - N_INPROJ_PARAMS - 3
    outs = refs[n_in + N_INPROJ_PARAMS:n_in + N_INPROJ_PARAMS + n_out]
    win_ref, wuq_ref, wukv_ref = refs[n_in + N_INPROJ_PARAMS + n_out:]

    @pl.when(pl.program_id(0) == 0)
    def _():
        _relayout_weights(win_raw, wuq_raw, wukv_raw, win_ref, wuq_ref, wukv_ref)

    params = (qg, wuq_ref, kvg, wukv_ref, scw, cw, cb, alog, dtbias)
    for h in range(tiles):
        half = refs[h * per_half:(h + 1) * per_half]
        u = pl.program_id(0) * tiles + h
        x = layout.select_source(u, half[:n_x])
        xp_ref, xn_ref, mod_ref, cos_ref, sin_ref = half[n_x:]
        h_ext = _inproj_normalise(x, xp_ref, xn_ref, mod_ref.at[0], gpre, layout.seq_edges(u))
        u_ext = _dot(h_ext, win_ref[...])
        _inproj_finish(u_ext, cos_ref, sin_ref, params, outs, h)


def _inproj_normalise(x, xp_ref, xn_ref, mod_ref, gpre_ref, seq_edges):
    first_of_seq, last_of_seq = seq_edges
    shift = mod_ref[0, 0:1, :]
    scale = mod_ref[0, 1:2, :]

    def norm_mod(xx):
        return _rms(xx, gpre_ref[...]) * (1.0 + scale) + shift

    h_prev = jnp.where(first_of_seq, 0.0, norm_mod(xp_ref[...]))
    h_next = jnp.where(last_of_seq, 0.0, norm_mod(xn_ref[...]))
    return jnp.concatenate([h_prev, norm_mod(x), h_next], axis=0).astype(BF16)


def _inproj_finish(u_ext, cos_ref, sin_ref, params, outs, slot):
    qg_ref, wuq_ref, kvg_ref, wukv_ref, scw_ref, cw_ref, cb_ref, alog_ref, dtbias_ref = params
    qt_ref, k_ref, vt_ref, kn_ref, qn_ref, ysc_ref, z_ref, act_ref, cs_ref, side_ref = outs
    tm = u_ext.shape[0] - 2 * SUBLANES
    rows = slice(slot * tm, (slot + 1) * tm)
    own = slice(SUBLANES, SUBLANES + tm)

    z_ref[rows, :] = u_ext[own, C_Z0:C_XBC0].astype(z_ref.dtype)
    lane = lax.broadcasted_iota(jnp.int32, (1, LANES), 1)
    krb = u_ext[own, C_KR0:C_SC0]
    dt_raw = jnp.where((lane >= DT_LANE0) & (lane < DT_LANE0 + 2 * SSD_HEADS), krb, 0.0)
    _ssd_decay_tables(dt_raw, alog_ref, dtbias_ref, cs_ref, side_ref, slot)

    conv = _conv3_rows(u_ext[:, C_XBC0:D_IN_PAD], cw_ref, tm) + cb_ref[...]
    act_ref[rows, :] = _silu(conv).astype(act_ref.dtype)

    w = SC_WIDTH
    prod = u_ext[:, C_SC0 + w:C_SC0 + 2 * w] * u_ext[:, C_SC0 + 2 * w:C_Z0]
    ysc_ref[rows, :] = (u_ext[own, C_SC0:C_SC0 + w] * _conv3_rows(prod, scw_ref, tm)).astype(ysc_ref.dtype)

    cos = cos_ref[...]
    sin = sin_ref[...]
    nh = MLA_HEADS
    cos_h = jnp.concatenate([cos] * nh, axis=1)
    sin_h = jnp.concatenate([sin] * nh, axis=1)

    def head_norm_maxima(a_f32):
        out = []
        for h in range(nh):
            a_h = a_f32[:, h * HEAD_PAD:(h + 1) * HEAD_PAD]
            n2 = jnp.max(jnp.sum(a_h * a_h, axis=1, keepdims=True), axis=0, keepdims=True)
            out.append(jnp.broadcast_to(n2, (1, LANES)))
        return jnp.concatenate(out + [jnp.zeros((SUBLANES - nh, LANES), F32)], axis=0)

    cq = _rms(u_ext[own, C_Q0:C_KV0], qg_ref[...]).astype(BF16)
    q2 = _dot(cq, wuq_ref[...])
    qw = nh * HEAD_PAD
    q_b = ((q2[:, :qw] * cos_h + q2[:, qw:] * sin_h) * Q_PRESCALE).astype(BF16)
    q_f = q_b.astype(F32)
    qt_ref[:, rows] = q_f.T.astype(BF16)
    qn_ref[slot] = head_norm_maxima(q_f)

    ckv = _rms(u_ext[own, C_KV0:C_KR0], kvg_ref[...]).astype(BF16)
    kv = _dot(ckv, wukv_ref[...])
    rope_lane = (lane >= ROPE_LANE0) & (lane < ROPE_LANE0 + QK_ROPE)
    cos_k = jnp.where(rope_lane, cos, 0.0)
    kr = krb * cos_k + pltpu.roll(krb, LANES // 2, 1) * sin
    k = kv[:, :qw] + jnp.concatenate([kr] * nh, axis=1)
    k_b = k.astype(BF16)
    k_ref[rows, :] = k_b
    kn_ref[slot] = head_norm_maxima(k_b.astype(F32))
    v_t = kv[:, qw:].T.astype(BF16)
    for h in range(nh):
        vt_ref[h, 0:V_HEAD, rows] = v_t[h * V_HEAD:(h + 1) * V_HEAD]
        vt_ref[h, V_HEAD:V_ROWS, rows] = jnp.ones((V_ROWS - V_HEAD, tm), BF16)


def _layer_spec(a, layer, single_buffer=False):
    block = (1,) + a.shape[1:]
    index_map = lambda j: (layer,) + (0,) * (a.ndim - 1)
    if single_buffer:
        return pl.BlockSpec(block, index_map, pipeline_mode=pl.Buffered(1))
    return pl.BlockSpec(block, index_map)


def _inproj(src, mod, layer, gpre, win, qg, wuq, kvg, wukv, cos_t, sin_t, scw, cw, cb, alog, dtbias, layout):
    tm = TOKEN_TILE
    d = src[0].shape[-1]
    split = len(src) == 2
    qw = MLA_HEADS * HEAD_PAD
    nt = layout.n_tiles
    tiles = INPROJ_TILES_PER_STEP
    assert nt % tiles == 0
    rows = nt * tm

    in_specs, operands = [], []
    for h in range(tiles):
        def u_of(j, h=h):
            return j * tiles + h

        def rope(j, u_of=u_of):
            return (layout.coords(u_of(j))[1], 0)

        in_specs += layout.source_specs(u_of, d, split) + layout.halo_specs(u_of, d, split)
        in_specs += [
            pl.BlockSpec((1, 1, N_MOD, d), lambda j, u_of=u_of: (layer, layout.mod_row(u_of(j)), 0, 0)),
            pl.BlockSpec((tm, LANES), rope), pl.BlockSpec((tm, LANES), rope),
        ]
        operands += list(src) + [src[-1], src[-1], mod, cos_t, sin_t]
    consts = [gpre, win, qg, wuq, kvg, wukv, scw, cw, cb, alog, dtbias]
    in_specs += [_layer_spec(c, layer, single_buffer=c is win) for c in consts]
    step_rows = tiles * tm
    chunks_per_step = step_rows // SSD_CHUNK

    def tok(width):
        return pl.BlockSpec((step_rows, width), lambda j: (j, 0))

    norm_shape = jax.ShapeDtypeStruct((nt, SUBLANES, LANES), F32)
    norm_spec = pl.BlockSpec((tiles, SUBLANES, LANES), lambda j: (j, 0, 0))
    out_shape = (
        jax.ShapeDtypeStruct((qw, rows), BF16),
        jax.ShapeDtypeStruct((rows, qw), BF16),
        jax.ShapeDtypeStruct((MLA_HEADS, V_ROWS, rows), BF16),
        norm_shape,
        norm_shape,
        jax.ShapeDtypeStruct((rows, SC_WIDTH), BF16),
        jax.ShapeDtypeStruct((rows, SSD_WIDTH), BF16),
        jax.ShapeDtypeStruct((rows, SSD_XBC), BF16),
        jax.ShapeDtypeStruct((rows, LANES), F32),
        jax.ShapeDtypeStruct((rows // SSD_CHUNK, SIDE_ROWS, SSD_CHUNK), F32),
    )
    return pl.pallas_call(
        functools.partial(_inproj_kernel, layout=layout, split=split, tiles=tiles),
        out_shape=out_shape,
        grid=(nt // tiles,),
        in_specs=in_specs,
        out_specs=(pl.BlockSpec((qw, step_rows), lambda j: (0, j)), tok(qw),
                   pl.BlockSpec((MLA_HEADS, V_ROWS, step_rows), lambda j: (0, 0, j)),
                   norm_spec, norm_spec,
                   tok(SC_WIDTH), tok(SSD_WIDTH), tok(SSD_XBC), tok(LANES),
                   pl.BlockSpec((chunks_per_step, SIDE_ROWS, SSD_CHUNK), lambda j: (j, 0, 0))),
        scratch_shapes=[pltpu.VMEM((d, D_IN_PAD), BF16),
                        pltpu.VMEM((Q_LORA, 2 * qw), BF16),
                        pltpu.VMEM((KV_LORA, qw + MLA_WIDTH), BF16)],
        compiler_params=pltpu.CompilerParams(
            dimension_semantics=("arbitrary",), vmem_limit_bytes=VMEM_LIMIT_BYTES),
        name="inproj",
    )(*operands, *consts)


def _attn_kernel(*refs, n_keys, n_ctx, ctx_queries, q_tiles):
    qt_refs = refs[:q_tiles]
    k_ref, vt_ref, kn_ref = refs[q_tiles:q_tiles + 3]
    qn_refs = refs[q_tiles + 3:2 * q_tiles + 3]
    o_ref, s_ref, p_ref = refs[2 * q_tiles + 3:]
    i = pl.program_id(1)
    kn = jnp.max(kn_ref[0], axis=0)
    for t in range(q_tiles):
        _attend_tile(qt_refs[t], qn_refs[t], k_ref, vt_ref, kn, o_ref, t, s_ref, p_ref, i,
                     n_keys=n_keys, n_ctx=n_ctx, ctx_queries=ctx_queries)


def _attend_tile(q_t, qn_ref, k_ref, vt_ref, kn, o_ref, slot, s_ref, p_ref, i, *, n_keys, n_ctx, ctx_queries):
    tq = q_t.shape[1]

    def finish(outs):
        o_ref[0, slot * tq:(slot + 1) * tq, :] = jnp.concatenate(outs, axis=0).T.astype(o_ref.dtype)

    def head_out(ov):
        return ov[0:V_HEAD] / ov[V_HEAD:V_HEAD + 1]

    def attend_two_pass(nk):
        def scores(h):
            s_ref[h, 0:nk, :] = _dot(k_ref[0, 0:nk, h * HEAD_PAD:(h + 1) * HEAD_PAD],
                                     q_t[h * HEAD_PAD:(h + 1) * HEAD_PAD, :])

        def probs(h):
            s = s_ref[h, 0:nk, :]
            p_ref[h, 0:nk, :] = jnp.exp2(s - jnp.max(s, axis=0, keepdims=True)).astype(BF16)

        def values(h):
            return head_out(_dot(vt_ref[h, :, 0:nk], p_ref[h, 0:nk, :]))

        scores(0), scores(1)
        scores(2), scores(3), probs(0), probs(1)
        outs = [values(0), values(1)]
        probs(2), probs(3)
        outs += [values(2), values(3)]
        finish(outs)

    def attend_one_pass(nk, shift):
        outs = []
        for pair in range(MLA_HEADS // 2):
            heads = (2 * pair, 2 * pair + 1)
            s = [_dot(k_ref[0, 0:nk, h * HEAD_PAD:(h + 1) * HEAD_PAD],
                      q_t[h * HEAD_PAD:(h + 1) * HEAD_PAD, :]) for h in heads]
            for h, sh in zip(heads, s):
                p_ref[h, 0:nk, :] = jnp.exp2(sh - shift[h]).astype(BF16)
            outs += [head_out(_dot(vt_ref[h, :, 0:nk], p_ref[h, 0:nk, :])) for h in heads]
        finish(outs)

    def attend(nk):
        bound = jnp.sqrt(qn_ref[0] * kn) * BOUND_SLACK
        one_pass = jnp.max(bound) <= SHIFT_MAX_BOUND

        @pl.when(one_pass)
        def _():
            attend_one_pass(nk, [bound[h:h + 1, 0:1] - SHIFT_HEADROOM for h in range(MLA_HEADS)])

        @pl.when(jnp.logical_not(one_pass))
        def _():
            attend_two_pass(nk)

    if ctx_queries:
        @pl.when(i == 0)
        def _():
            attend_two_pass(n_ctx)

        @pl.when(i > 0)
        def _():
            attend(n_keys)
    else:
        attend(n_keys)


def _attention(qt, k, vt, kn, qn, n_ctx, ctx_queries):
    bsz, t, qw = k.shape
    tq = TOKEN_TILE
    tiles = t // tq
    t0 = 0 if ctx_queries else n_ctx // tq
    nq = tiles - t0
    q_tiles = 1 if ctx_queries else 2
    assert nq % q_tiles == 0

    def tile_of(b, i, s):
        return b * tiles + i * q_tiles + s + t0

    return pl.pallas_call(
        functools.partial(_attn_kernel, n_keys=t, n_ctx=n_ctx, ctx_queries=ctx_queries, q_tiles=q_tiles),
        out_shape=jax.ShapeDtypeStruct((bsz, nq * tq, MLA_WIDTH), BF16),
        grid=(bsz, nq // q_tiles),
        in_specs=[pl.BlockSpec((qw, tq), lambda b, i, s=s: (0, tile_of(b, i, s))) for s in range(q_tiles)] + [
            pl.BlockSpec((1, t, qw), lambda b, i: (b, 0, 0)),
            pl.BlockSpec((MLA_HEADS, V_ROWS, t), lambda b, i: (0, 0, b)),
            pl.BlockSpec((1,) + kn.shape[1:], lambda b, i: (b, 0, 0, 0)),
        ] + [pl.BlockSpec((1, SUBLANES, LANES), lambda b, i, s=s: (tile_of(b, i, s), 0, 0)) for s in range(q_tiles)],
        out_specs=pl.BlockSpec((1, q_tiles * tq, MLA_WIDTH), lambda b, i: (b, i, 0)),
        scratch_shapes=[pltpu.VMEM((MLA_HEADS, t, tq), F32), pltpu.VMEM((MLA_HEADS, t, tq), BF16)],
        compiler_params=pltpu.CompilerParams(
            dimension_semantics=("parallel", "parallel"), vmem_limit_bytes=VMEM_LIMIT_BYTES),
        name="attention",
    )(*([qt] * q_tiles), k, vt, kn, *([qn] * q_tiles))


PAIRS_PER_CHUNK = SSD_HEADS // 2


def _ssd_build_lhs(act, cs, side, direction, lhs_ref, base):
    tc = SSD_CHUNK
    n = SSD_STATE
    bm = act[:, SSD_WIDTH:SSD_WIDTH + SSD_GN].astype(F32)
    cm = act[:, SSD_WIDTH + SSD_GN:]
    bm_t = bm.T

    r_i = lax.broadcasted_iota(jnp.int32, (tc, tc), 0)
    c_i = lax.broadcasted_iota(jnp.int32, (tc, tc), 1)
    tri = (c_i <= r_i) if direction == 0 else (c_i >= r_i)
    nd = 2 * SSD_HEADS
    csd_t, w_t = side[0:nd], side[nd:2 * nd]
    cm_f = cm.astype(F32)
    zeros_lhs = jnp.zeros((n, tc), BF16)
    m = tc + n
    for g in range(SSD_GROUPS):
        cb = _dot(cm[:, g * n:(g + 1) * n], bm_t[g * n:(g + 1) * n, :].astype(BF16))
        bt_g = bm_t[g * n:(g + 1) * n, :]
        for pr in range(HEADS_PER_GROUP // 2):
            slab = g * (HEADS_PER_GROUP // 2) + pr
            for hh in range(2):
                r = direction * SSD_HEADS + 2 * slab + hh
                col = DT_LANE0 + r
                cs_col = jnp.broadcast_to(cs[:, col:col + 1], (tc, tc))
                lmd = jnp.exp2(jnp.where(tri, cs_col - csd_t[r:r + 1, :], -jnp.inf))
                lhs_ref[base + slab, hh * m:hh * m + tc, 0:tc] = (cb * lmd).astype(BF16)
                lhs_ref[base + slab, hh * m:hh * m + tc, tc:2 * tc] = (cm_f * jnp.exp2(cs_col)).astype(BF16)
                lhs_ref[base + slab, hh * m + tc:(hh + 1) * m, 0:tc] = (bt_g * w_t[r:r + 1, :]).astype(BF16)
                lhs_ref[base + slab, hh * m + tc:(hh + 1) * m, tc:2 * tc] = zeros_lhs


def _ssd_apply(act, cs, direction, dskip_ref, y_ref, row0, h_ref, lhs_ref, base):
    tc = SSD_CHUNK
    n = SSD_STATE
    m = tc + n
    end = tc - 1 if direction == 0 else 0
    dec_tot = jnp.exp2(cs[end:end + 1, :])
    lane = lax.broadcasted_iota(jnp.int32, (1, LANES), 1)
    low_half = lane < SSD_HEAD_DIM
    zeros_h = jnp.zeros((n, LANES), BF16)
    for g in range(SSD_GROUPS):
        for pr in range(HEADS_PER_GROUP // 2):
            slab = g * (HEADS_PER_GROUP // 2) + pr
            h_pair = h_ref[direction, g, :, pr * LANES:(pr + 1) * LANES]
            h_b = h_pair.astype(BF16)
            xs_pair = act[:, slab * LANES:(slab + 1) * LANES]
            rhs = jnp.concatenate([xs_pair] + [h_b if gg == g else zeros_h for gg in range(SSD_GROUPS)], axis=0)
            out = _dot(lhs_ref[base + slab], rhs)
            y_pair = jnp.where(low_half, out[0:tc], out[m:m + tc])
            if direction == 0:
                y_pair = y_pair + dskip_ref[:, slab * LANES:(slab + 1) * LANES] * xs_pair.astype(F32)
            y_ref[row0:row0 + tc, slab * LANES:(slab + 1) * LANES] = y_pair
            h_new = jnp.where(low_half, out[tc:m], out[m + tc:2 * m])
            cols = [DT_LANE0 + direction * SSD_HEADS + 2 * slab + hh for hh in range(2)]
            keep = jnp.where(low_half, dec_tot[:, cols[0]:cols[0] + 1], dec_tot[:, cols[1]:cols[1] + 1])
            h_ref[direction, g, :, pr * LANES:(pr + 1) * LANES] = h_pair * keep + h_new


def _bwd_block(i, n_blocks, ctx_blocks):
    return jnp.where(i < ctx_blocks, ctx_blocks - 1 - i, n_blocks - 1 + ctx_blocks - i)


def _ssd_kernel(af_ref, ab_ref, csf_ref, csb_ref, sidef_ref, sideb_ref, dskip_ref, yf_ref, yb_ref,
                h_ref, lhs_ref):
    @pl.when(pl.program_id(1) == 0)
    def _():
        h_ref[...] = jnp.zeros_like(h_ref)

    dskip = dskip_ref.at[0]
    tc = SSD_CHUNK
    n_sub = af_ref.shape[1] // tc
    roles = []
    for bb in range(af_ref.shape[0]):
        for j in range(n_sub):
            jb = n_sub - 1 - j
            roles.append((bb, af_ref, csf_ref, sidef_ref, j, 0, yf_ref))
            roles.append((bb, ab_ref, csb_ref, sideb_ref, jb, 1, yb_ref))
    for idx, (bb, a_ref, c_ref, s_ref, j, direction, _) in enumerate(roles):
        r0 = j * tc
        _ssd_build_lhs(a_ref[bb, r0:r0 + tc, :], c_ref[bb, r0:r0 + tc, :], s_ref[bb, j], direction,
                       lhs_ref, idx * PAIRS_PER_CHUNK)
    for idx, (bb, a_ref, c_ref, _, j, direction, y_ref) in enumerate(roles):
        r0 = j * tc
        _ssd_apply(a_ref[bb, r0:r0 + tc, :], c_ref[bb, r0:r0 + tc, :], direction, dskip, y_ref.at[bb], r0,
                   h_ref.at[bb], lhs_ref, idx * PAIRS_PER_CHUNK)


def _ssd_scan(act, cs, side, layer, dskip, n_ctx):
    bsz, t, _ = act.shape
    tc = SSD_BLOCK
    n_blocks = t // tc
    ctx_blocks = n_ctx // tc
    assert n_ctx % tc == 0 and t % tc == 0

    def fwd(i):
        return i

    def bwd(i):
        return _bwd_block(i, n_blocks, ctx_blocks)

    bps = SSD_BATCH_PER_STEP
    assert bsz % bps == 0

    def main(order, width):
        return pl.BlockSpec((bps, tc, width), lambda b, i: (b, order(i), 0))

    def side_spec(order):
        return pl.BlockSpec((bps, tc // SSD_CHUNK, SIDE_ROWS, SSD_CHUNK), lambda b, i: (b, order(i), 0, 0))

    def const(shape):
        return pl.BlockSpec((1,) + shape[1:], lambda b, i: (layer,) + (0,) * (len(shape) - 1))

    y_shape = jax.ShapeDtypeStruct((bsz, t, SSD_WIDTH), F32)
    return pl.pallas_call(
        _ssd_kernel,
        out_shape=(y_shape, y_shape),
        grid=(bsz // bps, n_blocks),
        in_specs=[
            main(fwd, SSD_XBC), main(bwd, SSD_XBC), main(fwd, LANES), main(bwd, LANES),
            side_spec(fwd), side_spec(bwd),
            const(dskip.shape),
        ],
        out_specs=(main(fwd, SSD_WIDTH), main(bwd, SSD_WIDTH)),
        scratch_shapes=[pltpu.VMEM((bps, 2, SSD_GROUPS, SSD_STATE, HEADS_PER_GROUP * SSD_HEAD_DIM), F32),
                        pltpu.VMEM((bps * 2 * (tc // SSD_CHUNK) * PAIRS_PER_CHUNK, 2 * (SSD_CHUNK + SSD_STATE),
                                    2 * SSD_CHUNK), BF16)],
        compiler_params=pltpu.CompilerParams(
            dimension_semantics=("arbitrary", "arbitrary"), vmem_limit_bytes=VMEM_LIMIT_BYTES),
        name="ssd_scan",
    )(act, act, cs, cs, side, side, dskip)


def _mix_gather(att_ref, ysc_ref, z_ref, yf_ref, yb_ref, normg_ref, ycat_ref):
    gated = (yf_ref[...] + yb_ref[...]) * _silu(z_ref[...].astype(F32))
    gw = SSD_WIDTH // SSD_GROUPS
    w = SC_WIDTH
    ycat_ref[:, 0:MLA_WIDTH] = att_ref[...]
    ycat_ref[:, MLA_WIDTH:MLA_WIDTH + w] = ysc_ref[...]
    c0 = MLA_WIDTH + w
    for g in range(SSD_GROUPS):
        gg = gated[:, g * gw:(g + 1) * gw]
        gg = gg * lax.rsqrt(jnp.mean(gg * gg, axis=-1, keepdims=True) + EPS) * normg_ref[:, g * gw:(g + 1) * gw]
        ycat_ref[:, c0 + g * gw:c0 + (g + 1) * gw] = gg.astype(BF16)


def _mix_residual(x, mod_ref, y_ref, gpost_ref, gpre2_ref, x1_ref, h2_ref):
    gate1 = mod_ref[0, 2:3, :]
    shift2 = mod_ref[0, 3:4, :]
    scale2 = mod_ref[0, 4:5, :]
    x1 = x + gate1 * _rms(y_ref[...], gpost_ref[...])
    x1_ref[...] = x1
    h2_ref[...] = (_rms(x1, gpre2_ref[...]) * (1.0 + scale2) + shift2).astype(BF16)


def _mix_mlp(h2_ref, w1_ref, w2_ref, acc_ref):
    d_ff = w1_ref.shape[1]
    for c in range(d_ff // FF_CHUNK):
        a = _dot(h2_ref[...], w1_ref[:, c * FF_CHUNK:(c + 1) * FF_CHUNK])
        r = jnp.square(jnp.maximum(a, 0.0)).astype(BF16)
        part = _dot(r, w2_ref[c * FF_CHUNK:(c + 1) * FF_CHUNK, :])
        if c == 0:
            acc_ref[...] = part
        else:
            acc_ref[...] += part


N_MIX_PARAMS = 7
N_MIX_TOKEN_INPUTS = 6


def _mix_ffn_kernel(*refs, layout, latent_only, split):
    n_x = 2 if split else 1
    per_half = n_x + N_MIX_TOKEN_INPUTS
    n_in = TILES_PER_STEP * per_half
    normg, wout, gpost, gpre2, w1, w2, gpost2 = [r.at[0] for r in refs[n_in:n_in + N_MIX_PARAMS]]
    o_ref = refs[n_in + N_MIX_PARAMS]
    scratch = refs[n_in + N_MIX_PARAMS + 1:]
    per_tile = len(scratch) // TILES_PER_STEP
    tiles = [scratch[h * per_tile:(h + 1) * per_tile] for h in range(TILES_PER_STEP)]
    tm = TOKEN_TILE
    halves = [refs[h * per_half:(h + 1) * per_half] for h in range(TILES_PER_STEP)]
    mods = [half[n_x].at[0] for half in halves]
    for half, (ycat_ref, y_ref, _, _, _) in zip(halves, tiles):
        _mix_gather(*half[n_x + 1:], normg, ycat_ref)
        y_ref[...] = _dot(ycat_ref[...], wout[...])
    for h, (half, (_, y_ref, acc_ref, x1_ref, h2_ref)) in enumerate(zip(halves, tiles)):
        u = layout.unified_tile(pl.program_id(0) * TILES_PER_STEP + h, latent_only)
        x = layout.select_source(u, half[:n_x])
        _mix_residual(x, mods[h], y_ref, gpost, gpre2, x1_ref, h2_ref)
        _mix_mlp(h2_ref, w1, w2, acc_ref)
    for h, (_, _, acc_ref, x1_ref, _) in enumerate(tiles):
        gate2 = mods[h][0, 5:6, :]
        o_ref[h * tm:(h + 1) * tm, :] = x1_ref[...] + gate2 * _rms(acc_ref[...], gpost2[...])


def _mix_ffn(src, mod, layer, att, ysc, z, yf, yb, normg, wout, gpost, gpre2, w1, w2, gpost2, layout,
             latent_only):
    tm = TOKEN_TILE
    d = src[0].shape[-1]
    n_out_tiles = layout.bsz * (layout.lat_tiles if latent_only else layout.tiles_per_batch)
    assert n_out_tiles % TILES_PER_STEP == 0
    att_unified = att.shape[0] == layout.bsz * layout.tiles_per_batch * tm

    in_specs, operands = [], []
    for h in range(TILES_PER_STEP):
        def u_of(j, h=h):
            return layout.unified_tile(j * TILES_PER_STEP + h, latent_only)

        def tok(width, tile_of=u_of):
            return pl.BlockSpec((tm, width), lambda j: (tile_of(j), 0))

        in_specs += layout.source_specs(u_of, d, len(src) == 2)
        in_specs += [
            pl.BlockSpec((1, 1, N_MOD, d), lambda j, u_of=u_of: (layer, layout.mod_row(u_of(j)), 0, 0)),
            tok(MLA_WIDTH) if att_unified else tok(MLA_WIDTH, lambda j, h=h: j * TILES_PER_STEP + h),
            tok(SC_WIDTH), tok(SSD_WIDTH), tok(SSD_WIDTH), tok(SSD_WIDTH),
        ]
        operands += list(src) + [mod, att, ysc, z, yf, yb]
    consts = [normg, wout, gpost, gpre2, w1, w2, gpost2]
    in_specs += [_layer_spec(c, layer, single_buffer=True) for c in consts]
    return pl.pallas_call(
        functools.partial(_mix_ffn_kernel, layout=layout, latent_only=latent_only, split=len(src) == 2),
        out_shape=jax.ShapeDtypeStruct((n_out_tiles * tm, d), F32),
        grid=(n_out_tiles // TILES_PER_STEP,),
        in_specs=in_specs,
        out_specs=pl.BlockSpec((TILES_PER_STEP * tm, d), lambda j: (j, 0)),
        scratch_shapes=[pltpu.VMEM((tm, d), BF16), pltpu.VMEM((tm, d), F32), pltpu.VMEM((tm, d), F32),
                        pltpu.VMEM((tm, d), F32), pltpu.VMEM((tm, d), BF16)] * TILES_PER_STEP,
        compiler_params=pltpu.CompilerParams(
            dimension_semantics=("parallel",), vmem_limit_bytes=VMEM_LIMIT_BYTES),
        name="mix_ffn",
    )(*operands, *consts)


def _rope_tables(n_ctx, seq):
    f32 = np.float32
    half = QK_ROPE // 2
    inv_freq = (f32(ROPE_THETA) ** (-np.arange(0, half, 2, dtype=f32) / f32(half))).astype(f32)
    rows = seq // GRID_W
    row = np.repeat(np.arange(rows, dtype=f32), GRID_W)
    col = np.tile(np.arange(GRID_W, dtype=f32), rows)
    ang_r = row[:, None] * inv_freq
    ang_c = col[:, None] * inv_freq
    ang = np.concatenate([ang_r, ang_r, ang_c, ang_c], axis=-1).astype(f32)
    t = n_ctx + seq
    cos_t = np.ones((t, LANES), f32)
    sin_t = np.zeros((t, LANES), f32)
    cos_t[n_ctx:, ROPE_LANE0:ROPE_LANE0 + QK_ROPE] = np.cos(ang)
    sin_t[n_ctx:, ROPE_LANE0:ROPE_LANE0 + QK_ROPE] = np.sin(ang)
    return jnp.asarray(cos_t), jnp.asarray(sin_t)


def kernel(x, c, ctx, c_ctx, w_mod, b_mod, g_pre_mix, w_in, mla_q_norm, w_uq, mla_kv_norm, w_ukv, sc_conv_w, ssd_conv_w, ssd_conv_b, ssd_a_log, ssd_dt_bias, ssd_d, ssd_norm, w_out, g_post_mix, g_pre_ffn, w_ff1, w_ff2, g_post_ffn):
    bsz, seq, d = x.shape
    n_ctx = ctx.shape[1]
    depth = w_mod.shape[0]
    assert n_ctx == TOKEN_TILE and seq % TOKEN_TILE == 0 and seq % GRID_W == 0
    layout = _TokenLayout(bsz, n_ctx, seq)
    t = n_ctx + seq

    cvec = jnp.zeros((SUBLANES, d), F32).at[:bsz].set(c).at[layout.ctx_row].set(c_ctx)
    mod_all = _modulation(cvec, w_mod, b_mod).reshape(depth, SUBLANES, N_MOD, d)
    cos_t, sin_t = _rope_tables(n_ctx, seq)

    src = (ctx.reshape(bsz * n_ctx, d), x.reshape(bsz * seq, d))

    def rows3(a):
        return a.reshape(depth, 1, -1)

    def dt_lanes(a):
        flat = a.reshape(depth, 1, -1)
        return jnp.pad(flat, ((0, 0), (0, 0), (DT_LANE0, LANES - DT_LANE0 - flat.shape[-1])))

    alog, dtbias = dt_lanes(ssd_a_log), dt_lanes(ssd_dt_bias)
    dskip = rows3(jnp.repeat(ssd_d, SSD_HEAD_DIM, axis=-1))
    w_out_b, w_ff1_b, w_ff2_b = w_out.astype(BF16), w_ff1.astype(BF16), w_ff2.astype(BF16)
    w_in_t = jnp.swapaxes(w_in, 1, 2)

    def per_batch(a):
        return a.reshape((bsz, t) + a.shape[1:])

    for i in range(depth):
        last = i == depth - 1
        qt, k, vt, kn, qn, ysc, z, act, cs, side = _inproj(
            src, mod_all, i, rows3(g_pre_mix), w_in_t, rows3(mla_q_norm), w_uq, rows3(mla_kv_norm), w_ukv,
            cos_t, sin_t, sc_conv_w, ssd_conv_w, rows3(ssd_conv_b), alog, dtbias, layout)
        att = _attention(qt, per_batch(k), vt, kn.reshape(bsz, layout.tiles_per_batch, SUBLANES, LANES), qn,
                         n_ctx, ctx_queries=not last)
        yf, yb = _ssd_scan(per_batch(act), per_batch(cs), side.reshape(bsz, t // SSD_CHUNK, SIDE_ROWS, SSD_CHUNK),
                           i, dskip, n_ctx)
        out = _mix_ffn(src, mod_all, i, att.reshape(-1, MLA_WIDTH), ysc, z,
                       yf.reshape(-1, SSD_WIDTH), yb.reshape(-1, SSD_WIDTH), rows3(ssd_norm),
                       w_out_b, rows3(g_post_mix), rows3(g_pre_ffn), w_ff1_b, w_ff2_b, rows3(g_post_ffn),
                       layout, latent_only=last)
        src = (out,)
    return out.reshape(bsz, seq, d)
```

```python
import functools
import math

import jax
import jax.numpy as jnp
import numpy as np
from jax import lax
from jax.experimental import pallas as pl
from jax.experimental.pallas import tpu as pltpu

F32 = jnp.float32
BF16 = jnp.bfloat16

GRID_W = 64
EPS = 1e-6
N_MOD = 6
MLA_HEADS = 4
Q_LORA = 256
KV_LORA = 128
QK_NOPE = 64
QK_ROPE = 32
V_HEAD = 64
MLA_WIDTH = MLA_HEADS * V_HEAD
MLA_SCALE = (QK_NOPE + QK_ROPE) ** -0.5
ROPE_THETA = 10000.0
SC_WIDTH = 256
SSD_HEADS = 8
SSD_HEAD_DIM = 64
SSD_WIDTH = SSD_HEADS * SSD_HEAD_DIM
SSD_GROUPS = 2
SSD_STATE = 64
SSD_CHUNK = 128
SSD_GN = SSD_GROUPS * SSD_STATE
SSD_XBC = SSD_WIDTH + 2 * SSD_GN
HEADS_PER_GROUP = SSD_HEADS // SSD_GROUPS
IN_MLA = Q_LORA + KV_LORA + QK_ROPE
IN_SC = 3 * SC_WIDTH

LANES = 128
SUBLANES = 8
VMEM_LIMIT_BYTES = 56 * 1024 * 1024

HEAD_PAD = LANES
V_ROWS = V_HEAD + 16
ROPE_LANE0 = QK_NOPE
TOKEN_TILE = 256
TILES_PER_STEP = 2
INPROJ_TILES_PER_STEP = 4
SSD_BLOCK = 2 * SSD_CHUNK
SSD_BATCH_PER_STEP = 4
FF_CHUNK = 1024

C_Q0 = 0
C_KV0 = C_Q0 + Q_LORA
C_KR0 = C_KV0 + KV_LORA
C_SC0 = C_KR0 + LANES
C_Z0 = C_SC0 + IN_SC
C_XBC0 = C_Z0 + SSD_WIDTH
D_IN_PAD = C_XBC0 + SSD_XBC
DT_LANE0 = QK_ROPE

LOG2E = math.log2(math.e)
Q_PRESCALE = MLA_SCALE * LOG2E

SHIFT_HEADROOM = 60.0
SHIFT_MAX_BOUND = 90.0
BOUND_SLACK = 1.0 + 2.0 ** -8


def _rms(x, g):
    return x * lax.rsqrt(jnp.mean(x * x, axis=-1, keepdims=True) + EPS) * g


def _silu(x):
    return x / (1.0 + jnp.exp(-x))


def _dot(a, b):
    return jnp.dot(a, b, preferred_element_type=F32)


class _TokenLayout:
    def __init__(self, bsz, n_ctx, seq):
        tm = TOKEN_TILE
        assert n_ctx % tm == 0 and seq % tm == 0
        self.bsz = bsz
        self.ctx_tiles = n_ctx // tm
        self.lat_tiles = seq // tm
        self.tiles_per_batch = self.ctx_tiles + self.lat_tiles
        self.n_tiles = bsz * self.tiles_per_batch
        self.ctx_row = bsz

    def unified_tile(self, o, latent_only):
        if not latent_only:
            return o
        return o + (o // self.lat_tiles + 1) * self.ctx_tiles

    def coords(self, u):
        b = u // self.tiles_per_batch
        return b, u - b * self.tiles_per_batch

    def mod_row(self, u):
        b, i = self.coords(u)
        return jnp.where(i < self.ctx_tiles, self.ctx_row, b)

    def seq_edges(self, u):
        _, i = self.coords(u)
        first = (i == 0) | (i == self.ctx_tiles)
        last = (i == self.ctx_tiles - 1) | (i == self.tiles_per_batch - 1)
        return first, last

    def _latent_tile(self, u):
        b, i = self.coords(u)
        return b * self.lat_tiles + jnp.maximum(i - self.ctx_tiles, 0)

    def _context_tile(self, u):
        b, i = self.coords(u)
        return b * self.ctx_tiles + jnp.minimum(i, self.ctx_tiles - 1)

    def source_specs(self, u_of, d, split):
        tm = TOKEN_TILE
        if not split:
            return [pl.BlockSpec((tm, d), lambda j: (u_of(j), 0))]
        return [pl.BlockSpec((tm, d), lambda j: (self._context_tile(u_of(j)), 0)),
                pl.BlockSpec((tm, d), lambda j: (self._latent_tile(u_of(j)), 0))]

    def halo_specs(self, u_of, d, split):
        rb = TOKEN_TILE // SUBLANES
        if split:
            assert self.ctx_tiles == 1
            tile_of, n_rb = (lambda j: self._latent_tile(u_of(j))), self.bsz * self.lat_tiles * rb
        else:
            tile_of, n_rb = u_of, self.n_tiles * rb
        return [pl.BlockSpec((SUBLANES, d), lambda j: (jnp.maximum(tile_of(j) * rb - 1, 0), 0)),
                pl.BlockSpec((SUBLANES, d), lambda j: (jnp.minimum((tile_of(j) + 1) * rb, n_rb - 1), 0))]

    def select_source(self, u, refs):
        if len(refs) == 1:
            return refs[0][...]
        _, i = self.coords(u)
        return jnp.where(i < self.ctx_tiles, refs[0][...], refs[1][...])


def _mod_kernel(c_ref, w_ref, b_ref, o_ref):
    s = _silu(c_ref[...]).astype(BF16)
    o_ref[0] = _dot(s, w_ref[0].astype(BF16)) + b_ref[0]


def _modulation(cvec, w_mod, b_mod):
    depth, d, nd = w_mod.shape
    rows = cvec.shape[0]
    bn = 2 * d
    assert nd % bn == 0
    return pl.pallas_call(
        _mod_kernel,
        out_shape=jax.ShapeDtypeStruct((depth, rows, nd), F32),
        grid=(depth, nd // bn),
        in_specs=[
            pl.BlockSpec((rows, d), lambda l, j: (0, 0)),
            pl.BlockSpec((1, d, bn), lambda l, j: (l, 0, j)),
            pl.BlockSpec((1, 1, bn), lambda l, j: (l, 0, j)),
        ],
        out_specs=pl.BlockSpec((1, rows, bn), lambda l, j: (l, 0, j)),
        compiler_params=pltpu.CompilerParams(
            dimension_semantics=("parallel", "parallel"), vmem_limit_bytes=VMEM_LIMIT_BYTES),
        name="modulation",
    )(cvec, w_mod, b_mod.reshape(depth, 1, nd))


def _conv3_rows(a_ext, w_ref, tm):
    te = a_ext.shape[0]
    lo, hi = SUBLANES, SUBLANES + tm
    prev = pltpu.roll(a_ext, 1, 0)[lo:hi]
    nxt = pltpu.roll(a_ext, te - 1, 0)[lo:hi]
    return w_ref[0:1, :] * prev + w_ref[1:2, :] * a_ext[lo:hi] + w_ref[2:3, :] * nxt


N_INPROJ_PARAMS = 11
N_INPROJ_TOKEN_INPUTS = 5


def _split3(x):
    hi = x.astype(BF16)
    r = x - hi.astype(F32)
    mid = r.astype(BF16)
    lo = (r - mid.astype(F32)).astype(BF16)
    return hi, mid, lo


SIDE_ROWS = 2 * 2 * SSD_HEADS


def _ssd_decay_tables(dt_raw, alog_ref, dtbias_ref, cs_ref, side_ref, slot):
    tc = SSD_CHUNK
    tm = dt_raw.shape[0]
    nd = 2 * SSD_HEADS
    a = -jnp.exp(alog_ref[...])
    xb = dt_raw + dtbias_ref[...]
    dt = jnp.maximum(xb, 0.0) + jnp.log(1.0 + jnp.exp(-jnp.abs(xb)))
    la = dt * a * LOG2E
    r_i = lax.broadcasted_iota(jnp.int32, (tc, tc), 0)
    c_i = lax.broadcasted_iota(jnp.int32, (tc, tc), 1)
    tri_f = jnp.where(c_i <= r_i, 1.0, 0.0).astype(BF16)
    tri_b = jnp.where(c_i >= r_i, 1.0, 0.0).astype(BF16)
    lane = lax.broadcasted_iota(jnp.int32, (1, LANES), 1)
    fwd_lane = lane < DT_LANE0 + SSD_HEADS
    row = lax.broadcasted_iota(jnp.int32, (nd, 1), 0)
    for c in range(tm // tc):
        parts = _split3(la[c * tc:(c + 1) * tc])
        cs_f = sum(_dot(tri_f, p) for p in parts)
        cs_b = sum(_dot(tri_b, p) for p in parts)
        cs = jnp.where(fwd_lane, cs_f, cs_b)
        cs_ref[slot * tm + c * tc:slot * tm + (c + 1) * tc, :] = cs
        cs_t = cs.T[DT_LANE0:DT_LANE0 + nd]
        dt_t = dt[c * tc:(c + 1) * tc].T[DT_LANE0:DT_LANE0 + nd]
        cs_end = jnp.where(row < SSD_HEADS, cs_t[:, tc - 1:tc], cs_t[:, 0:1])
        w_t = jnp.exp2(cs_end - cs_t) * dt_t
        side_ref[slot * (tm // tc) + c] = jnp.concatenate([cs_t - jnp.log(dt_t) * LOG2E, w_t], axis=0)


def _rotate_cols(w):
    a, b, c, d = jnp.split(w, 4, axis=-1)
    return jnp.concatenate([-b, a, -d, c], axis=-1)


def _relayout_weights(win_raw, wuq_raw, wukv_raw, win_ref, wuq_ref, wukv_ref):
    d = win_raw.shape[1]
    w_kr = win_raw[Q_LORA + KV_LORA:IN_MLA, :]
    w_dt = win_raw[IN_MLA + IN_SC + SSD_WIDTH + SSD_XBC:, :]
    a, b, c, e = (w_kr[i * SUBLANES:(i + 1) * SUBLANES] for i in range(4))
    kr_block = jnp.concatenate([
        -b, a, -e, c,
        w_dt, jnp.zeros((LANES // 2 - QK_ROPE - w_dt.shape[0], d), F32),
        w_kr, jnp.zeros((LANES // 2 - QK_ROPE, d), F32)], axis=0)
    win_ref[:, C_KR0:C_SC0] = kr_block.T.astype(BF16)
    step = 2 * LANES
    for src0, dst0, n_rows in ((0, C_Q0, Q_LORA + KV_LORA), (IN_MLA, C_SC0, IN_SC + SSD_WIDTH + SSD_XBC)):
        for off in range(0, n_rows, step):
            rows = min(step, n_rows - off)
            blk = win_raw[src0 + off:src0 + off + rows, :]
            win_ref[:, dst0 + off:dst0 + off + rows] = blk.T.astype(BF16)

    dqk = QK_NOPE + QK_ROPE
    wq = wuq_raw[...]
    nq = wq.shape[0]
    plain, rot = [], []
    for h in range(MLA_HEADS):
        wh = wq[:, h * dqk:(h + 1) * dqk]
        pad = jnp.zeros((nq, HEAD_PAD - dqk), F32)
        plain += [wh, pad]
        rot += [jnp.zeros((nq, QK_NOPE), F32), _rotate_cols(wh[:, QK_NOPE:]), pad]
    wuq_ref[...] = jnp.concatenate(plain + rot, axis=1).astype(BF16)

    dkv = QK_NOPE + V_HEAD
    wkv = wukv_raw[...]
    nkv = wkv.shape[0]
    ks, vs = [], []
    for h in range(MLA_HEADS):
        wh = wkv[:, h * dkv:(h + 1) * dkv]
        ks += [wh[:, :QK_NOPE], jnp.zeros((nkv, HEAD_PAD - QK_NOPE), F32)]
        vs.append(wh[:, QK_NOPE:])
    wukv_ref[...] = jnp.concatenate(ks + vs, axis=1).astype(BF16)


def _inproj_kernel(*refs, layout, split, tiles):
    n_x = 2 if split else 1
    per_half = n_x + N_INPROJ_TOKEN_INPUTS
    n_in = tiles * per_half
    (gpre, win_raw, qg, wuq_raw, kvg, wukv_raw, scw, cw, cb, alog, dtbias) = [
        r.at[0] for r in refs[n_in:n_in + N_INPROJ_PARAMS]]
    n_out = len(refs) - n_in - N_INPROJ_PARAMS - 3
    outs = refs[n_in + N_INPROJ_PARAMS:n_in + N_INPROJ_PARAMS + n_out]
    win_ref, wuq_ref, wukv_ref = refs[n_in + N_INPROJ_PARAMS + n_out:]

    @pl.when(pl.program_id(0) == 0)
    def _():
        _relayout_weights(win_raw, wuq_raw, wukv_raw, win_ref, wuq_ref, wukv_ref)

    params = (qg, wuq_ref, kvg, wukv_ref, scw, cw, cb, alog, dtbias)
    for h in range(tiles):
        half = refs[h * per_half:(h + 1) * per_half]
        u = pl.program_id(0) * tiles + h
        x = layout.select_source(u, half[:n_x])
        xp_ref, xn_ref, mod_ref, cos_ref, sin_ref = half[n_x:]
        h_ext = _inproj_normalise(x, xp_ref, xn_ref, mod_ref.at[0], gpre, layout.seq_edges(u))
        u_ext = _dot(h_ext, win_ref[...])
        _inproj_finish(u_ext, cos_ref, sin_ref, params, outs, h)


def _inproj_normalise(x, xp_ref, xn_ref, mod_ref, gpre_ref, seq_edges):
    first_of_seq, last_of_seq = seq_edges
    shift = mod_ref[0, 0:1, :]
    scale = mod_ref[0, 1:2, :]

    def norm_mod(xx):
        return _rms(xx, gpre_ref[...]) * (1.0 + scale) + shift

    h_prev = jnp.where(first_of_seq, 0.0, norm_mod(xp_ref[...]))
    h_next = jnp.where(last_of_seq, 0.0, norm_mod(xn_ref[...]))
    return jnp.concatenate([h_prev, norm_mod(x), h_next], axis=0).astype(BF16)


def _inproj_finish(u_ext, cos_ref, sin_ref, params, outs, slot):
    qg_ref, wuq_ref, kvg_ref, wukv_ref, scw_ref, cw_ref, cb_ref, alog_ref, dtbias_ref = params
    qt_ref, k_ref, vt_ref, kn_ref, qn_ref, ysc_ref, z_ref, act_ref, cs_ref, side_ref = outs
    tm = u_ext.shape[0] - 2 * SUBLANES
    rows = slice(slot * tm, (slot + 1) * tm)
    own = slice(SUBLANES, SUBLANES + tm)

    z_ref[rows, :] = u_ext[own, C_Z0:C_XBC0].astype(z_ref.dtype)
    lane = lax.broadcasted_iota(jnp.int32, (1, LANES), 1)
    krb = u_ext[own, C_KR0:C_SC0]
    dt_raw = jnp.where((lane >= DT_LANE0) & (lane < DT_LANE0 + 2 * SSD_HEADS), krb, 0.0)
    _ssd_decay_tables(dt_raw, alog_ref, dtbias_ref, cs_ref, side_ref, slot)

    conv = _conv3_rows(u_ext[:, C_XBC0:D_IN_PAD], cw_ref, tm) + cb_ref[...]
    act_ref[rows, :] = _silu(conv).astype(act_ref.dtype)

    w = SC_WIDTH
    prod = u_ext[:, C_SC0 + w:C_SC0 + 2 * w] * u_ext[:, C_SC0 + 2 * w:C_Z0]
    ysc_ref[rows, :] = (u_ext[own, C_SC0:C_SC0 + w] * _conv3_rows(prod, scw_ref, tm)).astype(ysc_ref.dtype)

    cos = cos_ref[...]
    sin = sin_ref[...]
    nh = MLA_HEADS
    cos_h = jnp.concatenate([cos] * nh, axis=1)
    sin_h = jnp.concatenate([sin] * nh, axis=1)

    def head_norm_maxima(a_f32):
        out = []
        for h in range(nh):
            a_h = a_f32[:, h * HEAD_PAD:(h + 1) * HEAD_PAD]
            n2 = jnp.max(jnp.sum(a_h * a_h, axis=1, keepdims=True), axis=0, keepdims=True)
            out.append(jnp.broadcast_to(n2, (1, LANES)))
        return jnp.concatenate(out + [jnp.zeros((SUBLANES - nh, LANES), F32)], axis=0)

    cq = _rms(u_ext[own, C_Q0:C_KV0], qg_ref[...]).astype(BF16)
    q2 = _dot(cq, wuq_ref[...])
    qw = nh * HEAD_PAD
    q_b = ((q2[:, :qw] * cos_h + q2[:, qw:] * sin_h) * Q_PRESCALE).astype(BF16)
    q_f = q_b.astype(F32)
    qt_ref[:, rows] = q_f.T.astype(BF16)
    qn_ref[slot] = head_norm_maxima(q_f)

    ckv = _rms(u_ext[own, C_KV0:C_KR0], kvg_ref[...]).astype(BF16)
    kv = _dot(ckv, wukv_ref[...])
    rope_lane = (lane >= ROPE_LANE0) & (lane < ROPE_LANE0 + QK_ROPE)
    cos_k = jnp.where(rope_lane, cos, 0.0)
    kr = krb * cos_k + pltpu.roll(krb, LANES // 2, 1) * sin
    k = kv[:, :qw] + jnp.concatenate([kr] * nh, axis=1)
    k_b = k.astype(BF16)
    k_ref[rows, :] = k_b
    kn_ref[slot] = head_norm_maxima(k_b.astype(F32))
    v_t = kv[:, qw:].T.astype(BF16)
    for h in range(nh):
        vt_ref[h, 0:V_HEAD, rows] = v_t[h * V_HEAD:(h + 1) * V_HEAD]
        vt_ref[h, V_HEAD:V_ROWS, rows] = jnp.ones((V_ROWS - V_HEAD, tm), BF16)


def _layer_spec(a, layer, single_buffer=False):
    block = (1,) + a.shape[1:]
    index_map = lambda j: (layer,) + (0,) * (a.ndim - 1)
    if single_buffer:
        return pl.BlockSpec(block, index_map, pipeline_mode=pl.Buffered(1))
    return pl.BlockSpec(block, index_map)


def _inproj(src, mod, layer, gpre, win, qg, wuq, kvg, wukv, cos_t, sin_t, scw, cw, cb, alog, dtbias, layout):
    tm = TOKEN_TILE
    d = src[0].shape[-1]
    split = len(src) == 2
    qw = MLA_HEADS * HEAD_PAD
    nt = layout.n_tiles
    tiles = INPROJ_TILES_PER_STEP
    assert nt % tiles == 0
    rows = nt * tm

    in_specs, operands = [], []
    for h in range(tiles):
        def u_of(j, h=h):
            return j * tiles + h

        def rope(j, u_of=u_of):
            return (layout.coords(u_of(j))[1], 0)

        in_specs += layout.source_specs(u_of, d, split) + layout.halo_specs(u_of, d, split)
        in_specs += [
            pl.BlockSpec((1, 1, N_MOD, d), lambda j, u_of=u_of: (layer, layout.mod_row(u_of(j)), 0, 0)),
            pl.BlockSpec((tm, LANES), rope), pl.BlockSpec((tm, LANES), rope),
        ]
        operands += list(src) + [src[-1], src[-1], mod, cos_t, sin_t]
    consts = [gpre, win, qg, wuq, kvg, wukv, scw, cw, cb, alog, dtbias]
    in_specs += [_layer_spec(c, layer, single_buffer=c is win) for c in consts]
    step_rows = tiles * tm
    chunks_per_step = step_rows // SSD_CHUNK

    def tok(width):
        return pl.BlockSpec((step_rows, width), lambda j: (j, 0))

    norm_shape = jax.ShapeDtypeStruct((nt, SUBLANES, LANES), F32)
    norm_spec = pl.BlockSpec((tiles, SUBLANES, LANES), lambda j: (j, 0, 0))
    out_shape = (
        jax.ShapeDtypeStruct((qw, rows), BF16),
        jax.ShapeDtypeStruct((rows, qw), BF16),
        jax.ShapeDtypeStruct((MLA_HEADS, V_ROWS, rows), BF16),
        norm_shape,
        norm_shape,
        jax.ShapeDtypeStruct((rows, SC_WIDTH), BF16),
        jax.ShapeDtypeStruct((rows, SSD_WIDTH), BF16),
        jax.ShapeDtypeStruct((rows, SSD_XBC), BF16),
        jax.ShapeDtypeStruct((rows, LANES), F32),
        jax.ShapeDtypeStruct((rows // SSD_CHUNK, SIDE_ROWS, SSD_CHUNK), F32),
    )
    return pl.pallas_call(
        functools.partial(_inproj_kernel, layout=layout, split=split, tiles=tiles),
        out_shape=out_shape,
        grid=(nt // tiles,),
        in_specs=in_specs,
        out_specs=(pl.BlockSpec((qw, step_rows), lambda j: (0, j)), tok(qw),
                   pl.BlockSpec((MLA_HEADS, V_ROWS, step_rows), lambda j: (0, 0, j)),
                   norm_spec, norm_spec,
                   tok(SC_WIDTH), tok(SSD_WIDTH), tok(SSD_XBC), tok(LANES),
                   pl.BlockSpec((chunks_per_step, SIDE_ROWS, SSD_CHUNK), lambda j: (j, 0, 0))),
        scratch_shapes=[pltpu.VMEM((d, D_IN_PAD), BF16),
                        pltpu.VMEM((Q_LORA, 2 * qw), BF16),
                        pltpu.VMEM((KV_LORA, qw + MLA_WIDTH), BF16)],
        compiler_params=pltpu.CompilerParams(
            dimension_semantics=("arbitrary",), vmem_limit_bytes=VMEM_LIMIT_BYTES),
        name="inproj",
    )(*operands, *consts)


def _attn_kernel(*refs, n_keys, n_ctx, ctx_queries, q_tiles):
    qt_refs = refs[:q_tiles]
    k_ref, vt_ref, kn_ref = refs[q_tiles:q_tiles + 3]
    qn_refs = refs[q_tiles + 3:2 * q_tiles + 3]
    o_ref, s_ref, p_ref = refs[2 * q_tiles + 3:]
    i = pl.program_id(1)
    kn = jnp.max(kn_ref[0], axis=0)
    for t in range(q_tiles):
        _attend_tile(qt_refs[t], qn_refs[t], k_ref, vt_ref, kn, o_ref, t, s_ref, p_ref, i,
                     n_keys=n_keys, n_ctx=n_ctx, ctx_queries=ctx_queries)


def _attend_tile(q_t, qn_ref, k_ref, vt_ref, kn, o_ref, slot, s_ref, p_ref, i, *, n_keys, n_ctx, ctx_queries):
    tq = q_t.shape[1]

    def finish(outs):
        o_ref[0, slot * tq:(slot + 1) * tq, :] = jnp.concatenate(outs, axis=0).T.astype(o_ref.dtype)

    def head_out(ov):
        return ov[0:V_HEAD] / ov[V_HEAD:V_HEAD + 1]

    def attend_two_pass(nk):
        def scores(h):
            s_ref[h, 0:nk, :] = _dot(k_ref[0, 0:nk, h * HEAD_PAD:(h + 1) * HEAD_PAD],
                                     q_t[h * HEAD_PAD:(h + 1) * HEAD_PAD, :])

        def probs(h):
            s = s_ref[h, 0:nk, :]
            p_ref[h, 0:nk, :] = jnp.exp2(s - jnp.max(s, axis=0, keepdims=True)).astype(BF16)

        def values(h):
            return head_out(_dot(vt_ref[h, :, 0:nk], p_ref[h, 0:nk, :]))

        scores(0), scores(1)
        scores(2), scores(3), probs(0), probs(1)
        outs = [values(0), values(1)]
        probs(2), probs(3)
        outs += [values(2), values(3)]
        finish(outs)

    def attend_one_pass(nk, shift):
        outs = []
        for pair in range(MLA_HEADS // 2):
            heads = (2 * pair, 2 * pair + 1)
            s = [_dot(k_ref[0, 0:nk, h * HEAD_PAD:(h + 1) * HEAD_PAD],
                      q_t[h * HEAD_PAD:(h + 1) * HEAD_PAD, :]) for h in heads]
            for h, sh in zip(heads, s):
                p_ref[h, 0:nk, :] = jnp.exp2(sh - shift[h]).astype(BF16)
            outs += [head_out(_dot(vt_ref[h, :, 0:nk], p_ref[h, 0:nk, :])) for h in heads]
        finish(outs)

    def attend(nk):
        bound = jnp.sqrt(qn_ref[0] * kn) * BOUND_SLACK
        one_pass = jnp.max(bound) <= SHIFT_MAX_BOUND

        @pl.when(one_pass)
        def _():
            attend_one_pass(nk, [bound[h:h + 1, 0:1] - SHIFT_HEADROOM for h in range(MLA_HEADS)])

        @pl.when(jnp.logical_not(one_pass))
        def _():
            attend_two_pass(nk)

    if ctx_queries:
        @pl.when(i == 0)
        def _():
            attend_two_pass(n_ctx)

        @pl.when(i > 0)
        def _():
            attend(n_keys)
    else:
        attend(n_keys)


def _attention(qt, k, vt, kn, qn, n_ctx, ctx_queries):
    bsz, t, qw = k.shape
    tq = TOKEN_TILE
    tiles = t // tq
    t0 = 0 if ctx_queries else n_ctx // tq
    nq = tiles - t0
    q_tiles = 1 if ctx_queries else 2
    assert nq % q_tiles == 0

    def tile_of(b, i, s):
        return b * tiles + i * q_tiles + s + t0

    return pl.pallas_call(
        functools.partial(_attn_kernel, n_keys=t, n_ctx=n_ctx, ctx_queries=ctx_queries, q_tiles=q_tiles),
        out_shape=jax.ShapeDtypeStruct((bsz, nq * tq, MLA_WIDTH), BF16),
        grid=(bsz, nq // q_tiles),
        in_specs=[pl.BlockSpec((qw, tq), lambda b, i, s=s: (0, tile_of(b, i, s))) for s in range(q_tiles)] + [
            pl.BlockSpec((1, t, qw), lambda b, i: (b, 0, 0)),
            pl.BlockSpec((MLA_HEADS, V_ROWS, t), lambda b, i: (0, 0, b)),
            pl.BlockSpec((1,) + kn.shape[1:], lambda b, i: (b, 0, 0, 0)),
        ] + [pl.BlockSpec((1, SUBLANES, LANES), lambda b, i, s=s: (tile_of(b, i, s), 0, 0)) for s in range(q_tiles)],
        out_specs=pl.BlockSpec((1, q_tiles * tq, MLA_WIDTH), lambda b, i: (b, i, 0)),
        scratch_shapes=[pltpu.VMEM((MLA_HEADS, t, tq), F32), pltpu.VMEM((MLA_HEADS, t, tq), BF16)],
        compiler_params=pltpu.CompilerParams(
            dimension_semantics=("parallel", "parallel"), vmem_limit_bytes=VMEM_LIMIT_BYTES),
        name="attention",
    )(*([qt] * q_tiles), k, vt, kn, *([qn] * q_tiles))


PAIRS_PER_CHUNK = SSD_HEADS // 2


def _ssd_build_lhs(act, cs, side, direction, lhs_ref, base):
    tc = SSD_CHUNK
    n = SSD_STATE
    bm = act[:, SSD_WIDTH:SSD_WIDTH + SSD_GN].astype(F32)
    cm = act[:, SSD_WIDTH + SSD_GN:]
    bm_t = bm.T

    r_i = lax.broadcasted_iota(jnp.int32, (tc, tc), 0)
    c_i = lax.broadcasted_iota(jnp.int32, (tc, tc), 1)
    tri = (c_i <= r_i) if direction == 0 else (c_i >= r_i)
    nd = 2 * SSD_HEADS
    csd_t, w_t = side[0:nd], side[nd:2 * nd]
    cm_f = cm.astype(F32)
    zeros_lhs = jnp.zeros((n, tc), BF16)
    m = tc + n
    for g in range(SSD_GROUPS):
        cb = _dot(cm[:, g * n:(g + 1) * n], bm_t[g * n:(g + 1) * n, :].astype(BF16))
        bt_g = bm_t[g * n:(g + 1) * n, :]
        for pr in range(HEADS_PER_GROUP // 2):
            slab = g * (HEADS_PER_GROUP // 2) + pr
            for hh in range(2):
                r = direction * SSD_HEADS + 2 * slab + hh
                col = DT_LANE0 + r
                cs_col = jnp.broadcast_to(cs[:, col:col + 1], (tc, tc))
                lmd = jnp.exp2(jnp.where(tri, cs_col - csd_t[r:r + 1, :], -jnp.inf))
                lhs_ref[base + slab, hh * m:hh * m + tc, 0:tc] = (cb * lmd).astype(BF16)
                lhs_ref[base + slab, hh * m:hh * m + tc, tc:2 * tc] = (cm_f * jnp.exp2(cs_col)).astype(BF16)
                lhs_ref[base + slab, hh * m + tc:(hh + 1) * m, 0:tc] = (bt_g * w_t[r:r + 1, :]).astype(BF16)
                lhs_ref[base + slab, hh * m + tc:(hh + 1) * m, tc:2 * tc] = zeros_lhs


def _ssd_apply(act, cs, direction, dskip_ref, y_ref, row0, h_ref, lhs_ref, base):
    tc = SSD_CHUNK
    n = SSD_STATE
    m = tc + n
    end = tc - 1 if direction == 0 else 0
    dec_tot = jnp.exp2(cs[end:end + 1, :])
    lane = lax.broadcasted_iota(jnp.int32, (1, LANES), 1)
    low_half = lane < SSD_HEAD_DIM
    zeros_h = jnp.zeros((n, LANES), BF16)
    for g in range(SSD_GROUPS):
        for pr in range(HEADS_PER_GROUP // 2):
            slab = g * (HEADS_PER_GROUP // 2) + pr
            h_pair = h_ref[direction, g, :, pr * LANES:(pr + 1) * LANES]
            h_b = h_pair.astype(BF16)
            xs_pair = act[:, slab * LANES:(slab + 1) * LANES]
            rhs = jnp.concatenate([xs_pair] + [h_b if gg == g else zeros_h for gg in range(SSD_GROUPS)], axis=0)
            out = _dot(lhs_ref[base + slab], rhs)
            y_pair = jnp.where(low_half, out[0:tc], out[m:m + tc])
            if direction == 0:
                y_pair = y_pair + dskip_ref[:, slab * LANES:(slab + 1) * LANES] * xs_pair.astype(F32)
            y_ref[row0:row0 + tc, slab * LANES:(slab + 1) * LANES] = y_pair
            h_new = jnp.where(low_half, out[tc:m], out[m + tc:2 * m])
            cols = [DT_LANE0 + direction * SSD_HEADS + 2 * slab + hh for hh in range(2)]
            keep = jnp.where(low_half, dec_tot[:, cols[0]:cols[0] + 1], dec_tot[:, cols[1]:cols[1] + 1])
            h_ref[direction, g, :, pr * LANES:(pr + 1) * LANES] = h_pair * keep + h_new


def _bwd_block(i, n_blocks, ctx_blocks):
    return jnp.where(i < ctx_blocks, ctx_blocks - 1 - i, n_blocks - 1 + ctx_blocks - i)


def _ssd_kernel(af_ref, ab_ref, csf_ref, csb_ref, sidef_ref, sideb_ref, dskip_ref, yf_ref, yb_ref,
                h_ref, lhs_ref):
    @pl.when(pl.program_id(1) == 0)
    def _():
        h_ref[...] = jnp.zeros_like(h_ref)

    dskip = dskip_ref.at[0]
    tc = SSD_CHUNK
    n_sub = af_ref.shape[1] // tc
    roles = []
    for bb in range(af_ref.shape[0]):
        for j in range(n_sub):
            jb = n_sub - 1 - j
            roles.append((bb, af_ref, csf_ref, sidef_ref, j, 0, yf_ref))
            roles.append((bb, ab_ref, csb_ref, sideb_ref, jb, 1, yb_ref))
    for idx, (bb, a_ref, c_ref, s_ref, j, direction, _) in enumerate(roles):
        r0 = j * tc
        _ssd_build_lhs(a_ref[bb, r0:r0 + tc, :], c_ref[bb, r0:r0 + tc, :], s_ref[bb, j], direction,
                       lhs_ref, idx * PAIRS_PER_CHUNK)
    for idx, (bb, a_ref, c_ref, _, j, direction, y_ref) in enumerate(roles):
        r0 = j * tc
        _ssd_apply(a_ref[bb, r0:r0 + tc, :], c_ref[bb, r0:r0 + tc, :], direction, dskip, y_ref.at[bb], r0,
                   h_ref.at[bb], lhs_ref, idx * PAIRS_PER_CHUNK)


def _ssd_scan(act, cs, side, layer, dskip, n_ctx):
    bsz, t, _ = act.shape
    tc = SSD_BLOCK
    n_blocks = t // tc
    ctx_blocks = n_ctx // tc
    assert n_ctx % tc == 0 and t % tc == 0

    def fwd(i):
        return i

    def bwd(i):
        return _bwd_block(i, n_blocks, ctx_blocks)

    bps = SSD_BATCH_PER_STEP
    assert bsz % bps == 0

    def main(order, width):
        return pl.BlockSpec((bps, tc, width), lambda b, i: (b, order(i), 0))

    def side_spec(order):
        return pl.BlockSpec((bps, tc // SSD_CHUNK, SIDE_ROWS, SSD_CHUNK), lambda b, i: (b, order(i), 0, 0))

    def const(shape):
        return pl.BlockSpec((1,) + shape[1:], lambda b, i: (layer,) + (0,) * (len(shape) - 1))

    y_shape = jax.ShapeDtypeStruct((bsz, t, SSD_WIDTH), F32)
    return pl.pallas_call(
        _ssd_kernel,
        out_shape=(y_shape, y_shape),
        grid=(bsz // bps, n_blocks),
        in_specs=[
            main(fwd, SSD_XBC), main(bwd, SSD_XBC), main(fwd, LANES), main(bwd, LANES),
            side_spec(fwd), side_spec(bwd),
            const(dskip.shape),
        ],
        out_specs=(main(fwd, SSD_WIDTH), main(bwd, SSD_WIDTH)),
        scratch_shapes=[pltpu.VMEM((bps, 2, SSD_GROUPS, SSD_STATE, HEADS_PER_GROUP * SSD_HEAD_DIM), F32),
                        pltpu.VMEM((bps * 2 * (tc // SSD_CHUNK) * PAIRS_PER_CHUNK, 2 * (SSD_CHUNK + SSD_STATE),
                                    2 * SSD_CHUNK), BF16)],
        compiler_params=pltpu.CompilerParams(
            dimension_semantics=("arbitrary", "arbitrary"), vmem_limit_bytes=VMEM_LIMIT_BYTES),
        name="ssd_scan",
    )(act, act, cs, cs, side, side, dskip)


def _mix_gather(att_ref, ysc_ref, z_ref, yf_ref, yb_ref, normg_ref, ycat_ref):
    gated = (yf_ref[...] + yb_ref[...]) * _silu(z_ref[...].astype(F32))
    gw = SSD_WIDTH // SSD_GROUPS
    w = SC_WIDTH
    ycat_ref[:, 0:MLA_WIDTH] = att_ref[...]
    ycat_ref[:, MLA_WIDTH:MLA_WIDTH + w] = ysc_ref[...]
    c0 = MLA_WIDTH + w
    for g in range(SSD_GROUPS):
        gg = gated[:, g * gw:(g + 1) * gw]
        gg = gg * lax.rsqrt(jnp.mean(gg * gg, axis=-1, keepdims=True) + EPS) * normg_ref[:, g * gw:(g + 1) * gw]
        ycat_ref[:, c0 + g * gw:c0 + (g + 1) * gw] = gg.astype(BF16)


def _mix_residual(x, mod_ref, y_ref, gpost_ref, gpre2_ref, x1_ref, h2_ref):
    gate1 = mod_ref[0, 2:3, :]
    shift2 = mod_ref[0, 3:4, :]
    scale2 = mod_ref[0, 4:5, :]
    x1 = x + gate1 * _rms(y_ref[...], gpost_ref[...])
    x1_ref[...] = x1
    h2_ref[...] = (_rms(x1, gpre2_ref[...]) * (1.0 + scale2) + shift2).astype(BF16)


def _mix_mlp(h2_ref, w1_ref, w2_ref, acc_ref):
    d_ff = w1_ref.shape[1]
    for c in range(d_ff // FF_CHUNK):
        a = _dot(h2_ref[...], w1_ref[:, c * FF_CHUNK:(c + 1) * FF_CHUNK])
        r = jnp.square(jnp.maximum(a, 0.0)).astype(BF16)
        part = _dot(r, w2_ref[c * FF_CHUNK:(c + 1) * FF_CHUNK, :])
        if c == 0:
            acc_ref[...] = part
        else:
            acc_ref[...] += part


N_MIX_PARAMS = 7
N_MIX_TOKEN_INPUTS = 6


def _mix_ffn_kernel(*refs, layout, latent_only, split):
    n_x = 2 if split else 1
    per_half = n_x + N_MIX_TOKEN_INPUTS
    n_in = TILES_PER_STEP * per_half
    normg, wout, gpost, gpre2, w1, w2, gpost2 = [r.at[0] for r in refs[n_in:n_in + N_MIX_PARAMS]]
    o_ref = refs[n_in + N_MIX_PARAMS]
    scratch = refs[n_in + N_MIX_PARAMS + 1:]
    per_tile = len(scratch) // TILES_PER_STEP
    tiles = [scratch[h * per_tile:(h + 1) * per_tile] for h in range(TILES_PER_STEP)]
    tm = TOKEN_TILE
    halves = [refs[h * per_half:(h + 1) * per_half] for h in range(TILES_PER_STEP)]
    mods = [half[n_x].at[0] for half in halves]
    for half, (ycat_ref, y_ref, _, _, _) in zip(halves, tiles):
        _mix_gather(*half[n_x + 1:], normg, ycat_ref)
        y_ref[...] = _dot(ycat_ref[...], wout[...])
    for h, (half, (_, y_ref, acc_ref, x1_ref, h2_ref)) in enumerate(zip(halves, tiles)):
        u = layout.unified_tile(pl.program_id(0) * TILES_PER_STEP + h, latent_only)
        x = layout.select_source(u, half[:n_x])
        _mix_residual(x, mods[h], y_ref, gpost, gpre2, x1_ref, h2_ref)
        _mix_mlp(h2_ref, w1, w2, acc_ref)
    for h, (_, _, acc_ref, x1_ref, _) in enumerate(tiles):
        gate2 = mods[h][0, 5:6, :]
        o_ref[h * tm:(h + 1) * tm, :] = x1_ref[...] + gate2 * _rms(acc_ref[...], gpost2[...])


def _mix_ffn(src, mod, layer, att, ysc, z, yf, yb, normg, wout, gpost, gpre2, w1, w2, gpost2, layout,
             latent_only):
    tm = TOKEN_TILE
    d = src[0].shape[-1]
    n_out_tiles = layout.bsz * (layout.lat_tiles if latent_only else layout.tiles_per_batch)
    assert n_out_tiles % TILES_PER_STEP == 0
    att_unified = att.shape[0] == layout.bsz * layout.tiles_per_batch * tm

    in_specs, operands = [], []
    for h in range(TILES_PER_STEP):
        def u_of(j, h=h):
            return layout.unified_tile(j * TILES_PER_STEP + h, latent_only)

        def tok(width, tile_of=u_of):
            return pl.BlockSpec((tm, width), lambda j: (tile_of(j), 0))

        in_specs += layout.source_specs(u_of, d, len(src) == 2)
        in_specs += [
            pl.BlockSpec((1, 1, N_MOD, d), lambda j, u_of=u_of: (layer, layout.mod_row(u_of(j)), 0, 0)),
            tok(MLA_WIDTH) if att_unified else tok(MLA_WIDTH, lambda j, h=h: j * TILES_PER_STEP + h),
            tok(SC_WIDTH), tok(SSD_WIDTH), tok(SSD_WIDTH), tok(SSD_WIDTH),
        ]
        operands += list(src) + [mod, att, ysc, z, yf, yb]
    consts = [normg, wout, gpost, gpre2, w1, w2, gpost2]
    in_specs += [_layer_spec(c, layer, single_buffer=True) for c in consts]
    return pl.pallas_call(
        functools.partial(_mix_ffn_kernel, layout=layout, latent_only=latent_only, split=len(src) == 2),
        out_shape=jax.ShapeDtypeStruct((n_out_tiles * tm, d), F32),
        grid=(n_out_tiles // TILES_PER_STEP,),
        in_specs=in_specs,
        out_specs=pl.BlockSpec((TILES_PER_STEP * tm, d), lambda j: (j, 0)),
        scratch_shapes=[pltpu.VMEM((tm, d), BF16), pltpu.VMEM((tm, d), F32), pltpu.VMEM((tm, d), F32),
                        pltpu.VMEM((tm, d), F32), pltpu.VMEM((tm, d), BF16)] * TILES_PER_STEP,
        compiler_params=pltpu.CompilerParams(
            dimension_semantics=("parallel",), vmem_limit_bytes=VMEM_LIMIT_BYTES),
        name="mix_ffn",
    )(*operands, *consts)


def _rope_tables(n_ctx, seq):
    f32 = np.float32
    half = QK_ROPE // 2
    inv_freq = (f32(ROPE_THETA) ** (-np.arange(0, half, 2, dtype=f32) / f32(half))).astype(f32)
    rows = seq // GRID_W
    row = np.repeat(np.arange(rows, dtype=f32), GRID_W)
    col = np.tile(np.arange(GRID_W, dtype=f32), rows)
    ang_r = row[:, None] * inv_freq
    ang_c = col[:, None] * inv_freq
    ang = np.concatenate([ang_r, ang_r, ang_c, ang_c], axis=-1).astype(f32)
    t = n_ctx + seq
    cos_t = np.ones((t, LANES), f32)
    sin_t = np.zeros((t, LANES), f32)
    cos_t[n_ctx:, ROPE_LANE0:ROPE_LANE0 + QK_ROPE] = np.cos(ang)
    sin_t[n_ctx:, ROPE_LANE0:ROPE_LANE0 + QK_ROPE] = np.sin(ang)
    return jnp.asarray(cos_t), jnp.asarray(sin_t)


def kernel(x, c, ctx, c_ctx, w_mod, b_mod, g_pre_mix, w_in, mla_q_norm, w_uq, mla_kv_norm, w_ukv, sc_conv_w, ssd_conv_w, ssd_conv_b, ssd_a_log, ssd_dt_bias, ssd_d, ssd_norm, w_out, g_post_mix, g_pre_ffn, w_ff1, w_ff2, g_post_ffn):
    bsz, seq, d = x.shape
    n_ctx = ctx.shape[1]
    depth = w_mod.shape[0]
    assert n_ctx == TOKEN_TILE and seq % TOKEN_TILE == 0 and seq % GRID_W == 0
    layout = _TokenLayout(bsz, n_ctx, seq)
    t = n_ctx + seq

    cvec = jnp.zeros((SUBLANES, d), F32).at[:bsz].set(c).at[layout.ctx_row].set(c_ctx)
    mod_all = _modulation(cvec, w_mod, b_mod).reshape(depth, SUBLANES, N_MOD, d)
    cos_t, sin_t = _rope_tables(n_ctx, seq)

    src = (ctx.reshape(bsz * n_ctx, d), x.reshape(bsz * seq, d))

    def rows3(a):
        return a.reshape(depth, 1, -1)

    def dt_lanes(a):
        flat = a.reshape(depth, 1, -1)
        return jnp.pad(flat, ((0, 0), (0, 0), (DT_LANE0, LANES - DT_LANE0 - flat.shape[-1])))

    alog, dtbias = dt_lanes(ssd_a_log), dt_lanes(ssd_dt_bias)
    dskip = rows3(jnp.repeat(ssd_d, SSD_HEAD_DIM, axis=-1))
    w_out_b, w_ff1_b, w_ff2_b = w_out.astype(BF16), w_ff1.astype(BF16), w_ff2.astype(BF16)
    w_in_t = jnp.swapaxes(w_in, 1, 2)

    def per_batch(a):
        return a.reshape((bsz, t) + a.shape[1:])

    for i in range(depth):
        last = i == depth - 1
        qt, k, vt, kn, qn, ysc, z, act, cs, side = _inproj(
            src, mod_all, i, rows3(g_pre_mix), w_in_t, rows3(mla_q_norm), w_uq, rows3(mla_kv_norm), w_ukv,
            cos_t, sin_t, sc_conv_w, ssd_conv_w, rows3(ssd_conv_b), alog, dtbias, layout)
        att = _attention(qt, per_batch(k), vt, kn.reshape(bsz, layout.tiles_per_batch, SUBLANES, LANES), qn,
                         n_ctx, ctx_queries=not last)
        yf, yb = _ssd_scan(per_batch(act), per_batch(cs), side.reshape(bsz, t // SSD_CHUNK, SIDE_ROWS, SSD_CHUNK),
                           i, dskip, n_ctx)
        out = _mix_ffn(src, mod_all, i, att.reshape(-1, MLA_WIDTH), ysc, z,
                       yf.reshape(-1, SSD_WIDTH), yb.reshape(-1, SSD_WIDTH), rows3(ssd_norm),
                       w_out_b, rows3(g_post_mix), rows3(g_pre_ffn), w_ff1_b, w_ff2_b, rows3(g_post_ffn),
                       layout, latent_only=last)
        src = (out,)
    return out.reshape(bsz, seq, d)
```
